```python
import math
import jax, jax.numpy as jnp
from jax import lax
import numpy as np

D_MODEL = 1024
BATCH = 8
SEQ = 4096
DEPTH = 1

N_HEADS = 8
HEAD_DIM = D_MODEL // 16
ATTN_WIDTH = N_HEADS * HEAD_DIM
ROT_DIM = HEAD_DIM // 4
ROPE_THETA = 500000.0
MOBA_BLOCK = 256
MOBA_TOPK = 3
Q_CHUNK = 128
CONV_WIDTH = D_MODEL // 2
CONV_K = 3
IN_WIDTH = 3 * ATTN_WIDTH + 3 * CONV_WIDTH + 2 * D_MODEL
N_EXPERTS = 256
TOP_K = 8
N_GROUPS = 8
TOPK_GROUPS = 4
EXPERT_DIM = 256
SHARED_DIM = 256
ROUTED_SCALE = 2.5
MOE_BLOCK = 128
PLE_DIM = 256
LN_EPS = 1e-5
DN_ALPHA = (2 * DEPTH) ** 0.25
DN_BETA = (8 * DEPTH) ** -0.25
NEG = -1e30

kernel_name = 'hybrid_moba_shortconv_moe_deepnorm'


def layer_norm(t, g, b):
    tf = t.astype(jnp.float32)
    mu = jnp.mean(tf, axis=-1, keepdims=True)
    var = jnp.mean(jnp.square(tf - mu), axis=-1, keepdims=True)
    out = (tf - mu) * lax.rsqrt(var + LN_EPS) * g.astype(jnp.float32) + b.astype(jnp.float32)
    return out.astype(t.dtype)


def partial_rope(t, pos):
    half = ROT_DIM // 2
    inv_freq = ROPE_THETA ** (-jnp.arange(0, ROT_DIM, 2, dtype=jnp.float32) / ROT_DIM)
    ang = pos.astype(jnp.float32)[:, None] * inv_freq[None, :]
    cos, sin = jnp.cos(ang), jnp.sin(ang)
    tr = t[..., :ROT_DIM].astype(jnp.float32)
    t1, t2 = tr[..., :half], tr[..., half:]
    rot = jnp.concatenate([t1 * cos - t2 * sin, t1 * sin + t2 * cos], axis=-1).astype(t.dtype)
    return jnp.concatenate([rot, t[..., ROT_DIM:]], axis=-1)


def moba_attention(q, k, v):
    B, H, S, Dh = q.shape
    nb = S // MOBA_BLOCK
    nc = S // Q_CHUNK
    k_eff = min(MOBA_TOPK, nb)
    kb = k.reshape(B, H, nb, MOBA_BLOCK, Dh)
    vb = v.reshape(B, H, nb, MOBA_BLOCK, Dh)
    k_mean = jnp.mean(kb.astype(jnp.float32), axis=3)
    qc = q.reshape(B, H, nc, Q_CHUNK, Dh).transpose(0, 2, 1, 3, 4).reshape(B * nc, H, Q_CHUNK, Dh)
    flat = jnp.arange(B * nc, dtype=jnp.int32)
    b_ids = flat // nc
    c_ids = flat % nc
    scale = HEAD_DIM ** -0.5
    head_ids = jnp.arange(H)[:, None, None]

    def chunk(args):
        qi, bi, ci = args
        kb_b, vb_b, km_b = kb[bi], vb[bi], k_mean[bi]
        q_pos = ci * Q_CHUNK + jnp.arange(Q_CHUNK, dtype=jnp.int32)
        own = (ci * Q_CHUNK) // MOBA_BLOCK
        gate = jnp.einsum('hcd,hnd->hcn', qi.astype(jnp.float32), km_b)
        past = jnp.arange(nb)[None, None, :] < own
        gate = jnp.where(past, gate, -jnp.inf)
        _, sel = lax.top_k(gate, k_eff)
        sel_ok = sel < own
        k_sel = kb_b[head_ids, sel]
        v_sel = vb_b[head_ids, sel]
        k_own = lax.dynamic_index_in_dim(kb_b, own, axis=1, keepdims=False)
        v_own = lax.dynamic_index_in_dim(vb_b, own, axis=1, keepdims=False)
        s_sel = jnp.einsum('hcd,hckld->hckl', qi, k_sel).astype(jnp.float32) * scale
        s_sel = jnp.where(sel_ok[..., None], s_sel, NEG)
        k_pos = own * MOBA_BLOCK + jnp.arange(MOBA_BLOCK, dtype=jnp.int32)
        s_own = jnp.einsum('hcd,hld->hcl', qi, k_own).astype(jnp.float32) * scale
        s_own = jnp.where(k_pos[None, None, :] <= q_pos[None, :, None], s_own, NEG)
        logits = jnp.concatenate([s_sel.reshape(H, Q_CHUNK, k_eff * MOBA_BLOCK), s_own], axis=-1)
        probs = jax.nn.softmax(logits, axis=-1).astype(v.dtype)
        p_sel = probs[..., :k_eff * MOBA_BLOCK].reshape(H, Q_CHUNK, k_eff, MOBA_BLOCK)
        p_own = probs[..., k_eff * MOBA_BLOCK:]
        return (jnp.einsum('hckl,hckld->hcd', p_sel, v_sel)
                + jnp.einsum('hcl,hld->hcd', p_own, v_own))

    out = lax.map(chunk, (qc, b_ids, c_ids))
    return out.reshape(B, nc, H, Q_CHUNK, Dh).transpose(0, 2, 1, 3, 4).reshape(B, H, S, Dh)


def short_conv(b_gate, c_gate, h, conv_w):
    u = c_gate * h
    y = lax.conv_general_dilated(u, conv_w[:, None, :].astype(u.dtype), window_strides=(1,),
                                 padding=[(CONV_K - 1, 0)],
                                 dimension_numbers=('NWC', 'WIO', 'NWC'),
                                 feature_group_count=CONV_WIDTH)
    return b_gate * y


def token_mixer(u, w_in, conv_w, w_attn_out, w_conv_out, w_out):
    B, S, _ = u.shape
    z = u @ w_in
    splits = np.cumsum([ATTN_WIDTH, ATTN_WIDTH, ATTN_WIDTH, CONV_WIDTH, CONV_WIDTH, CONV_WIDTH, D_MODEL]).tolist()
    q, k, v, cb, cc, ch, ga, gc = jnp.split(z, splits, axis=-1)
    pos = jnp.arange(S, dtype=jnp.int32)
    to_heads = lambda t: t.reshape(B, S, N_HEADS, HEAD_DIM).transpose(0, 2, 1, 3)
    q = partial_rope(to_heads(q), pos)
    k = partial_rope(to_heads(k), pos)
    v = to_heads(v)
    s_pad = -(-S // MOBA_BLOCK) * MOBA_BLOCK
    padw = ((0, 0), (0, 0), (0, s_pad - S), (0, 0))
    attn = moba_attention(jnp.pad(q, padw), jnp.pad(k, padw), jnp.pad(v, padw))[:, :, :S]
    attn = attn.transpose(0, 2, 1, 3).reshape(B, S, ATTN_WIDTH)
    y_attn = attn @ w_attn_out
    y_conv = short_conv(cb, cc, ch, conv_w) @ w_conv_out
    merged = jax.nn.sigmoid(ga) * y_attn + jax.nn.sigmoid(gc) * y_conv
    return merged @ w_out


def route(xt, w_router, router_bias):
    T = xt.shape[0]
    scores = jax.nn.sigmoid(xt.astype(jnp.float32) @ w_router.astype(jnp.float32))
    choice = scores + router_bias.astype(jnp.float32)[None, :]
    grp = choice.reshape(T, N_GROUPS, N_EXPERTS // N_GROUPS)
    grp_score = jnp.sum(lax.top_k(grp, 2)[0], axis=-1)
    _, grp_idx = lax.top_k(grp_score, TOPK_GROUPS)
    grp_mask = jnp.any(grp_idx[:, :, None] == jnp.arange(N_GROUPS)[None, None, :], axis=1)
    exp_mask = jnp.repeat(grp_mask, N_EXPERTS // N_GROUPS, axis=1)
    _, top_idx = lax.top_k(jnp.where(exp_mask, choice, -jnp.inf), TOP_K)
    top_w = jnp.take_along_axis(scores, top_idx, axis=1)
    top_w = top_w / jnp.sum(top_w, axis=-1, keepdims=True) * ROUTED_SCALE
    return top_idx, top_w


def routed_experts(xt, top_idx, top_w, w_gate, w_up, w_down):
    T, D = xt.shape
    A = T * TOP_K
    e_flat = top_idx.reshape(-1).astype(jnp.int32)
    tok_flat = jnp.arange(A, dtype=jnp.int32) // TOP_K
    w_flat = top_w.reshape(-1)
    order = jnp.argsort(e_flat)
    e_sorted = e_flat[order]
    counts = jnp.bincount(e_flat, length=N_EXPERTS).astype(jnp.int32)
    starts = jnp.cumsum(counts) - counts
    padded = (counts + MOE_BLOCK - 1) // MOE_BLOCK * MOE_BLOCK
    pad_ends = jnp.cumsum(padded)
    pad_starts = pad_ends - padded
    dest = pad_starts[e_sorted] + jnp.arange(A, dtype=jnp.int32) - starts[e_sorted]
    n_blocks = -(-A // MOE_BLOCK) + N_EXPERTS
    R = n_blocks * MOE_BLOCK
    row_tok = jnp.full((R,), T, jnp.int32).at[dest].set(tok_flat[order])
    row_w = jnp.zeros((R,), jnp.float32).at[dest].set(w_flat[order])
    block_start = jnp.arange(n_blocks, dtype=jnp.int32) * MOE_BLOCK
    block_e = jnp.minimum(jnp.searchsorted(pad_ends, block_start, side='right'), N_EXPERTS - 1)
    x_pad = jnp.concatenate([xt, jnp.zeros((1, D), xt.dtype)], axis=0)

    def block(args):
        toks, ws, e = args
        xb = x_pad[toks]
        hb = jax.nn.silu(xb @ w_gate[e]) * (xb @ w_up[e])
        return (hb @ w_down[e]).astype(jnp.float32) * ws[:, None]

    outs = lax.map(block, (row_tok.reshape(n_blocks, MOE_BLOCK), row_w.reshape(n_blocks, MOE_BLOCK), block_e))
    y = jnp.zeros((T + 1, D), jnp.float32).at[row_tok].add(outs.reshape(R, D))
    return y[:T]


def moe_ffn(u, w_router, router_bias, w_exp_gate, w_exp_up, w_exp_down, w_sh_gate, w_sh_up, w_sh_down):
    B, S, D = u.shape
    xt = u.reshape(B * S, D)
    top_idx, top_w = route(xt, w_router, router_bias)
    routed = routed_experts(xt, top_idx, top_w, w_exp_gate, w_exp_up, w_exp_down)
    shared = (jax.nn.silu(xt @ w_sh_gate) * (xt @ w_sh_up)) @ w_sh_down
    return (routed + shared.astype(jnp.float32)).astype(u.dtype).reshape(B, S, D)


def setup_inputs(seed: int = 0) -> dict:
    key = jax.random.key(seed)
    ks = jax.random.split(key, 24)
    f32 = jnp.float32

    def nrm(k, shape, scale):
        return jax.random.normal(k, shape, f32) * scale

    return {
        'x': nrm(ks[0], (BATCH, SEQ, D_MODEL), 1.0),
        'p': nrm(ks[1], (DEPTH, BATCH, SEQ, PLE_DIM), 1.0),
        'w_in': nrm(ks[2], (DEPTH, D_MODEL, IN_WIDTH), D_MODEL ** -0.5),
        'conv_w': nrm(ks[3], (DEPTH, CONV_K, CONV_WIDTH), CONV_K ** -0.5),
        'w_attn_out': nrm(ks[4], (DEPTH, ATTN_WIDTH, D_MODEL), ATTN_WIDTH ** -0.5),
        'w_conv_out': nrm(ks[5], (DEPTH, CONV_WIDTH, D_MODEL), CONV_WIDTH ** -0.5),
        'w_out': nrm(ks[6], (DEPTH, D_MODEL, D_MODEL), DN_BETA * D_MODEL ** -0.5),
        'ln1_g': 1.0 + nrm(ks[7], (DEPTH, D_MODEL), 0.02),
        'ln1_b': nrm(ks[8], (DEPTH, D_MODEL), 0.02),
        'w_router': nrm(ks[9], (DEPTH, D_MODEL, N_EXPERTS), D_MODEL ** -0.5),
        'router_bias': nrm(ks[10], (DEPTH, N_EXPERTS), 0.01),
        'w_exp_gate': nrm(ks[11], (DEPTH, N_EXPERTS, D_MODEL, EXPERT_DIM), D_MODEL ** -0.5),
        'w_exp_up': nrm(ks[12], (DEPTH, N_EXPERTS, D_MODEL, EXPERT_DIM), D_MODEL ** -0.5),
        'w_exp_down': nrm(ks[13], (DEPTH, N_EXPERTS, EXPERT_DIM, D_MODEL), DN_BETA * EXPERT_DIM ** -0.5),
        'w_sh_gate': nrm(ks[14], (DEPTH, D_MODEL, SHARED_DIM), D_MODEL ** -0.5),
        'w_sh_up': nrm(ks[15], (DEPTH, D_MODEL, SHARED_DIM), D_MODEL ** -0.5),
        'w_sh_down': nrm(ks[16], (DEPTH, SHARED_DIM, D_MODEL), DN_BETA * SHARED_DIM ** -0.5),
        'ln2_g': 1.0 + nrm(ks[17], (DEPTH, D_MODEL), 0.02),
        'ln2_b': nrm(ks[18], (DEPTH, D_MODEL), 0.02),
        'w_ple_gate': nrm(ks[19], (DEPTH, D_MODEL, D_MODEL), D_MODEL ** -0.5),
        'w_ple_proj': nrm(ks[20], (DEPTH, PLE_DIM, D_MODEL), DN_BETA * PLE_DIM ** -0.5),
        'ln3_g': 1.0 + nrm(ks[21], (DEPTH, D_MODEL), 0.02),
        'ln3_b': nrm(ks[22], (DEPTH, D_MODEL), 0.02),
    }


def reference(x, p, w_in, conv_w, w_attn_out, w_conv_out, w_out, ln1_g, ln1_b,
              w_router, router_bias, w_exp_gate, w_exp_up, w_exp_down,
              w_sh_gate, w_sh_up, w_sh_down, ln2_g, ln2_b,
              w_ple_gate, w_ple_proj, ln3_g, ln3_b):
    h = x
    for i in range(DEPTH):
        mix = token_mixer(h, w_in[i], conv_w[i], w_attn_out[i], w_conv_out[i], w_out[i])
        h = layer_norm(DN_ALPHA * h + mix, ln1_g[i], ln1_b[i])
        ffn = moe_ffn(h, w_router[i], router_bias[i], w_exp_gate[i], w_exp_up[i], w_exp_down[i],
                      w_sh_gate[i], w_sh_up[i], w_sh_down[i])
        h = layer_norm(DN_ALPHA * h + ffn, ln2_g[i], ln2_b[i])
        ple = jax.nn.sigmoid(h @ w_ple_gate[i]) * (p[i] @ w_ple_proj[i])
        h = layer_norm(DN_ALPHA * h + ple, ln3_g[i], ln3_b[i])
    return h
```

```python
import functools
import math

import jax
import jax.numpy as jnp
from jax import lax
from jax.experimental import pallas as pl
from jax.experimental.pallas import tpu as pltpu

N_HEADS = 8
HEAD_DIM = 64
ROT_DIM = 16
ROPE_THETA = 500000.0
MOBA_BLOCK = 256
MOBA_TOPK = 3
CONV_K = 3
N_EXPERTS = 256
TOP_K = 8
N_GROUPS = 8
TOPK_GROUPS = 4
ROUTED_SCALE = 2.5
LN_EPS = 1e-5
NEG = -1e30

LANES = 128
SUBLANES = 8
HEADS_PER_LANE_GROUP = LANES // HEAD_DIM
VMEM_LIMIT_BYTES = 56 * 1024 * 1024

MXU_DTYPE = jnp.bfloat16
F32 = jnp.float32
I32 = jnp.int32

TM_INPROJ = 256
TM_POST = 256
TM_DISPATCH = 256
TM_EXPERT = 256
TM_FINAL = 256


def _dot(a, b):
    return jnp.dot(a, b, preferred_element_type=F32)


def _dot_nt(a, b):
    return lax.dot_general(a, b, (((1,), (1,)), ((), ())), preferred_element_type=F32)


def _split_bf16(a):
    hi = a.astype(MXU_DTYPE)
    lo = (a - hi.astype(F32)).astype(MXU_DTYPE)
    return hi, lo


def _layer_norm(t, g, b):
    mu = jnp.mean(t, axis=-1, keepdims=True)
    d = t - mu
    var = jnp.mean(d * d, axis=-1, keepdims=True)
    return d * lax.rsqrt(var + LN_EPS) * g + b


def _silu(t):
    return t * jax.nn.sigmoid(t)


def _inproj_kernel(x_ref, w_ref, cos_ref, sa_ref, sb_ref, cw_ref,
                   q_ref, k_ref, v_ref, km_ref, yc_ref, ga_ref, gc_ref, ubuf,
                   *, tiles_per_seq, aw, cwid, d_model):
    tm = x_ref.shape[0]
    j = pl.program_id(0) % tiles_per_seq
    xb = x_ref[...].astype(MXU_DTYPE)
    cos, sa, sb = cos_ref[...], sa_ref[...], sb_ref[...]

    def proj(c0, n):
        return _dot(xb, w_ref[:, c0:c0 + n])

    def rope(z):
        parts = []
        for g in range(aw // LANES):
            zg = z[:, g * LANES:(g + 1) * LANES]
            parts.append(zg * cos
                         + pltpu.roll(zg, LANES - ROT_DIM // 2, 1) * sa
                         + pltpu.roll(zg, ROT_DIM // 2, 1) * sb)
        return jnp.concatenate(parts, axis=1)

    q_ref[...] = (rope(proj(0, aw)) * (HEAD_DIM ** -0.5)).astype(q_ref.dtype)

    k = rope(proj(aw, aw))
    k_ref[...] = k.astype(k_ref.dtype)
    nb_seq = km_ref.shape[1]
    blocks_per_tile = tm // MOBA_BLOCK

    @pl.when(j == 0)
    def _():
        km_ref[...] = jnp.zeros(km_ref.shape, km_ref.dtype)

    rows = lax.broadcasted_iota(I32, (nb_seq, aw), 0)
    km = km_ref[0]
    for bi in range(blocks_per_tile):
        mean = jnp.sum(k[bi * MOBA_BLOCK:(bi + 1) * MOBA_BLOCK], axis=0, keepdims=True) * (1.0 / MOBA_BLOCK)
        km = jnp.where(rows == j * blocks_per_tile + bi, mean, km)
    km_ref[0] = km

    v_ref[...] = proj(2 * aw, aw).astype(v_ref.dtype)

    c0 = 3 * aw
    cb = proj(c0, cwid)
    u = proj(c0 + cwid, cwid) * proj(c0 + 2 * cwid, cwid)

    @pl.when(j == 0)
    def _():
        ubuf[0:SUBLANES, :] = jnp.zeros((SUBLANES, cwid), F32)

    ubuf[SUBLANES:SUBLANES + tm, :] = u
    um1 = ubuf[SUBLANES - 1:SUBLANES - 1 + tm, :]
    um2 = ubuf[SUBLANES - 2:SUBLANES - 2 + tm, :]
    conv = cw_ref[0:1, :] * um2 + cw_ref[1:2, :] * um1 + cw_ref[2:3, :] * u
    yc_ref[...] = (cb * conv).astype(yc_ref.dtype)
    ubuf[0:SUBLANES, :] = u[tm - SUBLANES:tm, :]

    g0 = c0 + 3 * cwid
    ga_ref[...] = jax.nn.sigmoid(proj(g0, d_model))
    gc_ref[...] = jax.nn.sigmoid(proj(g0 + d_model, d_model))


def _inproj(x2, w_in_b, cos_t, sa_t, sb_t, conv_w, *, seq, aw, cwid):
    T, D = x2.shape
    tm = TM_INPROJ
    assert seq % tm == 0 and tm % MOBA_BLOCK == 0 and T % seq == 0
    tiles_per_seq = seq // tm
    nb_seq = seq // MOBA_BLOCK
    nbatch = T // seq
    row = lambda i: (i, 0)
    tab = lambda i: (i % tiles_per_seq, 0)
    kern = functools.partial(_inproj_kernel, tiles_per_seq=tiles_per_seq, aw=aw, cwid=cwid, d_model=D)
    return pl.pallas_call(
        kern,
        grid=(T // tm,),
        in_specs=[
            pl.BlockSpec((tm, D), row),
            pl.BlockSpec(w_in_b.shape, lambda i: (0, 0)),
            pl.BlockSpec((tm, LANES), tab),
            pl.BlockSpec((tm, LANES), tab),
            pl.BlockSpec((tm, LANES), tab),
            pl.BlockSpec(conv_w.shape, lambda i: (0, 0)),
        ],
        out_specs=[
            pl.BlockSpec((tm, aw), row),
            pl.BlockSpec((tm, aw), row),
            pl.BlockSpec((tm, aw), row),
            pl.BlockSpec((1, nb_seq, aw), lambda i: (i // tiles_per_seq, 0, 0)),
            pl.BlockSpec((tm, cwid), row),
            pl.BlockSpec((tm, D), row),
            pl.BlockSpec((tm, D), row),
        ],
        out_shape=[
            jax.ShapeDtypeStruct((T, aw), MXU_DTYPE),
            jax.ShapeDtypeStruct((T, aw), MXU_DTYPE),
            jax.ShapeDtypeStruct((T, aw), MXU_DTYPE),
            jax.ShapeDtypeStruct((nbatch, nb_seq, aw), F32),
            jax.ShapeDtypeStruct((T, cwid), MXU_DTYPE),
            jax.ShapeDtypeStruct((T, D), F32),
            jax.ShapeDtypeStruct((T, D), F32),
        ],
        scratch_shapes=[pltpu.VMEM((tm + SUBLANES, cwid), F32)],
        compiler_params=pltpu.CompilerParams(
            dimension_semantics=("arbitrary",), vmem_limit_bytes=VMEM_LIMIT_BYTES),
        name="inproj",
    )(x2, w_in_b, cos_t, sa_t, sb_t, conv_w)


def _attn_kernel(q_ref, k_ref, v_ref, km_ref, o_ref, vt_ref, sel_ref):
    qb = pl.program_id(2)
    blk = MOBA_BLOCK
    nb = km_ref.shape[1]

    @pl.when(qb == 0)
    def _():
        for b in range(nb):
            vt_ref[b] = v_ref[0, b * blk:(b + 1) * blk, :].astype(F32).T.astype(vt_ref.dtype)

    q2 = q_ref[0]
    lane = lax.broadcasted_iota(I32, q2.shape, 1)
    km_hi, km_lo = _split_bf16(km_ref[0])
    blk_id = lax.broadcasted_iota(I32, (nb, blk), 0).astype(F32)
    past = blk_id < qb.astype(F32)
    key_pos = lax.broadcasted_iota(I32, (blk, blk), 0)
    qry_pos = lax.broadcasted_iota(I32, (blk, blk), 1)
    own_start = pl.multiple_of(qb * blk, blk)

    outs = []
    for h in range(HEADS_PER_LANE_GROUP):
        in_head = (lane >= h * HEAD_DIM) & (lane < (h + 1) * HEAD_DIM)
        qh = jnp.where(in_head, q2, jnp.zeros_like(q2))

        gate = jnp.where(past, _dot_nt(km_hi, qh) + _dot_nt(km_lo, qh), -jnp.inf)
        sel = jnp.zeros((nb, blk), F32)
        for _ in range(MOBA_TOPK):
            mx = jnp.max(gate, axis=0, keepdims=True)
            idx = jnp.min(jnp.where(gate == mx, blk_id, float(nb)), axis=0, keepdims=True)
            pick = blk_id == idx
            sel = jnp.where(pick, jnp.where(past, 1.0, sel), sel)
            gate = jnp.where(pick, -jnp.inf, gate)
        sel_ref[h] = sel

        s = _dot_nt(k_ref[0, pl.ds(own_start, blk), :], qh)
        s = jnp.where(key_pos <= qry_pos, s, NEG)
        m = jnp.max(s, axis=0, keepdims=True)
        p = jnp.exp(s - m)
        l = jnp.sum(p, axis=0, keepdims=True)
        acc = _dot(vt_ref[qb, h * HEAD_DIM:(h + 1) * HEAD_DIM, :], p.astype(MXU_DTYPE))

        def body(jb, carry, qh=qh, h=h):
            m, l, acc = carry
            start = pl.multiple_of(jb * blk, blk)
            s = _dot_nt(k_ref[0, pl.ds(start, blk), :], qh)
            s = jnp.where(sel_ref[h, pl.ds(jb, 1), :] > 0.5, s, NEG)
            m_new = jnp.maximum(m, jnp.max(s, axis=0, keepdims=True))
            alpha = jnp.exp(m - m_new)
            p = jnp.exp(s - m_new)
            l = alpha * l + jnp.sum(p, axis=0, keepdims=True)
            acc = alpha * acc + _dot(vt_ref[jb, h * HEAD_DIM:(h + 1) * HEAD_DIM, :], p.astype(MXU_DTYPE))
            return m_new, l, acc

        m, l, acc = lax.fori_loop(0, qb, body, (m, l, acc))
        outs.append(acc / l)

    o_ref[0] = jnp.concatenate(outs, axis=0).T.astype(o_ref.dtype)


def _attention(q, k, v, km, *, nbatch, seq, aw):
    blk = MOBA_BLOCK
    nb = seq // blk
    assert nb % SUBLANES == 0
    q3, k3, v3 = (t.reshape(nbatch, seq, aw) for t in (q, k, v))
    out = pl.pallas_call(
        _attn_kernel,
        grid=(nbatch, aw // LANES, nb),
        in_specs=[
            pl.BlockSpec((1, blk, LANES), lambda b, g, i: (b, i, g)),
            pl.BlockSpec((1, seq, LANES), lambda b, g, i: (b, 0, g)),
            pl.BlockSpec((1, seq, LANES), lambda b, g, i: (b, 0, g)),
            pl.BlockSpec((1, nb, LANES), lambda b, g, i: (b, 0, g)),
        ],
        out_specs=pl.BlockSpec((1, blk, LANES), lambda b, g, i: (b, i, g)),
        out_shape=jax.ShapeDtypeStruct((nbatch, seq, aw), MXU_DTYPE),
        scratch_shapes=[
            pltpu.VMEM((nb, LANES, blk), MXU_DTYPE),
            pltpu.VMEM((HEADS_PER_LANE_GROUP, nb, blk), F32),
        ],
        compiler_params=pltpu.CompilerParams(
            dimension_semantics=("arbitrary", "arbitrary", "arbitrary"),
            vmem_limit_bytes=VMEM_LIMIT_BYTES),
        name="moba_attn",
    )(q3, k3, v3, km)
    return out.reshape(nbatch * seq, aw)


def _post_kernel(attn_ref, yc_ref, ga_ref, gc_ref, x_ref, wao_ref, wco_ref, wo_ref,
                 g1_ref, b1_ref, wrh_ref, wrl_ref, rb_ref,
                 h1_ref, ti_ref, tw_ref, rk_ref, cnt_ref, cnt_acc, *, alpha):
    i = pl.program_id(0)
    tm = x_ref.shape[0]
    ne = wrh_ref.shape[0]

    y_attn = _dot(attn_ref[...], wao_ref[...])
    y_conv = _dot(yc_ref[...], wco_ref[...])
    merged = ga_ref[...] * y_attn + gc_ref[...] * y_conv
    mix = _dot(merged.astype(MXU_DTYPE), wo_ref[...])
    h1 = _layer_norm(alpha * x_ref[...] + mix, g1_ref[...], b1_ref[...])
    h1_ref[...] = h1

    h_hi, h_lo = _split_bf16(h1)
    logits = _dot_nt(wrh_ref[...], h_hi) + _dot_nt(wrh_ref[...], h_lo) + _dot_nt(wrl_ref[...], h_hi)
    scores = jax.nn.sigmoid(logits)
    choice = scores + rb_ref[...]

    gsz = ne // N_GROUPS
    gshape = (N_GROUPS, gsz, tm)
    c3 = choice.reshape(gshape)
    in_grp = lax.broadcasted_iota(I32, gshape, 1).astype(F32)
    m1 = jnp.max(c3, axis=1, keepdims=True)
    i1 = jnp.min(jnp.where(c3 == m1, in_grp, float(gsz)), axis=1, keepdims=True)
    m2 = jnp.max(jnp.where(in_grp == i1, -jnp.inf, c3), axis=1, keepdims=True)
    gscore = jnp.broadcast_to(m1 + m2, gshape).reshape(ne, tm)

    eid = lax.broadcasted_iota(I32, (ne, tm), 0).astype(F32)
    gid = lax.broadcasted_iota(I32, gshape, 0).astype(F32).reshape(ne, tm)
    cand = jnp.full((ne, tm), -jnp.inf, F32)
    for _ in range(TOPK_GROUPS):
        mx = jnp.max(gscore, axis=0, keepdims=True)
        idx = jnp.min(jnp.where(gscore == mx, gid, float(N_GROUPS)), axis=0, keepdims=True)
        pick = gid == idx
        cand = jnp.where(pick, choice, cand)
        gscore = jnp.where(pick, -jnp.inf, gscore)

    selmat = jnp.zeros((ne, tm), F32)
    idxs, svals = [], []
    for _ in range(TOP_K):
        mx = jnp.max(cand, axis=0, keepdims=True)
        idx = jnp.min(jnp.where(cand == mx, eid, float(ne)), axis=0, keepdims=True)
        pick = eid == idx
        svals.append(jnp.sum(jnp.where(pick, scores, 0.0), axis=0, keepdims=True))
        idxs.append(idx)
        selmat = jnp.where(pick, 1.0, selmat)
        cand = jnp.where(pick, -jnp.inf, cand)
    ssum = svals[0]
    for r in range(1, TOP_K):
        ssum = ssum + svals[r]

    @pl.when(i == 0)
    def _():
        cnt_acc[...] = jnp.zeros(cnt_acc.shape, F32)

    tr = lax.broadcasted_iota(I32, (tm, tm), 0)
    tc = lax.broadcasted_iota(I32, (tm, tm), 1)
    upper = jnp.where(tr < tc, 1.0, 0.0).astype(MXU_DTYPE)
    selb = selmat.astype(MXU_DTYPE)
    base = cnt_acc[...]
    rank = _dot(selb, upper) + jnp.concatenate([base] * (tm // LANES), axis=1)
    new_cnt = base + _dot(selb, jnp.ones((tm, LANES), MXU_DTYPE))
    cnt_acc[...] = new_cnt
    cnt_ref[...] = new_cnt

    for r in range(TOP_K):
        pick = eid == idxs[r]
        ti_ref[r:r + 1, :] = idxs[r].astype(I32)
        tw_ref[r:r + 1, :] = svals[r] / ssum * ROUTED_SCALE
        rk_ref[r:r + 1, :] = jnp.sum(jnp.where(pick, rank, 0.0), axis=0, keepdims=True).astype(I32)


def _post(attn, yc, ga, gc, x2, wao, wco, wo, g1, b1, wrh, wrl, rb, *, alpha):
    T, D = x2.shape
    tm = TM_POST
    aw, cwid = attn.shape[1], yc.shape[1]
    ne = wrh.shape[0]
    row = lambda i: (i, 0)
    full = lambda i: (0, 0)
    col = lambda i: (0, i)
    return pl.pallas_call(
        functools.partial(_post_kernel, alpha=alpha),
        grid=(T // tm,),
        in_specs=[
            pl.BlockSpec((tm, aw), row), pl.BlockSpec((tm, cwid), row),
            pl.BlockSpec((tm, D), row), pl.BlockSpec((tm, D), row), pl.BlockSpec((tm, D), row),
            pl.BlockSpec(wao.shape, full), pl.BlockSpec(wco.shape, full), pl.BlockSpec(wo.shape, full),
            pl.BlockSpec(g1.shape, full), pl.BlockSpec(b1.shape, full),
            pl.BlockSpec(wrh.shape, full), pl.BlockSpec(wrl.shape, full), pl.BlockSpec(rb.shape, full),
        ],
        out_specs=[
            pl.BlockSpec((tm, D), row),
            pl.BlockSpec((TOP_K, tm), col), pl.BlockSpec((TOP_K, tm), col), pl.BlockSpec((TOP_K, tm), col),
            pl.BlockSpec((ne, LANES), full),
        ],
        out_shape=[
            jax.ShapeDtypeStruct((T, D), F32),
            jax.ShapeDtypeStruct((TOP_K, T), I32),
            jax.ShapeDtypeStruct((TOP_K, T), F32),
            jax.ShapeDtypeStruct((TOP_K, T), I32),
            jax.ShapeDtypeStruct((ne, LANES), F32),
        ],
        scratch_shapes=[pltpu.VMEM((ne, LANES), F32)],
        compiler_params=pltpu.CompilerParams(
            dimension_semantics=("arbitrary",), vmem_limit_bytes=VMEM_LIMIT_BYTES),
        name="post_route",
    )(attn, yc, ga, gc, x2, wao, wco, wo, g1, b1, wrh, wrl, rb)


def _row_copy(src_hbm, dst_hbm, src_row, dst_row, sem):
    return pltpu.make_async_copy(src_hbm.at[pl.ds(src_row, 1)], dst_hbm.at[pl.ds(dst_row, 1)], sem)


def _dispatch_kernel(pstart_ref, zrow_ref, nzero_ref, ti_ref, rk_ref, h1_hbm,
                     xs_hbm, dest_ref, zbuf, zsem, sem):
    i = pl.program_id(0)
    tm = ti_ref.shape[1]
    te = zbuf.shape[0]

    def zero_copy(z):
        return pltpu.make_async_copy(zbuf, xs_hbm.at[pl.ds(pl.multiple_of(zrow_ref[z], te), te)], zsem)

    @pl.when(i == 0)
    def _():
        zbuf[...] = jnp.zeros(zbuf.shape, zbuf.dtype)
        nz = nzero_ref[0]
        lax.fori_loop(0, nz, lambda z, c: (zero_copy(z).start(), c)[1], 0)
        lax.fori_loop(0, nz, lambda z, c: (zero_copy(z).wait(), c)[1], 0)

    def slot(k, j):
        return pstart_ref[ti_ref[k, j]] + rk_ref[k, j]

    def issue(j, c):
        for k in range(TOP_K):
            d = slot(k, j)
            dest_ref[k, j] = d
            _row_copy(h1_hbm, xs_hbm, i * tm + j, d, sem).start()
        return c

    def drain(j, c):
        for k in range(TOP_K):
            _row_copy(h1_hbm, xs_hbm, i * tm + j, slot(k, j), sem).wait()
        return c

    lax.fori_loop(0, tm, issue, 0)
    lax.fori_loop(0, tm, drain, 0)


def _dispatch(pstart, zrow, nzero, topi, rnk, h1, *, n_rows):
    T, D = h1.shape
    tm = TM_DISPATCH
    smem_blk = pl.BlockSpec((TOP_K, tm), lambda i, *_: (0, i), memory_space=pltpu.SMEM)
    return pl.pallas_call(
        _dispatch_kernel,
        grid_spec=pltpu.PrefetchScalarGridSpec(
            num_scalar_prefetch=3,
            grid=(T // tm,),
            in_specs=[smem_blk, smem_blk, pl.BlockSpec(memory_space=pl.ANY)],
            out_specs=[pl.BlockSpec(memory_space=pl.ANY), smem_blk],
            scratch_shapes=[pltpu.VMEM((TM_EXPERT, D), h1.dtype),
                            pltpu.SemaphoreType.DMA, pltpu.SemaphoreType.DMA],
        ),
        out_shape=[jax.ShapeDtypeStruct((n_rows, D), h1.dtype),
                   jax.ShapeDtypeStruct((TOP_K, T), I32)],
        compiler_params=pltpu.CompilerParams(
            dimension_semantics=("arbitrary",), vmem_limit_bytes=VMEM_LIMIT_BYTES),
        name="dispatch",
    )(pstart, zrow, nzero, topi, rnk, h1)


def _experts_kernel(be_ref, nused_ref, xs_ref, wg_ref, wu_ref, wd_ref, ys_ref, wgu_b, wd_b):
    i = pl.program_id(0)
    ed = wg_ref.shape[2]

    @pl.when(i < nused_ref[0])
    def _():
        prev = be_ref[jnp.maximum(i - 1, 0)]

        @pl.when((i == 0) | (be_ref[i] != prev))
        def _():
            wgu_b[:, 0:ed] = wg_ref[0].astype(MXU_DTYPE)
            wgu_b[:, ed:2 * ed] = wu_ref[0].astype(MXU_DTYPE)
            wd_b[...] = wd_ref[0].astype(MXU_DTYPE)

        gu = _dot(xs_ref[...].astype(MXU_DTYPE), wgu_b[...])
        hb = _silu(gu[:, 0:ed]) * gu[:, ed:2 * ed]
        ys_ref[...] = _dot(hb.astype(MXU_DTYPE), wd_b[...])


def _experts(block_e, nused, xs, wg, wu, wd):
    R, D = xs.shape
    te = TM_EXPERT
    ed = wg.shape[2]
    nblk = R // te
    blk_map = lambda i, be, nu: (jnp.minimum(i, nu[0] - 1), 0)
    w_map = lambda i, be, nu: (be[i], 0, 0)
    return pl.pallas_call(
        _experts_kernel,
        grid_spec=pltpu.PrefetchScalarGridSpec(
            num_scalar_prefetch=2,
            grid=(nblk,),
            in_specs=[
                pl.BlockSpec((te, D), blk_map),
                pl.BlockSpec((1, D, ed), w_map),
                pl.BlockSpec((1, D, ed), w_map),
                pl.BlockSpec((1, ed, D), w_map),
            ],
            out_specs=pl.BlockSpec((te, D), blk_map),
            scratch_shapes=[pltpu.VMEM((D, 2 * ed), MXU_DTYPE), pltpu.VMEM((ed, D), MXU_DTYPE)],
        ),
        out_shape=jax.ShapeDtypeStruct((R, D), F32),
        compiler_params=pltpu.CompilerParams(
            dimension_semantics=("arbitrary",), vmem_limit_bytes=VMEM_LIMIT_BYTES),
        name="experts",
    )(block_e, nused, xs, wg, wu, wd)


def _final_kernel(dest_ref, tw_ref, h1_ref, p_ref, ys_hbm, wsgu_ref, wsd_ref, wpg_ref, wpp_ref,
                  g2_ref, b2_ref, g3_ref, b3_ref, o_ref, gbuf, sem, *, alpha):
    tm = h1_ref.shape[0]
    sd = wsd_ref.shape[0]

    def row_copy(k, j):
        return pltpu.make_async_copy(ys_hbm.at[pl.ds(dest_ref[k, j], 1)], gbuf.at[k, pl.ds(j, 1)], sem)

    def issue(j, c):
        for k in range(TOP_K):
            row_copy(k, j).start()
        return c

    def drain(j, c):
        for k in range(TOP_K):
            row_copy(k, j).wait()
        return c

    lax.fori_loop(0, tm, issue, 0)

    h1 = h1_ref[...]
    hb = h1.astype(MXU_DTYPE)
    gu = _dot(hb, wsgu_ref[...])
    shared = _dot((_silu(gu[:, 0:sd]) * gu[:, sd:2 * sd]).astype(MXU_DTYPE), wsd_ref[...])

    lax.fori_loop(0, tm, drain, 0)

    tw = tw_ref[...]
    wt = jnp.concatenate([tw, jnp.zeros((LANES - TOP_K, tm), F32)], axis=0).T
    routed = wt[:, 0:1] * gbuf[0]
    for k in range(1, TOP_K):
        routed = routed + wt[:, k:k + 1] * gbuf[k]

    h2 = _layer_norm(alpha * h1 + (routed + shared), g2_ref[...], b2_ref[...])
    gate = jax.nn.sigmoid(_dot(h2.astype(MXU_DTYPE), wpg_ref[...]))
    ple = gate * _dot(p_ref[...].astype(MXU_DTYPE), wpp_ref[...])
    o_ref[...] = _layer_norm(alpha * h2 + ple, g3_ref[...], b3_ref[...])


def _final(dest, topw, h1, p2, ys, wsgu, wsd, wpg, wpp, g2, b2, g3, b3, *, alpha):
    T, D = h1.shape
    tm = TM_FINAL
    row = lambda i: (i, 0)
    full = lambda i: (0, 0)
    col = lambda i: (0, i)
    return pl.pallas_call(
        functools.partial(_final_kernel, alpha=alpha),
        grid=(T // tm,),
        in_specs=[
            pl.BlockSpec((TOP_K, tm), col, memory_space=pltpu.SMEM),
            pl.BlockSpec((TOP_K, tm), col),
            pl.BlockSpec((tm, D), row),
            pl.BlockSpec((tm, p2.shape[1]), row),
            pl.BlockSpec(memory_space=pl.ANY),
            pl.BlockSpec(wsgu.shape, full), pl.BlockSpec(wsd.shape, full),
            pl.BlockSpec(wpg.shape, full), pl.BlockSpec(wpp.shape, full),
            pl.BlockSpec(g2.shape, full), pl.BlockSpec(b2.shape, full),
            pl.BlockSpec(g3.shape, full), pl.BlockSpec(b3.shape, full),
        ],
        out_specs=pl.BlockSpec((tm, D), row),
        out_shape=jax.ShapeDtypeStruct((T, D), F32),
        scratch_shapes=[pltpu.VMEM((TOP_K, tm, D), F32), pltpu.SemaphoreType.DMA],
        compiler_params=pltpu.CompilerParams(
            dimension_semantics=("arbitrary",), vmem_limit_bytes=VMEM_LIMIT_BYTES),
        name="final",
    )(dest, topw, h1, p2, ys, wsgu, wsd, wpg, wpp, g2, b2, g3, b3)


def _rope_tables(seq):
    half = ROT_DIM // 2
    inv_freq = ROPE_THETA ** (-jnp.arange(0, ROT_DIM, 2, dtype=F32) / ROT_DIM)
    ang = jnp.arange(seq, dtype=I32).astype(F32)[:, None] * inv_freq[None, :]
    cos, sin = jnp.cos(ang), jnp.sin(ang)
    ones = jnp.ones((seq, HEAD_DIM - ROT_DIM), F32)
    zeros = jnp.zeros((seq, HEAD_DIM - ROT_DIM), F32)
    zh = jnp.zeros((seq, half), F32)
    cos_h = jnp.concatenate([cos, cos, ones], axis=1)
    sa_h = jnp.concatenate([-sin, zh, zeros], axis=1)
    sb_h = jnp.concatenate([zh, sin, zeros], axis=1)
    rep = lambda t: jnp.concatenate([t] * HEADS_PER_LANE_GROUP, axis=1)
    return rep(cos_h), rep(sa_h), rep(sb_h)


def _expert_layout(counts, n_blocks):
    te = TM_EXPERT
    nblk_e = (counts + te - 1) // te
    blk_end = jnp.cumsum(nblk_e)
    blk_start = blk_end - nblk_e
    nused = blk_end[-1]
    pstart = (blk_start * te).astype(I32)
    bid = jnp.arange(n_blocks, dtype=I32)
    block_e = jnp.searchsorted(blk_end, jnp.minimum(bid, nused - 1), side='right').astype(I32)
    block_e = jnp.minimum(block_e, N_EXPERTS - 1)
    partial = (counts % te) != 0
    order = jnp.argsort(jnp.logical_not(partial), stable=True).astype(I32)
    n_partial = jnp.sum(partial).astype(I32)
    last_blk = (blk_end - 1).astype(I32)
    zi = jnp.arange(n_blocks + N_EXPERTS, dtype=I32)
    zblk = jnp.where(zi < n_partial, last_blk[order[jnp.minimum(zi, N_EXPERTS - 1)]],
                     nused + (zi - n_partial))
    nzero = n_partial + (n_blocks - nused)
    zrow = (jnp.clip(zblk, 0, n_blocks - 1) * te).astype(I32)
    return pstart, block_e, nused.astype(I32).reshape(1), zrow, nzero.astype(I32).reshape(1)


def _layer(h, p2, w_in, conv_w, w_attn_out, w_conv_out, w_out, ln1_g, ln1_b, w_router, router_bias,
           w_exp_gate, w_exp_up, w_exp_down, w_sh_gate, w_sh_up, w_sh_down, ln2_g, ln2_b,
           w_ple_gate, w_ple_proj, ln3_g, ln3_b, *, nbatch, seq, alpha):
    T, D = h.shape
    aw = w_attn_out.shape[0]
    cwid = w_conv_out.shape[0]
    bf = lambda w: w.astype(MXU_DTYPE)
    rowv = lambda g: g.reshape(1, -1)

    cos_t, sa_t, sb_t = _rope_tables(seq)
    q, k, v, km, yc, ga, gc = _inproj(h, bf(w_in), cos_t, sa_t, sb_t, conv_w, seq=seq, aw=aw, cwid=cwid)
    attn = _attention(q, k, v, km, nbatch=nbatch, seq=seq, aw=aw)

    wr_t = w_router.astype(F32).T
    wrh = wr_t.astype(MXU_DTYPE)
    wrl = (wr_t - wrh.astype(F32)).astype(MXU_DTYPE)
    h1, topi, topw, rnk, cnt = _post(
        attn, yc, ga, gc, h, bf(w_attn_out), bf(w_conv_out), bf(w_out), rowv(ln1_g), rowv(ln1_b),
        wrh, wrl, router_bias.astype(F32).reshape(-1, 1), alpha=alpha)

    n_blocks = (T * TOP_K) // TM_EXPERT + N_EXPERTS
    counts = cnt[:, 0].astype(I32)
    pstart, block_e, nused, zrow, nzero = _expert_layout(counts, n_blocks)
    xs, dest = _dispatch(pstart, zrow, nzero, topi, rnk, h1, n_rows=n_blocks * TM_EXPERT)
    ys = _experts(block_e, nused, xs, w_exp_gate, w_exp_up, w_exp_down)

    wsgu = jnp.concatenate([bf(w_sh_gate), bf(w_sh_up)], axis=1)
    return _final(dest, topw, h1, p2, ys, wsgu, bf(w_sh_down), bf(w_ple_gate), bf(w_ple_proj),
                  rowv(ln2_g), rowv(ln2_b), rowv(ln3_g), rowv(ln3_b), alpha=alpha)


def kernel(x, p, w_in, conv_w, w_attn_out, w_conv_out, w_out, ln1_g, ln1_b, w_router, router_bias,
           w_exp_gate, w_exp_up, w_exp_down, w_sh_gate, w_sh_up, w_sh_down, ln2_g, ln2_b,
           w_ple_gate, w_ple_proj, ln3_g, ln3_b):
    nbatch, seq, d_model = x.shape
    depth = w_in.shape[0]
    alpha = (2 * depth) ** 0.25
    assert seq % MOBA_BLOCK == 0
    h = x.reshape(nbatch * seq, d_model)
    for i in range(depth):
        h = _layer(h, p[i].reshape(nbatch * seq, -1), w_in[i], conv_w[i], w_attn_out[i], w_conv_out[i],
                   w_out[i], ln1_g[i], ln1_b[i], w_router[i], router_bias[i],
                   w_exp_gate[i], w_exp_up[i], w_exp_down[i], w_sh_gate[i], w_sh_up[i], w_sh_down[i],
                   ln2_g[i], ln2_b[i], w_ple_gate[i], w_ple_proj[i], ln3_g[i], ln3_b[i],
                   nbatch=nbatch, seq=seq, alpha=alpha)
    return h.reshape(nbatch, seq, d_model)
```

```python
import functools
import math

import jax
import jax.numpy as jnp
from jax import lax
from jax.experimental import pallas as pl
from jax.experimental.pallas import tpu as pltpu

N_HEADS = 8
HEAD_DIM = 64
ROT_DIM = 16
ROPE_THETA = 500000.0
MOBA_BLOCK = 256
MOBA_TOPK = 3
CONV_K = 3
N_EXPERTS = 256
TOP_K = 8
N_GROUPS = 8
TOPK_GROUPS = 4
ROUTED_SCALE = 2.5
LN_EPS = 1e-5
NEG = -1e30

LANES = 128
SUBLANES = 8
HEADS_PER_LANE_GROUP = LANES // HEAD_DIM
VMEM_LIMIT_BYTES = 56 * 1024 * 1024

MXU_DTYPE = jnp.bfloat16
F32 = jnp.float32
I32 = jnp.int32

TM_INPROJ = 256
TM_POST = 256
TM_DISPATCH = 256
TM_EXPERT = 256
TM_FINAL = 256


def _dot(a, b):
    return jnp.dot(a, b, preferred_element_type=F32)


def _dot_nt(a, b):
    return lax.dot_general(a, b, (((1,), (1,)), ((), ())), preferred_element_type=F32)


def _split_bf16(a):
    hi = a.astype(MXU_DTYPE)
    lo = (a - hi.astype(F32)).astype(MXU_DTYPE)
    return hi, lo


def _layer_norm(t, g, b):
    mu = jnp.mean(t, axis=-1, keepdims=True)
    d = t - mu
    var = jnp.mean(d * d, axis=-1, keepdims=True)
    return d * lax.rsqrt(var + LN_EPS) * g + b


def _silu(t):
    return t * jax.nn.sigmoid(t)


def _inproj_kernel(x_ref, w_ref, cos_ref, sa_ref, sb_ref, cw_ref,
                   q_ref, k_ref, v_ref, km_ref, yc_ref, ga_ref, gc_ref, ubuf,
                   *, tiles_per_seq, aw, cwid, d_model):
    tm = x_ref.shape[0]
    j = pl.program_id(0) % tiles_per_seq
    xb = x_ref[...].astype(MXU_DTYPE)
    cos, sa, sb = cos_ref[...], sa_ref[...], sb_ref[...]

    def proj(c0, n):
        return _dot(xb, w_ref[:, c0:c0 + n])

    def rope(z):
        parts = []
        for g in range(aw // LANES):
            zg = z[:, g * LANES:(g + 1) * LANES]
            parts.append(zg * cos
                         + pltpu.roll(zg, LANES - ROT_DIM // 2, 1) * sa
                         + pltpu.roll(zg, ROT_DIM // 2, 1) * sb)
        return jnp.concatenate(parts, axis=1)

    q_ref[...] = (rope(proj(0, aw)) * (HEAD_DIM ** -0.5)).astype(q_ref.dtype)

    k = rope(proj(aw, aw))
    k_ref[...] = k.astype(k_ref.dtype)
    nb_seq = km_ref.shape[1]
    blocks_per_tile = tm // MOBA_BLOCK

    @pl.when(j == 0)
    def _():
        km_ref[...] = jnp.zeros(km_ref.shape, km_ref.dtype)

    rows = lax.broadcasted_iota(I32, (nb_seq, aw), 0)
    km = km_ref[0]
    for bi in range(blocks_per_tile):
        mean = jnp.sum(k[bi * MOBA_BLOCK:(bi + 1) * MOBA_BLOCK], axis=0, keepdims=True) * (1.0 / MOBA_BLOCK)
        km = jnp.where(rows == j * blocks_per_tile + bi, mean, km)
    km_ref[0] = km

    v_ref[...] = proj(2 * aw, aw).astype(v_ref.dtype)

    c0 = 3 * aw
    cb = proj(c0, cwid)
    u = proj(c0 + cwid, cwid) * proj(c0 + 2 * cwid, cwid)

    @pl.when(j == 0)
    def _():
        ubuf[0:SUBLANES, :] = jnp.zeros((SUBLANES, cwid), F32)

    ubuf[SUBLANES:SUBLANES + tm, :] = u
    um1 = ubuf[SUBLANES - 1:SUBLANES - 1 + tm, :]
    um2 = ubuf[SUBLANES - 2:SUBLANES - 2 + tm, :]
    conv = cw_ref[0:1, :] * um2 + cw_ref[1:2, :] * um1 + cw_ref[2:3, :] * u
    yc_ref[...] = (cb * conv).astype(yc_ref.dtype)
    ubuf[0:SUBLANES, :] = u[tm - SUBLANES:tm, :]

    g0 = c0 + 3 * cwid
    ga_ref[...] = jax.nn.sigmoid(proj(g0, d_model))
    gc_ref[...] = jax.nn.sigmoid(proj(g0 + d_model, d_model))


def _inproj(x2, w_in_b, cos_t, sa_t, sb_t, conv_w, *, seq, aw, cwid):
    T, D = x2.shape
    tm = TM_INPROJ
    assert seq % tm == 0 and tm % MOBA_BLOCK == 0 and T % seq == 0
    tiles_per_seq = seq // tm
    nb_seq = seq // MOBA_BLOCK
    nbatch = T // seq
    row = lambda i: (i, 0)
    tab = lambda i: (i % tiles_per_seq, 0)
    kern = functools.partial(_inproj_kernel, tiles_per_seq=tiles_per_seq, aw=aw, cwid=cwid, d_model=D)
    return pl.pallas_call(
        kern,
        grid=(T // tm,),
        in_specs=[
            pl.BlockSpec((tm, D), row),
            pl.BlockSpec(w_in_b.shape, lambda i: (0, 0)),
            pl.BlockSpec((tm, LANES), tab),
            pl.BlockSpec((tm, LANES), tab),
            pl.BlockSpec((tm, LANES), tab),
            pl.BlockSpec(conv_w.shape, lambda i: (0, 0)),
        ],
        out_specs=[
            pl.BlockSpec((tm, aw), row),
            pl.BlockSpec((tm, aw), row),
            pl.BlockSpec((tm, aw), row),
            pl.BlockSpec((1, nb_seq, aw), lambda i: (i // tiles_per_seq, 0, 0)),
            pl.BlockSpec((tm, cwid), row),
            pl.BlockSpec((tm, D), row),
            pl.BlockSpec((tm, D), row),
        ],
        out_shape=[
            jax.ShapeDtypeStruct((T, aw), MXU_DTYPE),
            jax.ShapeDtypeStruct((T, aw), MXU_DTYPE),
            jax.ShapeDtypeStruct((T, aw), MXU_DTYPE),
            jax.ShapeDtypeStruct((nbatch, nb_seq, aw), F32),
            jax.ShapeDtypeStruct((T, cwid), MXU_DTYPE),
            jax.ShapeDtypeStruct((T, D), F32),
            jax.ShapeDtypeStruct((T, D), F32),
        ],
        scratch_shapes=[pltpu.VMEM((tm + SUBLANES, cwid), F32)],
        compiler_params=pltpu.CompilerParams(
            dimension_semantics=("arbitrary",), vmem_limit_bytes=VMEM_LIMIT_BYTES),
        name="inproj",
    )(x2, w_in_b, cos_t, sa_t, sb_t, conv_w)


def _attn_kernel(q_ref, k_ref, v_ref, km_ref, o_ref, vt_ref, sel_ref):
    qb = pl.program_id(2)
    blk = MOBA_BLOCK
    nb = km_ref.shape[1]

    @pl.when(qb == 0)
    def _():
        for b in range(nb):
            vt_ref[b] = v_ref[0, b * blk:(b + 1) * blk, :].astype(F32).T.astype(vt_ref.dtype)

    q2 = q_ref[0]
    km_hi, km_lo = _split_bf16(km_ref[0])
    blk_id = lax.broadcasted_iota(I32, (nb, blk), 0).astype(F32)
    past = blk_id < qb.astype(F32)
    key_pos = lax.broadcasted_iota(I32, (blk, blk), 0)
    qry_pos = lax.broadcasted_iota(I32, (blk, blk), 1)
    own_start = pl.multiple_of(qb * blk, blk)

    heads = range(HEADS_PER_LANE_GROUP)
    q2t = q2.astype(F32).T
    dim = lax.broadcasted_iota(I32, q2t.shape, 0)
    qts = []
    for h in heads:
        in_head = (dim >= h * HEAD_DIM) & (dim < (h + 1) * HEAD_DIM)
        qt = jnp.where(in_head, q2t, 0.0).astype(MXU_DTYPE)
        qts.append(qt)

        gate = jnp.where(past, _dot(km_hi, qt) + _dot(km_lo, qt), -jnp.inf)
        sel = jnp.zeros((nb, blk), F32)
        for _ in range(MOBA_TOPK):
            mx = jnp.max(gate, axis=0, keepdims=True)
            idx = jnp.min(jnp.where(gate == mx, blk_id, float(nb)), axis=0, keepdims=True)
            pick = blk_id == idx
            sel = jnp.where(pick, jnp.where(past, 1.0, sel), sel)
            gate = jnp.where(pick, -jnp.inf, gate)
        sel_ref[h] = sel

    def v_t(jb, h):
        return vt_ref[jb, h * HEAD_DIM:(h + 1) * HEAD_DIM, :]

    k_own = k_ref[0, pl.ds(own_start, blk), :]
    scores = [_dot(k_own, qts[h]) for h in heads]
    state = []
    for h in heads:
        s = jnp.where(key_pos <= qry_pos, scores[h], NEG)
        m = jnp.max(s, axis=0, keepdims=True)
        p = jnp.exp(s - m)
        state += [m, jnp.sum(p, axis=0, keepdims=True), _dot(v_t(qb, h), p.astype(MXU_DTYPE))]

    def make_step(nk):
        def step(b0, carry):
            kb = k_ref[0, pl.ds(pl.multiple_of(b0 * blk, blk), nk * blk), :]
            scores = [_dot(kb, qts[h]) for h in heads]
            new = []
            for h in heads:
                m, l, acc = carry[3 * h:3 * h + 3]
                parts = [jnp.where(sel_ref[h, pl.ds(b0 + t, 1), :] > 0.5,
                                   scores[h][t * blk:(t + 1) * blk], NEG) for t in range(nk)]
                s = jnp.concatenate(parts, axis=0) if nk > 1 else parts[0]
                m_new = jnp.maximum(m, jnp.max(s, axis=0, keepdims=True))
                alpha = jnp.exp(m - m_new)
                p = jnp.exp(s - m_new)
                vt = [v_t(b0 + t, h) for t in range(nk)]
                vt = jnp.concatenate(vt, axis=1) if nk > 1 else vt[0]
                new += [m_new, alpha * l + jnp.sum(p, axis=0, keepdims=True),
                        alpha * acc + _dot(vt, p.astype(MXU_DTYPE))]
            return tuple(new)
        return step

    quad, pair, single = make_step(4), make_step(2), make_step(1)
    n_quads = lax.shift_right_logical(qb, 2)
    state = lax.fori_loop(0, n_quads, lambda i, c: quad(4 * i, c), tuple(state))
    state = lax.fori_loop(0, lax.shift_right_logical(qb, 1) & 1, lambda i, c: pair(4 * n_quads, c), state)
    state = lax.fori_loop(0, qb & 1, lambda i, c: single(qb - 1, c), state)
    outs = [state[3 * h + 2] / state[3 * h + 1] for h in heads]
    o_ref[0] = jnp.concatenate(outs, axis=0).T.astype(o_ref.dtype)


def _attention(q, k, v, km, *, nbatch, seq, aw):
    blk = MOBA_BLOCK
    nb = seq // blk
    assert nb % SUBLANES == 0
    q3, k3, v3 = (t.reshape(nbatch, seq, aw) for t in (q, k, v))
    out = pl.pallas_call(
        _attn_kernel,
        grid=(nbatch, aw // LANES, nb),
        in_specs=[
            pl.BlockSpec((1, blk, LANES), lambda b, g, i: (b, i, g)),
            pl.BlockSpec((1, seq, LANES), lambda b, g, i: (b, 0, g)),
            pl.BlockSpec((1, seq, LANES), lambda b, g, i: (b, 0, g)),
            pl.BlockSpec((1, nb, LANES), lambda b, g, i: (b, 0, g)),
        ],
        out_specs=pl.BlockSpec((1, blk, LANES), lambda b, g, i: (b, i, g)),
        out_shape=jax.ShapeDtypeStruct((nbatch, seq, aw), MXU_DTYPE),
        scratch_shapes=[
            pltpu.VMEM((nb, LANES, blk), MXU_DTYPE),
            pltpu.VMEM((HEADS_PER_LANE_GROUP, nb, blk), F32),
        ],
        compiler_params=pltpu.CompilerParams(
            dimension_semantics=("arbitrary", "arbitrary", "arbitrary"),
            vmem_limit_bytes=VMEM_LIMIT_BYTES),
        name="moba_attn",
    )(q3, k3, v3, km)
    return out.reshape(nbatch * seq, aw)


def _post_kernel(attn_ref, yc_ref, ga_ref, gc_ref, x_ref, wao_ref, wco_ref, wo_ref,
                 g1_ref, b1_ref, wrh_ref, wrl_ref, rb_ref,
                 h1_ref, ti_ref, tw_ref, rk_ref, cnt_ref, cnt_acc, *, alpha):
    i = pl.program_id(0)
    tm = x_ref.shape[0]
    ne = wrh_ref.shape[0]

    y_attn = _dot(attn_ref[...], wao_ref[...])
    y_conv = _dot(yc_ref[...], wco_ref[...])
    merged = ga_ref[...] * y_attn + gc_ref[...] * y_conv
    mix = _dot(merged.astype(MXU_DTYPE), wo_ref[...])
    h1 = _layer_norm(alpha * x_ref[...] + mix, g1_ref[...], b1_ref[...])
    h1_ref[...] = h1

    h_hi, h_lo = _split_bf16(h1)
    logits = _dot_nt(wrh_ref[...], h_hi) + _dot_nt(wrh_ref[...], h_lo) + _dot_nt(wrl_ref[...], h_hi)
    scores = jax.nn.sigmoid(logits)
    choice = scores + rb_ref[...]

    gsz = ne // N_GROUPS
    gshape = (N_GROUPS, gsz, tm)
    c3 = choice.reshape(gshape)
    in_grp = lax.broadcasted_iota(I32, gshape, 1).astype(F32)
    m1 = jnp.max(c3, axis=1, keepdims=True)
    i1 = jnp.min(jnp.where(c3 == m1, in_grp, float(gsz)), axis=1, keepdims=True)
    m2 = jnp.max(jnp.where(in_grp == i1, -jnp.inf, c3), axis=1, keepdims=True)
    gscore = jnp.broadcast_to(m1 + m2, gshape).reshape(ne, tm)

    eid = lax.broadcasted_iota(I32, (ne, tm), 0).astype(F32)
    gid = lax.broadcasted_iota(I32, gshape, 0).astype(F32).reshape(ne, tm)
    cand = jnp.full((ne, tm), -jnp.inf, F32)
    for _ in range(TOPK_GROUPS):
        mx = jnp.max(gscore, axis=0, keepdims=True)
        idx = jnp.min(jnp.where(gscore == mx, gid, float(N_GROUPS)), axis=0, keepdims=True)
        pick = gid == idx
        cand = jnp.where(pick, choice, cand)
        gscore = jnp.where(pick, -jnp.inf, gscore)

    selmat = jnp.zeros((ne, tm), F32)
    idxs, svals = [], []
    for _ in range(TOP_K):
        mx = jnp.max(cand, axis=0, keepdims=True)
        idx = jnp.min(jnp.where(cand == mx, eid, float(ne)), axis=0, keepdims=True)
        pick = eid == idx
        svals.append(jnp.sum(jnp.where(pick, scores, 0.0), axis=0, keepdims=True))
        idxs.append(idx)
        selmat = jnp.where(pick, 1.0, selmat)
        cand = jnp.where(pick, -jnp.inf, cand)
    ssum = svals[0]
    for r in range(1, TOP_K):
        ssum = ssum + svals[r]

    @pl.when(i == 0)
    def _():
        cnt_acc[...] = jnp.zeros(cnt_acc.shape, F32)

    tr = lax.broadcasted_iota(I32, (tm, tm), 0)
    tc = lax.broadcasted_iota(I32, (tm, tm), 1)
    upper = jnp.where(tr < tc, 1.0, 0.0).astype(MXU_DTYPE)
    selb = selmat.astype(MXU_DTYPE)
    base = cnt_acc[...]
    rank = _dot(selb, upper) + jnp.concatenate([base] * (tm // LANES), axis=1)
    new_cnt = base + _dot(selb, jnp.ones((tm, LANES), MXU_DTYPE))
    cnt_acc[...] = new_cnt
    cnt_ref[...] = new_cnt

    for r in range(TOP_K):
        pick = eid == idxs[r]
        ti_ref[r:r + 1, :] = idxs[r].astype(I32)
        tw_ref[r:r + 1, :] = svals[r] / ssum * ROUTED_SCALE
        rk_ref[r:r + 1, :] = jnp.sum(jnp.where(pick, rank, 0.0), axis=0, keepdims=True).astype(I32)


def _post(attn, yc, ga, gc, x2, wao, wco, wo, g1, b1, wrh, wrl, rb, *, alpha):
    T, D = x2.shape
    tm = TM_POST
    aw, cwid = attn.shape[1], yc.shape[1]
    ne = wrh.shape[0]
    row = lambda i: (i, 0)
    full = lambda i: (0, 0)
    col = lambda i: (0, i)
    return pl.pallas_call(
        functools.partial(_post_kernel, alpha=alpha),
        grid=(T // tm,),
        in_specs=[
            pl.BlockSpec((tm, aw), row), pl.BlockSpec((tm, cwid), row),
            pl.BlockSpec((tm, D), row), pl.BlockSpec((tm, D), row), pl.BlockSpec((tm, D), row),
            pl.BlockSpec(wao.shape, full), pl.BlockSpec(wco.shape, full), pl.BlockSpec(wo.shape, full),
            pl.BlockSpec(g1.shape, full), pl.BlockSpec(b1.shape, full),
            pl.BlockSpec(wrh.shape, full), pl.BlockSpec(wrl.shape, full), pl.BlockSpec(rb.shape, full),
        ],
        out_specs=[
            pl.BlockSpec((tm, D), row),
            pl.BlockSpec((TOP_K, tm), col), pl.BlockSpec((TOP_K, tm), col), pl.BlockSpec((TOP_K, tm), col),
            pl.BlockSpec((ne, LANES), full),
        ],
        out_shape=[
            jax.ShapeDtypeStruct((T, D), F32),
            jax.ShapeDtypeStruct((TOP_K, T), I32),
            jax.ShapeDtypeStruct((TOP_K, T), F32),
            jax.ShapeDtypeStruct((TOP_K, T), I32),
            jax.ShapeDtypeStruct((ne, LANES), F32),
        ],
        scratch_shapes=[pltpu.VMEM((ne, LANES), F32)],
        compiler_params=pltpu.CompilerParams(
            dimension_semantics=("arbitrary",), vmem_limit_bytes=VMEM_LIMIT_BYTES),
        name="post_route",
    )(attn, yc, ga, gc, x2, wao, wco, wo, g1, b1, wrh, wrl, rb)


def _row_copy(src, dst, src_row, dst_row, sem):
    return pltpu.make_async_copy(src.at[pl.ds(src_row, 1)], dst.at[pl.ds(dst_row, 1)], sem)


def _dispatch_kernel(pstart_ref, zrow_ref, nzero_ref, ti_ref, rk_ref, h1_ref,
                     xs_hbm, dest_ref, zbuf, zsem, sem):
    i = pl.program_id(0)
    tm = ti_ref.shape[1]
    te = zbuf.shape[0]

    def zero_copy(z):
        return pltpu.make_async_copy(zbuf, xs_hbm.at[pl.ds(pl.multiple_of(zrow_ref[z], te), te)], zsem)

    @pl.when(i == 0)
    def _():
        zbuf[...] = jnp.zeros(zbuf.shape, zbuf.dtype)
        nz = nzero_ref[0]
        lax.fori_loop(0, nz, lambda z, c: (zero_copy(z).start(), c)[1], 0)
        lax.fori_loop(0, nz, lambda z, c: (zero_copy(z).wait(), c)[1], 0)

    def slot(k, j):
        return pstart_ref[ti_ref[k, j]] + rk_ref[k, j]

    def issue(j, c):
        for k in range(TOP_K):
            d = slot(k, j)
            dest_ref[k, j] = d
            _row_copy(h1_ref, xs_hbm, j, d, sem).start()
        return c

    def drain(j, c):
        for k in range(TOP_K):
            _row_copy(h1_ref, xs_hbm, j, slot(k, j), sem).wait()
        return c

    lax.fori_loop(0, tm, issue, 0)
    lax.fori_loop(0, tm, drain, 0)


def _dispatch(pstart, zrow, nzero, topi, rnk, h1, *, n_rows):
    T, D = h1.shape
    tm = TM_DISPATCH
    smem_blk = pl.BlockSpec((TOP_K, tm), lambda i, *_: (0, i), memory_space=pltpu.SMEM)
    return pl.pallas_call(
        _dispatch_kernel,
        grid_spec=pltpu.PrefetchScalarGridSpec(
            num_scalar_prefetch=3,
            grid=(T // tm,),
            in_specs=[smem_blk, smem_blk, pl.BlockSpec((tm, D), lambda i, *_: (i, 0))],
            out_specs=[pl.BlockSpec(memory_space=pl.ANY), smem_blk],
            scratch_shapes=[pltpu.VMEM((TM_EXPERT, D), h1.dtype),
                            pltpu.SemaphoreType.DMA, pltpu.SemaphoreType.DMA],
        ),
        out_shape=[jax.ShapeDtypeStruct((n_rows, D), h1.dtype),
                   jax.ShapeDtypeStruct((TOP_K, T), I32)],
        compiler_params=pltpu.CompilerParams(
            dimension_semantics=("arbitrary",), vmem_limit_bytes=VMEM_LIMIT_BYTES),
        name="dispatch",
    )(pstart, zrow, nzero, topi, rnk, h1)


def _experts_kernel(be_ref, nused_ref, xs_ref, wg_ref, wu_ref, wd_ref, ys_ref, wgu_b, wd_b):
    i = pl.program_id(0)
    ed = wg_ref.shape[2]

    @pl.when(i < nused_ref[0])
    def _():
        prev = be_ref[jnp.maximum(i - 1, 0)]

        @pl.when((i == 0) | (be_ref[i] != prev))
        def _():
            wgu_b[:, 0:ed] = wg_ref[0].astype(MXU_DTYPE)
            wgu_b[:, ed:2 * ed] = wu_ref[0].astype(MXU_DTYPE)
            wd_b[...] = wd_ref[0].astype(MXU_DTYPE)

        gu = _dot(xs_ref[...].astype(MXU_DTYPE), wgu_b[...])
        hb = _silu(gu[:, 0:ed]) * gu[:, ed:2 * ed]
        ys_ref[...] = _dot(hb.astype(MXU_DTYPE), wd_b[...])


def _experts(block_e, nused, xs, wg, wu, wd):
    R, D = xs.shape
    te = TM_EXPERT
    ed = wg.shape[2]
    nblk = R // te
    blk_map = lambda i, be, nu: (jnp.minimum(i, nu[0] - 1), 0)
    w_map = lambda i, be, nu: (be[i], 0, 0)
    return pl.pallas_call(
        _experts_kernel,
        grid_spec=pltpu.PrefetchScalarGridSpec(
            num_scalar_prefetch=2,
            grid=(nblk,),
            in_specs=[
                pl.BlockSpec((te, D), blk_map),
                pl.BlockSpec((1, D, ed), w_map),
                pl.BlockSpec((1, D, ed), w_map),
                pl.BlockSpec((1, ed, D), w_map),
            ],
            out_specs=pl.BlockSpec((te, D), blk_map),
            scratch_shapes=[pltpu.VMEM((D, 2 * ed), MXU_DTYPE), pltpu.VMEM((ed, D), MXU_DTYPE)],
        ),
        out_shape=jax.ShapeDtypeStruct((R, D), F32),
        compiler_params=pltpu.CompilerParams(
            dimension_semantics=("arbitrary",), vmem_limit_bytes=VMEM_LIMIT_BYTES),
        name="experts",
    )(block_e, nused, xs, wg, wu, wd)


def _final_kernel(dest_ref, tw_ref, h1_ref, p_ref, ys_hbm, wsgu_ref, wsd_ref, wpg_ref, wpp_ref,
                  g2_ref, b2_ref, g3_ref, b3_ref, o_ref, gbuf, sem, *, alpha):
    tm = h1_ref.shape[0]
    sd = wsd_ref.shape[0]

    def row_copy(k, j):
        return pltpu.make_async_copy(ys_hbm.at[pl.ds(dest_ref[k, j], 1)], gbuf.at[k, pl.ds(j, 1)], sem)

    def issue(j, c):
        for k in range(TOP_K):
            row_copy(k, j).start()
        return c

    def drain(j, c):
        for k in range(TOP_K):
            row_copy(k, j).wait()
        return c

    lax.fori_loop(0, tm, issue, 0)

    h1 = h1_ref[...]
    hb = h1.astype(MXU_DTYPE)
    gu = _dot(hb, wsgu_ref[...])
    shared = _dot((_silu(gu[:, 0:sd]) * gu[:, sd:2 * sd]).astype(MXU_DTYPE), wsd_ref[...])

    lax.fori_loop(0, tm, drain, 0)

    tw = tw_ref[...]
    wt = jnp.concatenate([tw, jnp.zeros((LANES - TOP_K, tm), F32)], axis=0).T
    routed = wt[:, 0:1] * gbuf[0]
    for k in range(1, TOP_K):
        routed = routed + wt[:, k:k + 1] * gbuf[k]

    h2 = _layer_norm(alpha * h1 + (routed + shared), g2_ref[...], b2_ref[...])
    gate = jax.nn.sigmoid(_dot(h2.astype(MXU_DTYPE), wpg_ref[...]))
    ple = gate * _dot(p_ref[...].astype(MXU_DTYPE), wpp_ref[...])
    o_ref[...] = _layer_norm(alpha * h2 + ple, g3_ref[...], b3_ref[...])


def _final(dest, topw, h1, p2, ys, wsgu, wsd, wpg, wpp, g2, b2, g3, b3, *, alpha):
    T, D = h1.shape
    tm = TM_FINAL
    row = lambda i: (i, 0)
    full = lambda i: (0, 0)
    col = lambda i: (0, i)
    return pl.pallas_call(
        functools.partial(_final_kernel, alpha=alpha),
        grid=(T // tm,),
        in_specs=[
            pl.BlockSpec((TOP_K, tm), col, memory_space=pltpu.SMEM),
            pl.BlockSpec((TOP_K, tm), col),
            pl.BlockSpec((tm, D), row),
            pl.BlockSpec((tm, p2.shape[1]), row),
            pl.BlockSpec(memory_space=pl.ANY),
            pl.BlockSpec(wsgu.shape, full), pl.BlockSpec(wsd.shape, full),
            pl.BlockSpec(wpg.shape, full), pl.BlockSpec(wpp.shape, full),
            pl.BlockSpec(g2.shape, full), pl.BlockSpec(b2.shape, full),
            pl.BlockSpec(g3.shape, full), pl.BlockSpec(b3.shape, full),
        ],
        out_specs=pl.BlockSpec((tm, D), row),
        out_shape=jax.ShapeDtypeStruct((T, D), F32),
        scratch_shapes=[pltpu.VMEM((TOP_K, tm, D), F32), pltpu.SemaphoreType.DMA],
        compiler_params=pltpu.CompilerParams(
            dimension_semantics=("arbitrary",), vmem_limit_bytes=VMEM_LIMIT_BYTES),
        name="final",
    )(dest, topw, h1, p2, ys, wsgu, wsd, wpg, wpp, g2, b2, g3, b3)


def _rope_tables(seq):
    half = ROT_DIM // 2
    inv_freq = ROPE_THETA ** (-jnp.arange(0, ROT_DIM, 2, dtype=F32) / ROT_DIM)
    ang = jnp.arange(seq, dtype=I32).astype(F32)[:, None] * inv_freq[None, :]
    cos, sin = jnp.cos(ang), jnp.sin(ang)
    ones = jnp.ones((seq, HEAD_DIM - ROT_DIM), F32)
    zeros = jnp.zeros((seq, HEAD_DIM - ROT_DIM), F32)
    zh = jnp.zeros((seq, half), F32)
    cos_h = jnp.concatenate([cos, cos, ones], axis=1)
    sa_h = jnp.concatenate([-sin, zh, zeros], axis=1)
    sb_h = jnp.concatenate([zh, sin, zeros], axis=1)
    rep = lambda t: jnp.concatenate([t] * HEADS_PER_LANE_GROUP, axis=1)
    return rep(cos_h), rep(sa_h), rep(sb_h)


def _expert_layout(counts, n_blocks):
    te = TM_EXPERT
    nblk_e = (counts + te - 1) // te
    blk_end = jnp.cumsum(nblk_e)
    blk_start = blk_end - nblk_e
    nused = blk_end[-1]
    pstart = (blk_start * te).astype(I32)
    bid = jnp.arange(n_blocks, dtype=I32)
    block_e = jnp.searchsorted(blk_end, jnp.minimum(bid, nused - 1), side='right').astype(I32)
    block_e = jnp.minimum(block_e, N_EXPERTS - 1)
    partial = (counts % te) != 0
    order = jnp.argsort(jnp.logical_not(partial), stable=True).astype(I32)
    n_partial = jnp.sum(partial).astype(I32)
    last_blk = (blk_end - 1).astype(I32)
    zi = jnp.arange(n_blocks + N_EXPERTS, dtype=I32)
    zblk = jnp.where(zi < n_partial, last_blk[order[jnp.minimum(zi, N_EXPERTS - 1)]],
                     nused + (zi - n_partial))
    nzero = n_partial + (n_blocks - nused)
    zrow = (jnp.clip(zblk, 0, n_blocks - 1) * te).astype(I32)
    return pstart, block_e, nused.astype(I32).reshape(1), zrow, nzero.astype(I32).reshape(1)


def _layer(h, p2, w_in, conv_w, w_attn_out, w_conv_out, w_out, ln1_g, ln1_b, w_router, router_bias,
           w_exp_gate, w_exp_up, w_exp_down, w_sh_gate, w_sh_up, w_sh_down, ln2_g, ln2_b,
           w_ple_gate, w_ple_proj, ln3_g, ln3_b, *, nbatch, seq, alpha):
    T, D = h.shape
    aw = w_attn_out.shape[0]
    cwid = w_conv_out.shape[0]
    bf = lambda w: w.astype(MXU_DTYPE)
    rowv = lambda g: g.reshape(1, -1)

    cos_t, sa_t, sb_t = _rope_tables(seq)
    q, k, v, km, yc, ga, gc = _inproj(h, bf(w_in), cos_t, sa_t, sb_t, conv_w, seq=seq, aw=aw, cwid=cwid)
    attn = _attention(q, k, v, km, nbatch=nbatch, seq=seq, aw=aw)

    wr_t = w_router.astype(F32).T
    wrh = wr_t.astype(MXU_DTYPE)
    wrl = (wr_t - wrh.astype(F32)).astype(MXU_DTYPE)
    h1, topi, topw, rnk, cnt = _post(
        attn, yc, ga, gc, h, bf(w_attn_out), bf(w_conv_out), bf(w_out), rowv(ln1_g), rowv(ln1_b),
        wrh, wrl, router_bias.astype(F32).reshape(-1, 1), alpha=alpha)

    n_blocks = (T * TOP_K) // TM_EXPERT + N_EXPERTS
    counts = cnt[:, 0].astype(I32)
    pstart, block_e, nused, zrow, nzero = _expert_layout(counts, n_blocks)
    xs, dest = _dispatch(pstart, zrow, nzero, topi, rnk, h1, n_rows=n_blocks * TM_EXPERT)
    ys = _experts(block_e, nused, xs, w_exp_gate, w_exp_up, w_exp_down)

    wsgu = jnp.concatenate([bf(w_sh_gate), bf(w_sh_up)], axis=1)
    return _final(dest, topw, h1, p2, ys, wsgu, bf(w_sh_down), bf(w_ple_gate), bf(w_ple_proj),
                  rowv(ln2_g), rowv(ln2_b), rowv(ln3_g), rowv(ln3_b), alpha=alpha)


def kernel(x, p, w_in, conv_w, w_attn_out, w_conv_out, w_out, ln1_g, ln1_b, w_router, router_bias,
           w_exp_gate, w_exp_up, w_exp_down, w_sh_gate, w_sh_up, w_sh_down, ln2_g, ln2_b,
           w_ple_gate, w_ple_proj, ln3_g, ln3_b):
    nbatch, seq, d_model = x.shape
    depth = w_in.shape[0]
    alpha = (2 * depth) ** 0.25
    assert seq % MOBA_BLOCK == 0
    h = x.reshape(nbatch * seq, d_model)
    for i in range(depth):
        h = _layer(h, p[i].reshape(nbatch * seq, -1), w_in[i], conv_w[i], w_attn_out[i], w_conv_out[i],
                   w_out[i], ln1_g[i], ln1_b[i], w_router[i], router_bias[i],
                   w_exp_gate[i], w_exp_up[i], w_exp_down[i], w_sh_gate[i], w_sh_up[i], w_sh_down[i],
                   ln2_g[i], ln2_b[i], w_ple_gate[i], w_ple_proj[i], ln3_g[i], ln3_b[i],
                   nbatch=nbatch, seq=seq, alpha=alpha)
    return h.reshape(nbatch, seq, d_model)
```

```python
import functools
import math

import jax
import jax.numpy as jnp
from jax import lax
from jax.experimental import pallas as pl
from jax.experimental.pallas import tpu as pltpu

N_HEADS = 8
HEAD_DIM = 64
ROT_DIM = 16
ROPE_THETA = 500000.0
MOBA_BLOCK = 256
MOBA_TOPK = 3
CONV_K = 3
N_EXPERTS = 256
TOP_K = 8
N_GROUPS = 8
TOPK_GROUPS = 4
ROUTED_SCALE = 2.5
LN_EPS = 1e-5
NEG = -1e30

LANES = 128
SUBLANES = 8
HEADS_PER_LANE_GROUP = LANES // HEAD_DIM
VMEM_LIMIT_BYTES = 56 * 1024 * 1024

MXU_DTYPE = jnp.bfloat16
F32 = jnp.float32
I32 = jnp.int32

TM_INPROJ = 256
TM_POST = 256
TM_DISPATCH = 256
TM_EXPERT = 256
TM_FINAL = 256


def _dot(a, b):
    return jnp.dot(a, b, preferred_element_type=F32)


def _dot_nt(a, b):
    return lax.dot_general(a, b, (((1,), (1,)), ((), ())), preferred_element_type=F32)


def _split_bf16(a):
    hi = a.astype(MXU_DTYPE)
    lo = (a - hi.astype(F32)).astype(MXU_DTYPE)
    return hi, lo


def _tt_load(ref, n_tok, lead=()):
    chunks = [ref[lead + (pl.ds(s, n_tok, stride=SUBLANES), slice(None))] for s in range(SUBLANES)]
    return jnp.concatenate(chunks, axis=1)


def _tt_store(ref, val):
    n_tok = val.shape[0]
    for s in range(SUBLANES):
        ref[pl.ds(s, n_tok, stride=SUBLANES), :] = val[:, s * LANES:(s + 1) * LANES]


def _tile_copy(src, dst, src_tok, dst_tok, sem):
    rows = lambda t: pl.ds(pl.multiple_of(t * SUBLANES, SUBLANES), SUBLANES)
    return pltpu.make_async_copy(src.at[rows(src_tok)], dst.at[rows(dst_tok)], sem)


def _layer_norm(t, g, b):
    mu = jnp.mean(t, axis=-1, keepdims=True)
    d = t - mu
    var = jnp.mean(d * d, axis=-1, keepdims=True)
    return d * lax.rsqrt(var + LN_EPS) * g + b


def _silu(t):
    return t * jax.nn.sigmoid(t)


def _inproj_kernel(x_ref, w_ref, cos_ref, sa_ref, sb_ref, cw_ref,
                   q_ref, k_ref, v_ref, km_ref, yc_ref, ga_ref, gc_ref, ubuf,
                   *, tiles_per_seq, aw, cwid, d_model):
    tm = x_ref.shape[0]
    j = pl.program_id(0) % tiles_per_seq
    xb = x_ref[...].astype(MXU_DTYPE)
    cos, sa, sb = cos_ref[...], sa_ref[...], sb_ref[...]

    def proj(c0, n):
        return _dot(xb, w_ref[:, c0:c0 + n])

    def rope(z):
        parts = []
        for g in range(aw // LANES):
            zg = z[:, g * LANES:(g + 1) * LANES]
            parts.append(zg * cos
                         + pltpu.roll(zg, LANES - ROT_DIM // 2, 1) * sa
                         + pltpu.roll(zg, ROT_DIM // 2, 1) * sb)
        return jnp.concatenate(parts, axis=1)

    q_ref[...] = (rope(proj(0, aw)) * (HEAD_DIM ** -0.5)).astype(q_ref.dtype)

    k = rope(proj(aw, aw))
    k_ref[...] = k.astype(k_ref.dtype)
    nb_seq = km_ref.shape[1]
    blocks_per_tile = tm // MOBA_BLOCK

    @pl.when(j == 0)
    def _():
        km_ref[...] = jnp.zeros(km_ref.shape, km_ref.dtype)

    rows = lax.broadcasted_iota(I32, (nb_seq, aw), 0)
    km = km_ref[0]
    for bi in range(blocks_per_tile):
        mean = jnp.sum(k[bi * MOBA_BLOCK:(bi + 1) * MOBA_BLOCK], axis=0, keepdims=True) * (1.0 / MOBA_BLOCK)
        km = jnp.where(rows == j * blocks_per_tile + bi, mean, km)
    km_ref[0] = km

    v_ref[...] = proj(2 * aw, aw).astype(v_ref.dtype)

    c0 = 3 * aw
    cb = proj(c0, cwid)
    u = proj(c0 + cwid, cwid) * proj(c0 + 2 * cwid, cwid)

    @pl.when(j == 0)
    def _():
        ubuf[0:SUBLANES, :] = jnp.zeros((SUBLANES, cwid), F32)

    ubuf[SUBLANES:SUBLANES + tm, :] = u
    um1 = ubuf[SUBLANES - 1:SUBLANES - 1 + tm, :]
    um2 = ubuf[SUBLANES - 2:SUBLANES - 2 + tm, :]
    conv = cw_ref[0:1, :] * um2 + cw_ref[1:2, :] * um1 + cw_ref[2:3, :] * u
    yc_ref[...] = (cb * conv).astype(yc_ref.dtype)
    ubuf[0:SUBLANES, :] = u[tm - SUBLANES:tm, :]

    g0 = c0 + 3 * cwid
    ga_ref[...] = jax.nn.sigmoid(proj(g0, d_model))
    gc_ref[...] = jax.nn.sigmoid(proj(g0 + d_model, d_model))


def _inproj(x2, w_in_b, cos_t, sa_t, sb_t, conv_w, *, seq, aw, cwid):
    T, D = x2.shape
    tm = TM_INPROJ
    assert seq % tm == 0 and tm % MOBA_BLOCK == 0 and T % seq == 0
    tiles_per_seq = seq // tm
    nb_seq = seq // MOBA_BLOCK
    nbatch = T // seq
    row = lambda i: (i, 0)
    tab = lambda i: (i % tiles_per_seq, 0)
    kern = functools.partial(_inproj_kernel, tiles_per_seq=tiles_per_seq, aw=aw, cwid=cwid, d_model=D)
    return pl.pallas_call(
        kern,
        grid=(T // tm,),
        in_specs=[
            pl.BlockSpec((tm, D), row),
            pl.BlockSpec(w_in_b.shape, lambda i: (0, 0)),
            pl.BlockSpec((tm, LANES), tab),
            pl.BlockSpec((tm, LANES), tab),
            pl.BlockSpec((tm, LANES), tab),
            pl.BlockSpec(conv_w.shape, lambda i: (0, 0)),
        ],
        out_specs=[
            pl.BlockSpec((tm, aw), row),
            pl.BlockSpec((tm, aw), row),
            pl.BlockSpec((tm, aw), row),
            pl.BlockSpec((1, nb_seq, aw), lambda i: (i // tiles_per_seq, 0, 0)),
            pl.BlockSpec((tm, cwid), row),
            pl.BlockSpec((tm, D), row),
            pl.BlockSpec((tm, D), row),
        ],
        out_shape=[
            jax.ShapeDtypeStruct((T, aw), MXU_DTYPE),
            jax.ShapeDtypeStruct((T, aw), MXU_DTYPE),
            jax.ShapeDtypeStruct((T, aw), MXU_DTYPE),
            jax.ShapeDtypeStruct((nbatch, nb_seq, aw), F32),
            jax.ShapeDtypeStruct((T, cwid), MXU_DTYPE),
            jax.ShapeDtypeStruct((T, D), F32),
            jax.ShapeDtypeStruct((T, D), F32),
        ],
        scratch_shapes=[pltpu.VMEM((tm + SUBLANES, cwid), F32)],
        compiler_params=pltpu.CompilerParams(
            dimension_semantics=("arbitrary",), vmem_limit_bytes=VMEM_LIMIT_BYTES),
        name="inproj",
    )(x2, w_in_b, cos_t, sa_t, sb_t, conv_w)


def _attn_kernel(q_ref, k_ref, v_ref, km_ref, o_ref, vt_ref, sel_ref):
    qb = pl.program_id(2)
    blk = MOBA_BLOCK
    nb = km_ref.shape[1]

    @pl.when(qb == 0)
    def _():
        for b in range(nb):
            vt_ref[b] = v_ref[0, b * blk:(b + 1) * blk, :].astype(F32).T.astype(vt_ref.dtype)

    q2 = q_ref[0]
    km_hi, km_lo = _split_bf16(km_ref[0])
    blk_id = lax.broadcasted_iota(I32, (nb, blk), 0).astype(F32)
    past = blk_id < qb.astype(F32)
    key_pos = lax.broadcasted_iota(I32, (blk, blk), 0)
    qry_pos = lax.broadcasted_iota(I32, (blk, blk), 1)
    own_start = pl.multiple_of(qb * blk, blk)

    heads = range(HEADS_PER_LANE_GROUP)
    q2t = q2.astype(F32).T
    dim = lax.broadcasted_iota(I32, q2t.shape, 0)
    qts = []
    for h in heads:
        in_head = (dim >= h * HEAD_DIM) & (dim < (h + 1) * HEAD_DIM)
        qt = jnp.where(in_head, q2t, 0.0).astype(MXU_DTYPE)
        qts.append(qt)

        gate = jnp.where(past, _dot(km_hi, qt) + _dot(km_lo, qt), -jnp.inf)
        sel = jnp.zeros((nb, blk), F32)
        for _ in range(MOBA_TOPK):
            mx = jnp.max(gate, axis=0, keepdims=True)
            idx = jnp.min(jnp.where(gate == mx, blk_id, float(nb)), axis=0, keepdims=True)
            pick = blk_id == idx
            sel = jnp.where(pick, jnp.where(past, 1.0, sel), sel)
            gate = jnp.where(pick, -jnp.inf, gate)
        sel_ref[h] = sel

    def v_t(jb, h):
        return vt_ref[jb, h * HEAD_DIM:(h + 1) * HEAD_DIM, :]

    k_own = k_ref[0, pl.ds(own_start, blk), :]
    scores = [_dot(k_own, qts[h]) for h in heads]
    state = []
    for h in heads:
        s = jnp.where(key_pos <= qry_pos, scores[h], NEG)
        m = jnp.max(s, axis=0, keepdims=True)
        p = jnp.exp(s - m)
        state += [m, jnp.sum(p, axis=0, keepdims=True), _dot(v_t(qb, h), p.astype(MXU_DTYPE))]

    def make_step(nk):
        def step(b0, carry):
            kb = k_ref[0, pl.ds(pl.multiple_of(b0 * blk, blk), nk * blk), :]
            scores = [_dot(kb, qts[h]) for h in heads]
            new = []
            for h in heads:
                m, l, acc = carry[3 * h:3 * h + 3]
                parts = [jnp.where(sel_ref[h, pl.ds(b0 + t, 1), :] > 0.5,
                                   scores[h][t * blk:(t + 1) * blk], NEG) for t in range(nk)]
                s = jnp.concatenate(parts, axis=0) if nk > 1 else parts[0]
                m_new = jnp.maximum(m, jnp.max(s, axis=0, keepdims=True))
                alpha = jnp.exp(m - m_new)
                p = jnp.exp(s - m_new)
                vt = [v_t(b0 + t, h) for t in range(nk)]
                vt = jnp.concatenate(vt, axis=1) if nk > 1 else vt[0]
                new += [m_new, alpha * l + jnp.sum(p, axis=0, keepdims=True),
                        alpha * acc + _dot(vt, p.astype(MXU_DTYPE))]
            return tuple(new)
        return step

    quad, pair, single = make_step(4), make_step(2), make_step(1)
    n_quads = lax.shift_right_logical(qb, 2)
    state = lax.fori_loop(0, n_quads, lambda i, c: quad(4 * i, c), tuple(state))
    state = lax.fori_loop(0, lax.shift_right_logical(qb, 1) & 1, lambda i, c: pair(4 * n_quads, c), state)
    state = lax.fori_loop(0, qb & 1, lambda i, c: single(qb - 1, c), state)
    outs = [state[3 * h + 2] / state[3 * h + 1] for h in heads]
    o_ref[0] = jnp.concatenate(outs, axis=0).T.astype(o_ref.dtype)


def _attention(q, k, v, km, *, nbatch, seq, aw):
    blk = MOBA_BLOCK
    nb = seq // blk
    assert nb % SUBLANES == 0
    q3, k3, v3 = (t.reshape(nbatch, seq, aw) for t in (q, k, v))
    out = pl.pallas_call(
        _attn_kernel,
        grid=(nbatch, aw // LANES, nb),
        in_specs=[
            pl.BlockSpec((1, blk, LANES), lambda b, g, i: (b, i, g)),
            pl.BlockSpec((1, seq, LANES), lambda b, g, i: (b, 0, g)),
            pl.BlockSpec((1, seq, LANES), lambda b, g, i: (b, 0, g)),
            pl.BlockSpec((1, nb, LANES), lambda b, g, i: (b, 0, g)),
        ],
        out_specs=pl.BlockSpec((1, blk, LANES), lambda b, g, i: (b, i, g)),
        out_shape=jax.ShapeDtypeStruct((nbatch, seq, aw), MXU_DTYPE),
        scratch_shapes=[
            pltpu.VMEM((nb, LANES, blk), MXU_DTYPE),
            pltpu.VMEM((HEADS_PER_LANE_GROUP, nb, blk), F32),
        ],
        compiler_params=pltpu.CompilerParams(
            dimension_semantics=("arbitrary", "arbitrary", "arbitrary"),
            vmem_limit_bytes=VMEM_LIMIT_BYTES),
        name="moba_attn",
    )(q3, k3, v3, km)
    return out.reshape(nbatch * seq, aw)


def _post_kernel(attn_ref, yc_ref, ga_ref, gc_ref, x_ref, wao_ref, wco_ref, wo_ref,
                 g1_ref, b1_ref, wrh_ref, wrl_ref, rb_ref,
                 h1_ref, ti_ref, tw_ref, rk_ref, cnt_ref, cnt_acc, *, alpha):
    i = pl.program_id(0)
    tm = x_ref.shape[0]
    ne = wrh_ref.shape[0]

    y_attn = _dot(attn_ref[...], wao_ref[...])
    y_conv = _dot(yc_ref[...], wco_ref[...])
    merged = ga_ref[...] * y_attn + gc_ref[...] * y_conv
    mix = _dot(merged.astype(MXU_DTYPE), wo_ref[...])
    h1 = _layer_norm(alpha * x_ref[...] + mix, g1_ref[...], b1_ref[...])
    _tt_store(h1_ref, h1)

    h_hi, h_lo = _split_bf16(h1)
    logits = _dot_nt(wrh_ref[...], h_hi) + _dot_nt(wrh_ref[...], h_lo) + _dot_nt(wrl_ref[...], h_hi)
    scores = jax.nn.sigmoid(logits)
    choice = scores + rb_ref[...]

    gsz = ne // N_GROUPS
    gshape = (N_GROUPS, gsz, tm)
    c3 = choice.reshape(gshape)
    in_grp = lax.broadcasted_iota(I32, gshape, 1).astype(F32)
    m1 = jnp.max(c3, axis=1, keepdims=True)
    i1 = jnp.min(jnp.where(c3 == m1, in_grp, float(gsz)), axis=1, keepdims=True)
    m2 = jnp.max(jnp.where(in_grp == i1, -jnp.inf, c3), axis=1, keepdims=True)
    gscore = jnp.broadcast_to(m1 + m2, gshape).reshape(ne, tm)

    eid = lax.broadcasted_iota(I32, (ne, tm), 0).astype(F32)
    gid = lax.broadcasted_iota(I32, gshape, 0).astype(F32).reshape(ne, tm)
    cand = jnp.full((ne, tm), -jnp.inf, F32)
    for _ in range(TOPK_GROUPS):
        mx = jnp.max(gscore, axis=0, keepdims=True)
        idx = jnp.min(jnp.where(gscore == mx, gid, float(N_GROUPS)), axis=0, keepdims=True)
        pick = gid == idx
        cand = jnp.where(pick, choice, cand)
        gscore = jnp.where(pick, -jnp.inf, gscore)

    selmat = jnp.zeros((ne, tm), F32)
    idxs, svals = [], []
    for _ in range(TOP_K):
        mx = jnp.max(cand, axis=0, keepdims=True)
        idx = jnp.min(jnp.where(cand == mx, eid, float(ne)), axis=0, keepdims=True)
        pick = eid == idx
        svals.append(jnp.sum(jnp.where(pick, scores, 0.0), axis=0, keepdims=True))
        idxs.append(idx)
        selmat = jnp.where(pick, 1.0, selmat)
        cand = jnp.where(pick, -jnp.inf, cand)
    ssum = svals[0]
    for r in range(1, TOP_K):
        ssum = ssum + svals[r]

    @pl.when(i == 0)
    def _():
        cnt_acc[...] = jnp.zeros(cnt_acc.shape, F32)

    tr = lax.broadcasted_iota(I32, (tm, tm), 0)
    tc = lax.broadcasted_iota(I32, (tm, tm), 1)
    upper = jnp.where(tr < tc, 1.0, 0.0).astype(MXU_DTYPE)
    selb = selmat.astype(MXU_DTYPE)
    base = cnt_acc[...]
    rank = _dot(selb, upper) + jnp.concatenate([base] * (tm // LANES), axis=1)
    new_cnt = base + _dot(selb, jnp.ones((tm, LANES), MXU_DTYPE))
    cnt_acc[...] = new_cnt
    cnt_ref[...] = new_cnt

    for r in range(TOP_K):
        pick = eid == idxs[r]
        ti_ref[r:r + 1, :] = idxs[r].astype(I32)
        tw_ref[r:r + 1, :] = svals[r] / ssum * ROUTED_SCALE
        rk_ref[r:r + 1, :] = jnp.sum(jnp.where(pick, rank, 0.0), axis=0, keepdims=True).astype(I32)


def _post(attn, yc, ga, gc, x2, wao, wco, wo, g1, b1, wrh, wrl, rb, *, alpha):
    T, D = x2.shape
    tm = TM_POST
    assert D == SUBLANES * LANES
    aw, cwid = attn.shape[1], yc.shape[1]
    ne = wrh.shape[0]
    row = lambda i: (i, 0)
    full = lambda i: (0, 0)
    col = lambda i: (0, i)
    return pl.pallas_call(
        functools.partial(_post_kernel, alpha=alpha),
        grid=(T // tm,),
        in_specs=[
            pl.BlockSpec((tm, aw), row), pl.BlockSpec((tm, cwid), row),
            pl.BlockSpec((tm, D), row), pl.BlockSpec((tm, D), row), pl.BlockSpec((tm, D), row),
            pl.BlockSpec(wao.shape, full), pl.BlockSpec(wco.shape, full), pl.BlockSpec(wo.shape, full),
            pl.BlockSpec(g1.shape, full), pl.BlockSpec(b1.shape, full),
            pl.BlockSpec(wrh.shape, full), pl.BlockSpec(wrl.shape, full), pl.BlockSpec(rb.shape, full),
        ],
        out_specs=[
            pl.BlockSpec((tm * SUBLANES, LANES), row),
            pl.BlockSpec((TOP_K, tm), col), pl.BlockSpec((TOP_K, tm), col), pl.BlockSpec((TOP_K, tm), col),
            pl.BlockSpec((ne, LANES), full),
        ],
        out_shape=[
            jax.ShapeDtypeStruct((T * SUBLANES, LANES), F32),
            jax.ShapeDtypeStruct((TOP_K, T), I32),
            jax.ShapeDtypeStruct((TOP_K, T), F32),
            jax.ShapeDtypeStruct((TOP_K, T), I32),
            jax.ShapeDtypeStruct((ne, LANES), F32),
        ],
        scratch_shapes=[pltpu.VMEM((ne, LANES), F32)],
        compiler_params=pltpu.CompilerParams(
            dimension_semantics=("arbitrary",), vmem_limit_bytes=VMEM_LIMIT_BYTES),
        name="post_route",
    )(attn, yc, ga, gc, x2, wao, wco, wo, g1, b1, wrh, wrl, rb)


def _slots_kernel(pstart_ref, ti_ref, rk_ref, dest_ref):
    ti = ti_ref[...]
    start = lax.fori_loop(0, N_EXPERTS, lambda e, acc: jnp.where(ti == e, pstart_ref[e], acc),
                          jnp.zeros(ti.shape, I32))
    dest_ref[...] = start + rk_ref[...]


def _slots(pstart, topi, rnk):
    T = topi.shape[1]
    tc = min(T, 2048)
    blk = pl.BlockSpec((TOP_K, tc), lambda i, *_: (0, i))
    return pl.pallas_call(
        _slots_kernel,
        grid_spec=pltpu.PrefetchScalarGridSpec(
            num_scalar_prefetch=1, grid=(T // tc,), in_specs=[blk, blk], out_specs=blk),
        out_shape=jax.ShapeDtypeStruct((TOP_K, T), I32),
        compiler_params=pltpu.CompilerParams(dimension_semantics=("arbitrary",)),
        name="slots",
    )(pstart, topi, rnk)


def _dispatch_kernel(zrow_ref, nzero_ref, dest_ref, h1_ref, xs_hbm, zbuf, zsem, sem):
    i = pl.program_id(0)
    tm = dest_ref.shape[1]
    zrows = zbuf.shape[0]

    def zero_copy(z):
        start = pl.multiple_of(zrow_ref[z] * SUBLANES, zrows)
        return pltpu.make_async_copy(zbuf, xs_hbm.at[pl.ds(start, zrows)], zsem)

    @pl.when(i == 0)
    def _():
        zbuf[...] = jnp.zeros(zbuf.shape, zbuf.dtype)
        nz = nzero_ref[0]
        lax.fori_loop(0, nz, lambda z, c: (zero_copy(z).start(), c)[1], 0)
        lax.fori_loop(0, nz, lambda z, c: (zero_copy(z).wait(), c)[1], 0)

    def copy(k, j):
        return _tile_copy(h1_ref, xs_hbm, j, dest_ref[k, j], sem)

    def issue(j, c):
        for k in range(TOP_K):
            copy(k, j).start()
        return c

    def drain(j, c):
        for k in range(TOP_K):
            copy(k, j).wait()
        return c

    lax.fori_loop(0, tm, issue, 0)
    lax.fori_loop(0, tm, drain, 0)


def _dispatch(zrow, nzero, dest, h1, *, n_rows):
    T = h1.shape[0] // SUBLANES
    tm = TM_DISPATCH
    return pl.pallas_call(
        _dispatch_kernel,
        grid_spec=pltpu.PrefetchScalarGridSpec(
            num_scalar_prefetch=2,
            grid=(T // tm,),
            in_specs=[pl.BlockSpec((TOP_K, tm), lambda i, *_: (0, i), memory_space=pltpu.SMEM),
                      pl.BlockSpec((tm * SUBLANES, LANES), lambda i, *_: (i, 0))],
            out_specs=pl.BlockSpec(memory_space=pl.ANY),
            scratch_shapes=[pltpu.VMEM((TM_EXPERT * SUBLANES, LANES), h1.dtype),
                            pltpu.SemaphoreType.DMA, pltpu.SemaphoreType.DMA],
        ),
        out_shape=jax.ShapeDtypeStruct((n_rows * SUBLANES, LANES), h1.dtype),
        compiler_params=pltpu.CompilerParams(
            dimension_semantics=("arbitrary",), vmem_limit_bytes=VMEM_LIMIT_BYTES),
        name="dispatch",
    )(zrow, nzero, dest, h1)


def _experts_kernel(be_ref, nused_ref, xs_ref, wg_ref, wu_ref, wd_ref, ys_ref, wgu_b, wd_b):
    i = pl.program_id(0)
    ed = wg_ref.shape[2]

    @pl.when(i < nused_ref[0])
    def _():
        prev = be_ref[jnp.maximum(i - 1, 0)]

        @pl.when((i == 0) | (be_ref[i] != prev))
        def _():
            wgu_b[:, 0:ed] = wg_ref[0].astype(MXU_DTYPE)
            wgu_b[:, ed:2 * ed] = wu_ref[0].astype(MXU_DTYPE)
            wd_b[...] = wd_ref[0].astype(MXU_DTYPE)

        te = xs_ref.shape[0] // SUBLANES
        gu = _dot(_tt_load(xs_ref, te).astype(MXU_DTYPE), wgu_b[...])
        hb = _silu(gu[:, 0:ed]) * gu[:, ed:2 * ed]
        _tt_store(ys_ref, _dot(hb.astype(MXU_DTYPE), wd_b[...]))


def _experts(block_e, nused, xs, wg, wu, wd):
    te = TM_EXPERT
    D, ed = wg.shape[1], wg.shape[2]
    nblk = xs.shape[0] // (te * SUBLANES)
    blk_map = lambda i, be, nu: (jnp.minimum(i, nu[0] - 1), 0)
    w_map = lambda i, be, nu: (be[i], 0, 0)
    return pl.pallas_call(
        _experts_kernel,
        grid_spec=pltpu.PrefetchScalarGridSpec(
            num_scalar_prefetch=2,
            grid=(nblk,),
            in_specs=[
                pl.BlockSpec((te * SUBLANES, LANES), blk_map),
                pl.BlockSpec((1, D, ed), w_map),
                pl.BlockSpec((1, D, ed), w_map),
                pl.BlockSpec((1, ed, D), w_map),
            ],
            out_specs=pl.BlockSpec((te * SUBLANES, LANES), blk_map),
            scratch_shapes=[pltpu.VMEM((D, 2 * ed), MXU_DTYPE), pltpu.VMEM((ed, D), MXU_DTYPE)],
        ),
        out_shape=jax.ShapeDtypeStruct(xs.shape, F32),
        compiler_params=pltpu.CompilerParams(
            dimension_semantics=("arbitrary",), vmem_limit_bytes=VMEM_LIMIT_BYTES),
        name="experts",
    )(block_e, nused, xs, wg, wu, wd)


def _final_kernel(dest_ref, tw_ref, h1_ref, p_ref, ys_hbm, wsgu_ref, wsd_ref, wpg_ref, wpp_ref,
                  g2_ref, b2_ref, g3_ref, b3_ref, o_ref, gbuf, routed_ref, sem, *, alpha):
    tm = o_ref.shape[0]
    sd = wsd_ref.shape[0]

    def row_copy(k, j):
        return _tile_copy(ys_hbm, gbuf.at[k], dest_ref[k, j], j, sem)

    def issue(j, c):
        for k in range(TOP_K):
            row_copy(k, j).start()
        return c

    def drain(j, c):
        for k in range(TOP_K):
            row_copy(k, j).wait()
        return c

    lax.fori_loop(0, tm, issue, 0)

    h1 = _tt_load(h1_ref, tm)
    hb = h1.astype(MXU_DTYPE)
    gu = _dot(hb, wsgu_ref[...])
    shared = _dot((_silu(gu[:, 0:sd]) * gu[:, sd:2 * sd]).astype(MXU_DTYPE), wsd_ref[...])

    lax.fori_loop(0, tm, drain, 0)

    tw = tw_ref[...]
    wt = jnp.concatenate([tw, jnp.zeros((LANES - TOP_K, tm), F32)], axis=0).T
    for r in range(tm // SUBLANES):
        wr = wt[r * SUBLANES:(r + 1) * SUBLANES, :]
        wb = [jnp.broadcast_to(wr[:, k:k + 1], (SUBLANES, LANES)) for k in range(TOP_K)]
        for s in range(SUBLANES):
            rows = pl.ds(r * SUBLANES * SUBLANES + s, SUBLANES, stride=SUBLANES)
            acc = wb[0] * gbuf[0, rows, :]
            for k in range(1, TOP_K):
                acc = acc + wb[k] * gbuf[k, rows, :]
            routed_ref[r * SUBLANES:(r + 1) * SUBLANES, s * LANES:(s + 1) * LANES] = acc
    routed = routed_ref[...]

    h2 = _layer_norm(alpha * h1 + (routed + shared), g2_ref[...], b2_ref[...])
    gate = jax.nn.sigmoid(_dot(h2.astype(MXU_DTYPE), wpg_ref[...]))
    ple = gate * _dot(p_ref[...].astype(MXU_DTYPE), wpp_ref[...])
    o_ref[...] = _layer_norm(alpha * h2 + ple, g3_ref[...], b3_ref[...])


def _final(dest, topw, h1, p2, ys, wsgu, wsd, wpg, wpp, g2, b2, g3, b3, *, alpha):
    T, D = p2.shape[0], wpg.shape[0]
    tm = TM_FINAL
    row = lambda i: (i, 0)
    full = lambda i: (0, 0)
    col = lambda i: (0, i)
    return pl.pallas_call(
        functools.partial(_final_kernel, alpha=alpha),
        grid=(T // tm,),
        in_specs=[
            pl.BlockSpec((TOP_K, tm), col, memory_space=pltpu.SMEM),
            pl.BlockSpec((TOP_K, tm), col),
            pl.BlockSpec((tm * SUBLANES, LANES), row),
            pl.BlockSpec((tm, p2.shape[1]), row),
            pl.BlockSpec(memory_space=pl.ANY),
            pl.BlockSpec(wsgu.shape, full), pl.BlockSpec(wsd.shape, full),
            pl.BlockSpec(wpg.shape, full), pl.BlockSpec(wpp.shape, full),
            pl.BlockSpec(g2.shape, full), pl.BlockSpec(b2.shape, full),
            pl.BlockSpec(g3.shape, full), pl.BlockSpec(b3.shape, full),
        ],
        out_specs=pl.BlockSpec((tm, D), row),
        out_shape=jax.ShapeDtypeStruct((T, D), F32),
        scratch_shapes=[pltpu.VMEM((TOP_K, tm * SUBLANES, LANES), F32), pltpu.VMEM((tm, D), F32),
                        pltpu.SemaphoreType.DMA],
        compiler_params=pltpu.CompilerParams(
            dimension_semantics=("arbitrary",), vmem_limit_bytes=VMEM_LIMIT_BYTES),
        name="final",
    )(dest, topw, h1, p2, ys, wsgu, wsd, wpg, wpp, g2, b2, g3, b3)


def _rope_tables(seq):
    half = ROT_DIM // 2
    inv_freq = ROPE_THETA ** (-jnp.arange(0, ROT_DIM, 2, dtype=F32) / ROT_DIM)
    ang = jnp.arange(seq, dtype=I32).astype(F32)[:, None] * inv_freq[None, :]
    cos, sin = jnp.cos(ang), jnp.sin(ang)
    ones = jnp.ones((seq, HEAD_DIM - ROT_DIM), F32)
    zeros = jnp.zeros((seq, HEAD_DIM - ROT_DIM), F32)
    zh = jnp.zeros((seq, half), F32)
    cos_h = jnp.concatenate([cos, cos, ones], axis=1)
    sa_h = jnp.concatenate([-sin, zh, zeros], axis=1)
    sb_h = jnp.concatenate([zh, sin, zeros], axis=1)
    rep = lambda t: jnp.concatenate([t] * HEADS_PER_LANE_GROUP, axis=1)
    return rep(cos_h), rep(sa_h), rep(sb_h)


def _expert_layout(counts, n_blocks):
    te = TM_EXPERT
    nblk_e = (counts + te - 1) // te
    blk_end = jnp.cumsum(nblk_e)
    blk_start = blk_end - nblk_e
    nused = blk_end[-1]
    pstart = (blk_start * te).astype(I32)
    bid = jnp.arange(n_blocks, dtype=I32)
    block_e = jnp.searchsorted(blk_end, jnp.minimum(bid, nused - 1), side='right').astype(I32)
    block_e = jnp.minimum(block_e, N_EXPERTS - 1)
    partial = (counts % te) != 0
    order = jnp.argsort(jnp.logical_not(partial), stable=True).astype(I32)
    n_partial = jnp.sum(partial).astype(I32)
    last_blk = (blk_end - 1).astype(I32)
    zi = jnp.arange(n_blocks + N_EXPERTS, dtype=I32)
    zblk = jnp.where(zi < n_partial, last_blk[order[jnp.minimum(zi, N_EXPERTS - 1)]],
                     nused + (zi - n_partial))
    nzero = n_partial + (n_blocks - nused)
    zrow = (jnp.clip(zblk, 0, n_blocks - 1) * te).astype(I32)
    return pstart, block_e, nused.astype(I32).reshape(1), zrow, nzero.astype(I32).reshape(1)


def _layer(h, p2, w_in, conv_w, w_attn_out, w_conv_out, w_out, ln1_g, ln1_b, w_router, router_bias,
           w_exp_gate, w_exp_up, w_exp_down, w_sh_gate, w_sh_up, w_sh_down, ln2_g, ln2_b,
           w_ple_gate, w_ple_proj, ln3_g, ln3_b, *, nbatch, seq, alpha):
    T, D = h.shape
    aw = w_attn_out.shape[0]
    cwid = w_conv_out.shape[0]
    bf = lambda w: w.astype(MXU_DTYPE)
    rowv = lambda g: g.reshape(1, -1)

    cos_t, sa_t, sb_t = _rope_tables(seq)
    q, k, v, km, yc, ga, gc = _inproj(h, bf(w_in), cos_t, sa_t, sb_t, conv_w, seq=seq, aw=aw, cwid=cwid)
    attn = _attention(q, k, v, km, nbatch=nbatch, seq=seq, aw=aw)

    wr_t = w_router.astype(F32).T
    wrh = wr_t.astype(MXU_DTYPE)
    wrl = (wr_t - wrh.astype(F32)).astype(MXU_DTYPE)
    h1, topi, topw, rnk, cnt = _post(
        attn, yc, ga, gc, h, bf(w_attn_out), bf(w_conv_out), bf(w_out), rowv(ln1_g), rowv(ln1_b),
        wrh, wrl, router_bias.astype(F32).reshape(-1, 1), alpha=alpha)

    n_blocks = (T * TOP_K) // TM_EXPERT + N_EXPERTS
    counts = cnt[:, 0].astype(I32)
    pstart, block_e, nused, zrow, nzero = _expert_layout(counts, n_blocks)
    dest = _slots(pstart, topi, rnk)
    xs = _dispatch(zrow, nzero, dest, h1, n_rows=n_blocks * TM_EXPERT)
    ys = _experts(block_e, nused, xs, w_exp_gate, w_exp_up, w_exp_down)

    wsgu = jnp.concatenate([bf(w_sh_gate), bf(w_sh_up)], axis=1)
    return _final(dest, topw, h1, p2, ys, wsgu, bf(w_sh_down), bf(w_ple_gate), bf(w_ple_proj),
                  rowv(ln2_g), rowv(ln2_b), rowv(ln3_g), rowv(ln3_b), alpha=alpha)


def kernel(x, p, w_in, conv_w, w_attn_out, w_conv_out, w_out, ln1_g, ln1_b, w_router, router_bias,
           w_exp_gate, w_exp_up, w_exp_down, w_sh_gate, w_sh_up, w_sh_down, ln2_g, ln2_b,
           w_ple_gate, w_ple_proj, ln3_g, ln3_b):
    nbatch, seq, d_model = x.shape
    depth = w_in.shape[0]
    alpha = (2 * depth) ** 0.25
    assert seq % MOBA_BLOCK == 0
    h = x.reshape(nbatch * seq, d_model)
    for i in range(depth):
        h = _layer(h, p[i].reshape(nbatch * seq, -1), w_in[i], conv_w[i], w_attn_out[i], w_conv_out[i],
                   w_out[i], ln1_g[i], ln1_b[i], w_router[i], router_bias[i],
                   w_exp_gate[i], w_exp_up[i], w_exp_down[i], w_sh_gate[i], w_sh_up[i], w_sh_down[i],
                   ln2_g[i], ln2_b[i], w_ple_gate[i], w_ple_proj[i], ln3_g[i], ln3_b[i],
                   nbatch=nbatch, seq=seq, alpha=alpha)
    return h.reshape(nbatch, seq, d_model)
```

```python
import functools
import math

import jax
import jax.numpy as jnp
from jax import lax
from jax.experimental import pallas as pl
from jax.experimental.pallas import tpu as pltpu

N_HEADS = 8
HEAD_DIM = 64
ROT_DIM = 16
ROPE_THETA = 500000.0
MOBA_BLOCK = 256
MOBA_TOPK = 3
CONV_K = 3
N_EXPERTS = 256
TOP_K = 8
N_GROUPS = 8
TOPK_GROUPS = 4
ROUTED_SCALE = 2.5
LN_EPS = 1e-5
NEG = -1e30

LANES = 128
SUBLANES = 8
HEADS_PER_LANE_GROUP = LANES // HEAD_DIM
VMEM_LIMIT_BYTES = 56 * 1024 * 1024

MXU_DTYPE = jnp.bfloat16
F32 = jnp.float32
I32 = jnp.int32

TM_INPROJ = 256
TM_POST = 256
TM_DISPATCH = 256
TM_EXPERT = 256
TM_FINAL = 256


def _dot(a, b):
    return jnp.dot(a, b, preferred_element_type=F32)


def _dot_nt(a, b):
    return lax.dot_general(a, b, (((1,), (1,)), ((), ())), preferred_element_type=F32)


def _split_bf16(a):
    hi = a.astype(MXU_DTYPE)
    lo = (a - hi.astype(F32)).astype(MXU_DTYPE)
    return hi, lo


def _tt_load(ref, n_tok, lead=()):
    chunks = [ref[lead + (pl.ds(s, n_tok, stride=SUBLANES), slice(None))] for s in range(SUBLANES)]
    return jnp.concatenate(chunks, axis=1)


def _tt_store(ref, val):
    n_tok = val.shape[0]
    for s in range(SUBLANES):
        ref[pl.ds(s, n_tok, stride=SUBLANES), :] = val[:, s * LANES:(s + 1) * LANES]


def _tile_copy(src, dst, src_tok, dst_tok, sem):
    rows = lambda t: pl.ds(pl.multiple_of(t * SUBLANES, SUBLANES), SUBLANES)
    return pltpu.make_async_copy(src.at[rows(src_tok)], dst.at[rows(dst_tok)], sem)


def _layer_norm(t, g, b):
    mu = jnp.mean(t, axis=-1, keepdims=True)
    d = t - mu
    var = jnp.mean(d * d, axis=-1, keepdims=True)
    return d * lax.rsqrt(var + LN_EPS) * g + b


def _silu(t):
    return t * jax.nn.sigmoid(t)


def _inproj_kernel(x_ref, w_ref, cos_ref, sa_ref, sb_ref, cw_ref,
                   q_ref, k_ref, v_ref, km_ref, yc_ref, ga_ref, gc_ref, ubuf,
                   *, tiles_per_seq, aw, cwid, d_model):
    tm = x_ref.shape[0]
    j = pl.program_id(0) % tiles_per_seq
    xb = x_ref[...].astype(MXU_DTYPE)
    cos, sa, sb = cos_ref[...], sa_ref[...], sb_ref[...]

    def proj(c0, n):
        return _dot(xb, w_ref[:, c0:c0 + n])

    def rope(z):
        parts = []
        for g in range(aw // LANES):
            zg = z[:, g * LANES:(g + 1) * LANES]
            parts.append(zg * cos
                         + pltpu.roll(zg, LANES - ROT_DIM // 2, 1) * sa
                         + pltpu.roll(zg, ROT_DIM // 2, 1) * sb)
        return jnp.concatenate(parts, axis=1)

    q_ref[...] = (rope(proj(0, aw)) * (HEAD_DIM ** -0.5)).astype(q_ref.dtype)

    k = rope(proj(aw, aw))
    k_ref[...] = k.astype(k_ref.dtype)
    nb_seq = km_ref.shape[1]
    blocks_per_tile = tm // MOBA_BLOCK

    @pl.when(j == 0)
    def _():
        km_ref[...] = jnp.zeros(km_ref.shape, km_ref.dtype)

    rows = lax.broadcasted_iota(I32, (nb_seq, aw), 0)
    km = km_ref[0]
    for bi in range(blocks_per_tile):
        mean = jnp.sum(k[bi * MOBA_BLOCK:(bi + 1) * MOBA_BLOCK], axis=0, keepdims=True) * (1.0 / MOBA_BLOCK)
        km = jnp.where(rows == j * blocks_per_tile + bi, mean, km)
    km_ref[0] = km

    v_ref[...] = proj(2 * aw, aw).astype(v_ref.dtype)

    c0 = 3 * aw
    cb = proj(c0, cwid)
    u = proj(c0 + cwid, cwid) * proj(c0 + 2 * cwid, cwid)

    @pl.when(j == 0)
    def _():
        ubuf[0:SUBLANES, :] = jnp.zeros((SUBLANES, cwid), F32)

    ubuf[SUBLANES:SUBLANES + tm, :] = u
    um1 = ubuf[SUBLANES - 1:SUBLANES - 1 + tm, :]
    um2 = ubuf[SUBLANES - 2:SUBLANES - 2 + tm, :]
    conv = cw_ref[0:1, :] * um2 + cw_ref[1:2, :] * um1 + cw_ref[2:3, :] * u
    yc_ref[...] = (cb * conv).astype(yc_ref.dtype)
    ubuf[0:SUBLANES, :] = u[tm - SUBLANES:tm, :]

    g0 = c0 + 3 * cwid
    ga_ref[...] = jax.nn.sigmoid(proj(g0, d_model))
    gc_ref[...] = jax.nn.sigmoid(proj(g0 + d_model, d_model))


def _inproj(x2, w_in_b, cos_t, sa_t, sb_t, conv_w, *, seq, aw, cwid):
    T, D = x2.shape
    tm = TM_INPROJ
    assert seq % tm == 0 and tm % MOBA_BLOCK == 0 and T % seq == 0
    tiles_per_seq = seq // tm
    nb_seq = seq // MOBA_BLOCK
    nbatch = T // seq
    row = lambda i: (i, 0)
    tab = lambda i: (i % tiles_per_seq, 0)
    kern = functools.partial(_inproj_kernel, tiles_per_seq=tiles_per_seq, aw=aw, cwid=cwid, d_model=D)
    return pl.pallas_call(
        kern,
        grid=(T // tm,),
        in_specs=[
            pl.BlockSpec((tm, D), row),
            pl.BlockSpec(w_in_b.shape, lambda i: (0, 0)),
            pl.BlockSpec((tm, LANES), tab),
            pl.BlockSpec((tm, LANES), tab),
            pl.BlockSpec((tm, LANES), tab),
            pl.BlockSpec(conv_w.shape, lambda i: (0, 0)),
        ],
        out_specs=[
            pl.BlockSpec((tm, aw), row),
            pl.BlockSpec((tm, aw), row),
            pl.BlockSpec((tm, aw), row),
            pl.BlockSpec((1, nb_seq, aw), lambda i: (i // tiles_per_seq, 0, 0)),
            pl.BlockSpec((tm, cwid), row),
            pl.BlockSpec((tm, D), row),
            pl.BlockSpec((tm, D), row),
        ],
        out_shape=[
            jax.ShapeDtypeStruct((T, aw), MXU_DTYPE),
            jax.ShapeDtypeStruct((T, aw), MXU_DTYPE),
            jax.ShapeDtypeStruct((T, aw), MXU_DTYPE),
            jax.ShapeDtypeStruct((nbatch, nb_seq, aw), F32),
            jax.ShapeDtypeStruct((T, cwid), MXU_DTYPE),
            jax.ShapeDtypeStruct((T, D), F32),
            jax.ShapeDtypeStruct((T, D), F32),
        ],
        scratch_shapes=[pltpu.VMEM((tm + SUBLANES, cwid), F32)],
        compiler_params=pltpu.CompilerParams(
            dimension_semantics=("arbitrary",), vmem_limit_bytes=VMEM_LIMIT_BYTES),
        name="inproj",
    )(x2, w_in_b, cos_t, sa_t, sb_t, conv_w)


def _attn_kernel(q_ref, k_ref, v_ref, km_ref, o_ref, vt_ref, sel_ref):
    qb = pl.program_id(2)
    blk = MOBA_BLOCK
    nb = km_ref.shape[1]

    @pl.when(qb == 0)
    def _():
        for b in range(nb):
            vt_ref[b] = v_ref[0, b * blk:(b + 1) * blk, :].astype(F32).T.astype(vt_ref.dtype)

    q2 = q_ref[0]
    km_hi, km_lo = _split_bf16(km_ref[0])
    blk_id = lax.broadcasted_iota(I32, (nb, blk), 0).astype(F32)
    past = blk_id < qb.astype(F32)
    key_pos = lax.broadcasted_iota(I32, (blk, blk), 0)
    qry_pos = lax.broadcasted_iota(I32, (blk, blk), 1)
    own_start = pl.multiple_of(qb * blk, blk)

    heads = range(HEADS_PER_LANE_GROUP)
    q2t = q2.astype(F32).T
    dim = lax.broadcasted_iota(I32, q2t.shape, 0)
    qts = []
    for h in heads:
        in_head = (dim >= h * HEAD_DIM) & (dim < (h + 1) * HEAD_DIM)
        qt = jnp.where(in_head, q2t, 0.0).astype(MXU_DTYPE)
        qts.append(qt)

        gate = jnp.where(past, _dot(km_hi, qt) + _dot(km_lo, qt), -jnp.inf)
        sel = jnp.zeros((nb, blk), F32)
        for _ in range(MOBA_TOPK):
            mx = jnp.max(gate, axis=0, keepdims=True)
            idx = jnp.min(jnp.where(gate == mx, blk_id, float(nb)), axis=0, keepdims=True)
            pick = blk_id == idx
            sel = jnp.where(pick, jnp.where(past, 1.0, sel), sel)
            gate = jnp.where(pick, -jnp.inf, gate)
        sel_ref[h] = sel

    def v_t(jb, h):
        return vt_ref[jb, h * HEAD_DIM:(h + 1) * HEAD_DIM, :]

    k_own = k_ref[0, pl.ds(own_start, blk), :]
    scores = [_dot(k_own, qts[h]) for h in heads]
    state = []
    for h in heads:
        s = jnp.where(key_pos <= qry_pos, scores[h], NEG)
        m = jnp.max(s, axis=0, keepdims=True)
        p = jnp.exp(s - m)
        state += [m, jnp.sum(p, axis=0, keepdims=True), _dot(v_t(qb, h), p.astype(MXU_DTYPE))]

    def make_step(nk):
        def step(b0, carry):
            kb = k_ref[0, pl.ds(pl.multiple_of(b0 * blk, blk), nk * blk), :]
            scores = [_dot(kb, qts[h]) for h in heads]
            new = []
            for h in heads:
                m, l, acc = carry[3 * h:3 * h + 3]
                parts = [jnp.where(sel_ref[h, pl.ds(b0 + t, 1), :] > 0.5,
                                   scores[h][t * blk:(t + 1) * blk], NEG) for t in range(nk)]
                s = jnp.concatenate(parts, axis=0) if nk > 1 else parts[0]
                m_new = jnp.maximum(m, jnp.max(s, axis=0, keepdims=True))
                alpha = jnp.exp(m - m_new)
                p = jnp.exp(s - m_new)
                vt = [v_t(b0 + t, h) for t in range(nk)]
                vt = jnp.concatenate(vt, axis=1) if nk > 1 else vt[0]
                new += [m_new, alpha * l + jnp.sum(p, axis=0, keepdims=True),
                        alpha * acc + _dot(vt, p.astype(MXU_DTYPE))]
            return tuple(new)
        return step

    quad, pair, single = make_step(4), make_step(2), make_step(1)
    n_quads = lax.shift_right_logical(qb, 2)
    state = lax.fori_loop(0, n_quads, lambda i, c: quad(4 * i, c), tuple(state))
    state = lax.fori_loop(0, lax.shift_right_logical(qb, 1) & 1, lambda i, c: pair(4 * n_quads, c), state)
    state = lax.fori_loop(0, qb & 1, lambda i, c: single(qb - 1, c), state)
    outs = [state[3 * h + 2] / state[3 * h + 1] for h in heads]
    o_ref[0] = jnp.concatenate(outs, axis=0).T.astype(o_ref.dtype)


def _attention(q, k, v, km, *, nbatch, seq, aw):
    blk = MOBA_BLOCK
    nb = seq // blk
    assert nb % SUBLANES == 0
    q3, k3, v3 = (t.reshape(nbatch, seq, aw) for t in (q, k, v))
    out = pl.pallas_call(
        _attn_kernel,
        grid=(nbatch, aw // LANES, nb),
        in_specs=[
            pl.BlockSpec((1, blk, LANES), lambda b, g, i: (b, i, g)),
            pl.BlockSpec((1, seq, LANES), lambda b, g, i: (b, 0, g)),
            pl.BlockSpec((1, seq, LANES), lambda b, g, i: (b, 0, g)),
            pl.BlockSpec((1, nb, LANES), lambda b, g, i: (b, 0, g)),
        ],
        out_specs=pl.BlockSpec((1, blk, LANES), lambda b, g, i: (b, i, g)),
        out_shape=jax.ShapeDtypeStruct((nbatch, seq, aw), MXU_DTYPE),
        scratch_shapes=[
            pltpu.VMEM((nb, LANES, blk), MXU_DTYPE),
            pltpu.VMEM((HEADS_PER_LANE_GROUP, nb, blk), F32),
        ],
        compiler_params=pltpu.CompilerParams(
            dimension_semantics=("arbitrary", "arbitrary", "arbitrary"),
            vmem_limit_bytes=VMEM_LIMIT_BYTES),
        name="moba_attn",
    )(q3, k3, v3, km)
    return out.reshape(nbatch * seq, aw)


def _post_kernel(attn_ref, yc_ref, ga_ref, gc_ref, x_ref, wao_ref, wco_ref, wo_ref,
                 g1_ref, b1_ref, wrh_ref, wrl_ref, rb_ref,
                 h1_ref, ti_ref, tw_ref, rk_ref, cnt_ref, cnt_acc, *, alpha):
    i = pl.program_id(0)
    tm = x_ref.shape[0]
    ne = wrh_ref.shape[0]

    y_attn = _dot(attn_ref[...], wao_ref[...])
    y_conv = _dot(yc_ref[...], wco_ref[...])
    merged = ga_ref[...] * y_attn + gc_ref[...] * y_conv
    mix = _dot(merged.astype(MXU_DTYPE), wo_ref[...])
    h1 = _layer_norm(alpha * x_ref[...] + mix, g1_ref[...], b1_ref[...])
    _tt_store(h1_ref, h1)

    h_hi, h_lo = _split_bf16(h1)
    logits = _dot_nt(wrh_ref[...], h_hi) + _dot_nt(wrh_ref[...], h_lo) + _dot_nt(wrl_ref[...], h_hi)
    scores = jax.nn.sigmoid(logits)
    choice = scores + rb_ref[...]

    gsz = ne // N_GROUPS
    gshape = (N_GROUPS, gsz, tm)
    c3 = choice.reshape(gshape)
    in_grp = lax.broadcasted_iota(I32, gshape, 1).astype(F32)
    m1 = jnp.max(c3, axis=1, keepdims=True)
    i1 = jnp.min(jnp.where(c3 == m1, in_grp, float(gsz)), axis=1, keepdims=True)
    m2 = jnp.max(jnp.where(in_grp == i1, -jnp.inf, c3), axis=1, keepdims=True)
    gscore = jnp.broadcast_to(m1 + m2, gshape).reshape(ne, tm)

    eid = lax.broadcasted_iota(I32, (ne, tm), 0).astype(F32)
    gid = lax.broadcasted_iota(I32, gshape, 0).astype(F32).reshape(ne, tm)
    cand = jnp.full((ne, tm), -jnp.inf, F32)
    for _ in range(TOPK_GROUPS):
        mx = jnp.max(gscore, axis=0, keepdims=True)
        idx = jnp.min(jnp.where(gscore == mx, gid, float(N_GROUPS)), axis=0, keepdims=True)
        pick = gid == idx
        cand = jnp.where(pick, choice, cand)
        gscore = jnp.where(pick, -jnp.inf, gscore)

    selmat = jnp.zeros((ne, tm), F32)
    idxs, svals = [], []
    for _ in range(TOP_K):
        mx = jnp.max(cand, axis=0, keepdims=True)
        idx = jnp.min(jnp.where(cand == mx, eid, float(ne)), axis=0, keepdims=True)
        pick = eid == idx
        svals.append(jnp.sum(jnp.where(pick, scores, 0.0), axis=0, keepdims=True))
        idxs.append(idx)
        selmat = jnp.where(pick, 1.0, selmat)
        cand = jnp.where(pick, -jnp.inf, cand)
    ssum = svals[0]
    for r in range(1, TOP_K):
        ssum = ssum + svals[r]

    @pl.when(i == 0)
    def _():
        cnt_acc[...] = jnp.zeros(cnt_acc.shape, F32)

    tr = lax.broadcasted_iota(I32, (tm, tm), 0)
    tc = lax.broadcasted_iota(I32, (tm, tm), 1)
    upper = jnp.where(tr < tc, 1.0, 0.0).astype(MXU_DTYPE)
    selb = selmat.astype(MXU_DTYPE)
    base = cnt_acc[...]
    rank = _dot(selb, upper) + jnp.concatenate([base] * (tm // LANES), axis=1)
    new_cnt = base + _dot(selb, jnp.ones((tm, LANES), MXU_DTYPE))
    cnt_acc[...] = new_cnt
    cnt_ref[...] = new_cnt

    for r in range(TOP_K):
        pick = eid == idxs[r]
        ti_ref[r:r + 1, :] = idxs[r].astype(I32)
        tw_ref[r:r + 1, :] = svals[r] / ssum * ROUTED_SCALE
        rk_ref[r:r + 1, :] = jnp.sum(jnp.where(pick, rank, 0.0), axis=0, keepdims=True).astype(I32)


def _post(attn, yc, ga, gc, x2, wao, wco, wo, g1, b1, wrh, wrl, rb, *, alpha):
    T, D = x2.shape
    tm = TM_POST
    assert D == SUBLANES * LANES
    aw, cwid = attn.shape[1], yc.shape[1]
    ne = wrh.shape[0]
    row = lambda i: (i, 0)
    full = lambda i: (0, 0)
    col = lambda i: (0, i)
    return pl.pallas_call(
        functools.partial(_post_kernel, alpha=alpha),
        grid=(T // tm,),
        in_specs=[
            pl.BlockSpec((tm, aw), row), pl.BlockSpec((tm, cwid), row),
            pl.BlockSpec((tm, D), row), pl.BlockSpec((tm, D), row), pl.BlockSpec((tm, D), row),
            pl.BlockSpec(wao.shape, full), pl.BlockSpec(wco.shape, full), pl.BlockSpec(wo.shape, full),
            pl.BlockSpec(g1.shape, full), pl.BlockSpec(b1.shape, full),
            pl.BlockSpec(wrh.shape, full), pl.BlockSpec(wrl.shape, full), pl.BlockSpec(rb.shape, full),
        ],
        out_specs=[
            pl.BlockSpec((tm * SUBLANES, LANES), row),
            pl.BlockSpec((TOP_K, tm), col), pl.BlockSpec((TOP_K, tm), col), pl.BlockSpec((TOP_K, tm), col),
            pl.BlockSpec((ne, LANES), full),
        ],
        out_shape=[
            jax.ShapeDtypeStruct((T * SUBLANES, LANES), F32),
            jax.ShapeDtypeStruct((TOP_K, T), I32),
            jax.ShapeDtypeStruct((TOP_K, T), F32),
            jax.ShapeDtypeStruct((TOP_K, T), I32),
            jax.ShapeDtypeStruct((ne, LANES), F32),
        ],
        scratch_shapes=[pltpu.VMEM((ne, LANES), F32)],
        compiler_params=pltpu.CompilerParams(
            dimension_semantics=("arbitrary",), vmem_limit_bytes=VMEM_LIMIT_BYTES),
        name="post_route",
    )(attn, yc, ga, gc, x2, wao, wco, wo, g1, b1, wrh, wrl, rb)


def _slots_kernel(pstart_ref, ti_ref, rk_ref, dest_ref):
    ti = ti_ref[...]
    start = lax.fori_loop(0, N_EXPERTS, lambda e, acc: jnp.where(ti == e, pstart_ref[e], acc),
                          jnp.zeros(ti.shape, I32))
    dest_ref[...] = start + rk_ref[...]


def _slots(pstart, topi, rnk):
    T = topi.shape[1]
    tc = min(T, 2048)
    blk = pl.BlockSpec((TOP_K, tc), lambda i, *_: (0, i))
    return pl.pallas_call(
        _slots_kernel,
        grid_spec=pltpu.PrefetchScalarGridSpec(
            num_scalar_prefetch=1, grid=(T // tc,), in_specs=[blk, blk], out_specs=blk),
        out_shape=jax.ShapeDtypeStruct((TOP_K, T), I32),
        compiler_params=pltpu.CompilerParams(dimension_semantics=("arbitrary",)),
        name="slots",
    )(pstart, topi, rnk)


def _dispatch_kernel(zrow_ref, nzero_ref, dest_ref, h1_ref, xs_hbm, zbuf, zsem, sem):
    i = pl.program_id(0)
    tm = dest_ref.shape[1]
    zrows = zbuf.shape[0]

    def zero_copy(z):
        start = pl.multiple_of(zrow_ref[z] * SUBLANES, zrows)
        return pltpu.make_async_copy(zbuf, xs_hbm.at[pl.ds(start, zrows)], zsem)

    @pl.when(i == 0)
    def _():
        zbuf[...] = jnp.zeros(zbuf.shape, zbuf.dtype)
        nz = nzero_ref[0]
        lax.fori_loop(0, nz, lambda z, c: (zero_copy(z).start(), c)[1], 0)
        lax.fori_loop(0, nz, lambda z, c: (zero_copy(z).wait(), c)[1], 0)

    def copy(k, j):
        return _tile_copy(h1_ref, xs_hbm, j, dest_ref[k, j], sem)

    def issue(j, c):
        for k in range(TOP_K):
            copy(k, j).start(priority=k % 2)
        return c

    def drain(j, c):
        for k in range(TOP_K):
            copy(k, j).wait()
        return c

    lax.fori_loop(0, tm, issue, 0)
    lax.fori_loop(0, tm, drain, 0)


def _dispatch(zrow, nzero, dest, h1, *, n_rows):
    T = h1.shape[0] // SUBLANES
    tm = TM_DISPATCH
    return pl.pallas_call(
        _dispatch_kernel,
        grid_spec=pltpu.PrefetchScalarGridSpec(
            num_scalar_prefetch=2,
            grid=(T // tm,),
            in_specs=[pl.BlockSpec((TOP_K, tm), lambda i, *_: (0, i), memory_space=pltpu.SMEM),
                      pl.BlockSpec((tm * SUBLANES, LANES), lambda i, *_: (i, 0))],
            out_specs=pl.BlockSpec(memory_space=pl.ANY),
            scratch_shapes=[pltpu.VMEM((TM_EXPERT * SUBLANES, LANES), h1.dtype),
                            pltpu.SemaphoreType.DMA, pltpu.SemaphoreType.DMA],
        ),
        out_shape=jax.ShapeDtypeStruct((n_rows * SUBLANES, LANES), h1.dtype),
        compiler_params=pltpu.CompilerParams(
            dimension_semantics=("arbitrary",), vmem_limit_bytes=VMEM_LIMIT_BYTES),
        name="dispatch",
    )(zrow, nzero, dest, h1)


def _experts_kernel(be_ref, nused_ref, xs_ref, wg_ref, wu_ref, wd_ref, ys_ref, wgu_b, wd_b):
    i = pl.program_id(0)
    ed = wg_ref.shape[2]

    @pl.when(i < nused_ref[0])
    def _():
        prev = be_ref[jnp.maximum(i - 1, 0)]

        @pl.when((i == 0) | (be_ref[i] != prev))
        def _():
            wgu_b[:, 0:ed] = wg_ref[0].astype(MXU_DTYPE)
            wgu_b[:, ed:2 * ed] = wu_ref[0].astype(MXU_DTYPE)
            wd_b[...] = wd_ref[0].astype(MXU_DTYPE)

        te = xs_ref.shape[0] // SUBLANES
        gu = _dot(_tt_load(xs_ref, te).astype(MXU_DTYPE), wgu_b[...])
        hb = _silu(gu[:, 0:ed]) * gu[:, ed:2 * ed]
        _tt_store(ys_ref, _dot(hb.astype(MXU_DTYPE), wd_b[...]))


def _experts(block_e, nused, xs, wg, wu, wd):
    te = TM_EXPERT
    D, ed = wg.shape[1], wg.shape[2]
    nblk = xs.shape[0] // (te * SUBLANES)
    blk_map = lambda i, be, nu: (jnp.minimum(i, nu[0] - 1), 0)
    w_map = lambda i, be, nu: (be[i], 0, 0)
    return pl.pallas_call(
        _experts_kernel,
        grid_spec=pltpu.PrefetchScalarGridSpec(
            num_scalar_prefetch=2,
            grid=(nblk,),
            in_specs=[
                pl.BlockSpec((te * SUBLANES, LANES), blk_map),
                pl.BlockSpec((1, D, ed), w_map),
                pl.BlockSpec((1, D, ed), w_map),
                pl.BlockSpec((1, ed, D), w_map),
            ],
            out_specs=pl.BlockSpec((te * SUBLANES, LANES), blk_map),
            scratch_shapes=[pltpu.VMEM((D, 2 * ed), MXU_DTYPE), pltpu.VMEM((ed, D), MXU_DTYPE)],
        ),
        out_shape=jax.ShapeDtypeStruct(xs.shape, F32),
        compiler_params=pltpu.CompilerParams(
            dimension_semantics=("arbitrary",), vmem_limit_bytes=VMEM_LIMIT_BYTES),
        name="experts",
    )(block_e, nused, xs, wg, wu, wd)


def _final_kernel(dest_ref, dest_next_ref, tw_ref, h1_ref, p_ref, ys_hbm, wsgu_ref, wsd_ref, wpg_ref,
                  wpp_ref, g2_ref, b2_ref, g3_ref, b3_ref, o_ref, gbuf_ref, routed_ref, sems, *, alpha):
    i = pl.program_id(0)
    n = pl.num_programs(0)
    tm = o_ref.shape[0]
    sd = wsd_ref.shape[0]
    cur = i % 2
    gbuf = gbuf_ref.at[cur]

    def gather(idx_ref, slot, wait):
        def copy(k, j):
            return _tile_copy(ys_hbm, gbuf_ref.at[slot, k], idx_ref[k, j], j, sems.at[slot])

        def body(j, c):
            for k in range(TOP_K):
                if wait:
                    copy(k, j).wait()
                else:
                    copy(k, j).start(priority=k % 2)
            return c

        lax.fori_loop(0, tm, body, 0)

    @pl.when(i == 0)
    def _():
        gather(dest_ref, 0, wait=False)

    @pl.when(i + 1 < n)
    def _():
        gather(dest_next_ref, 1 - cur, wait=False)

    h1 = _tt_load(h1_ref, tm)
    hb = h1.astype(MXU_DTYPE)
    gu = _dot(hb, wsgu_ref[...])
    shared = _dot((_silu(gu[:, 0:sd]) * gu[:, sd:2 * sd]).astype(MXU_DTYPE), wsd_ref[...])

    gather(dest_ref, cur, wait=True)

    tw = tw_ref[...]
    wt = jnp.concatenate([tw, jnp.zeros((LANES - TOP_K, tm), F32)], axis=0).T
    for r in range(tm // SUBLANES):
        wr = wt[r * SUBLANES:(r + 1) * SUBLANES, :]
        wb = [jnp.broadcast_to(wr[:, k:k + 1], (SUBLANES, LANES)) for k in range(TOP_K)]
        for s in range(SUBLANES):
            rows = pl.ds(r * SUBLANES * SUBLANES + s, SUBLANES, stride=SUBLANES)
            acc = wb[0] * gbuf[0, rows, :]
            for k in range(1, TOP_K):
                acc = acc + wb[k] * gbuf[k, rows, :]
            routed_ref[r * SUBLANES:(r + 1) * SUBLANES, s * LANES:(s + 1) * LANES] = acc
    routed = routed_ref[...]

    h2 = _layer_norm(alpha * h1 + (routed + shared), g2_ref[...], b2_ref[...])
    gate = jax.nn.sigmoid(_dot(h2.astype(MXU_DTYPE), wpg_ref[...]))
    ple = gate * _dot(p_ref[...].astype(MXU_DTYPE), wpp_ref[...])
    o_ref[...] = _layer_norm(alpha * h2 + ple, g3_ref[...], b3_ref[...])


def _final(dest, topw, h1, p2, ys, wsgu, wsd, wpg, wpp, g2, b2, g3, b3, *, alpha):
    T, D = p2.shape[0], wpg.shape[0]
    tm = TM_FINAL
    n_tiles = T // tm
    row = lambda i: (i, 0)
    full = lambda i: (0, 0)
    col = lambda i: (0, i)
    col_next = lambda i: (0, jnp.minimum(i + 1, n_tiles - 1))
    return pl.pallas_call(
        functools.partial(_final_kernel, alpha=alpha),
        grid=(n_tiles,),
        in_specs=[
            pl.BlockSpec((TOP_K, tm), col, memory_space=pltpu.SMEM),
            pl.BlockSpec((TOP_K, tm), col_next, memory_space=pltpu.SMEM),
            pl.BlockSpec((TOP_K, tm), col),
            pl.BlockSpec((tm * SUBLANES, LANES), row),
            pl.BlockSpec((tm, p2.shape[1]), row),
            pl.BlockSpec(memory_space=pl.ANY),
            pl.BlockSpec(wsgu.shape, full), pl.BlockSpec(wsd.shape, full),
            pl.BlockSpec(wpg.shape, full), pl.BlockSpec(wpp.shape, full),
            pl.BlockSpec(g2.shape, full), pl.BlockSpec(b2.shape, full),
            pl.BlockSpec(g3.shape, full), pl.BlockSpec(b3.shape, full),
        ],
        out_specs=pl.BlockSpec((tm, D), row),
        out_shape=jax.ShapeDtypeStruct((T, D), F32),
        scratch_shapes=[pltpu.VMEM((2, TOP_K, tm * SUBLANES, LANES), F32), pltpu.VMEM((tm, D), F32),
                        pltpu.SemaphoreType.DMA((2,))],
        compiler_params=pltpu.CompilerParams(
            dimension_semantics=("arbitrary",), vmem_limit_bytes=VMEM_LIMIT_BYTES),
        name="final",
    )(dest, dest, topw, h1, p2, ys, wsgu, wsd, wpg, wpp, g2, b2, g3, b3)


def _rope_tables(seq):
    half = ROT_DIM // 2
    inv_freq = ROPE_THETA ** (-jnp.arange(0, ROT_DIM, 2, dtype=F32) / ROT_DIM)
    ang = jnp.arange(seq, dtype=I32).astype(F32)[:, None] * inv_freq[None, :]
    cos, sin = jnp.cos(ang), jnp.sin(ang)
    ones = jnp.ones((seq, HEAD_DIM - ROT_DIM), F32)
    zeros = jnp.zeros((seq, HEAD_DIM - ROT_DIM), F32)
    zh = jnp.zeros((seq, half), F32)
    cos_h = jnp.concatenate([cos, cos, ones], axis=1)
    sa_h = jnp.concatenate([-sin, zh, zeros], axis=1)
    sb_h = jnp.concatenate([zh, sin, zeros], axis=1)
    rep = lambda t: jnp.concatenate([t] * HEADS_PER_LANE_GROUP, axis=1)
    return rep(cos_h), rep(sa_h), rep(sb_h)


def _expert_layout(counts, n_blocks):
    te = TM_EXPERT
    nblk_e = (counts + te - 1) // te
    blk_end = jnp.cumsum(nblk_e)
    blk_start = blk_end - nblk_e
    nused = blk_end[-1]
    pstart = (blk_start * te).astype(I32)
    bid = jnp.arange(n_blocks, dtype=I32)
    block_e = jnp.searchsorted(blk_end, jnp.minimum(bid, nused - 1), side='right').astype(I32)
    block_e = jnp.minimum(block_e, N_EXPERTS - 1)
    partial = (counts % te) != 0
    order = jnp.argsort(jnp.logical_not(partial), stable=True).astype(I32)
    n_partial = jnp.sum(partial).astype(I32)
    last_blk = (blk_end - 1).astype(I32)
    zi = jnp.arange(n_blocks + N_EXPERTS, dtype=I32)
    zblk = jnp.where(zi < n_partial, last_blk[order[jnp.minimum(zi, N_EXPERTS - 1)]],
                     nused + (zi - n_partial))
    nzero = n_partial + (n_blocks - nused)
    zrow = (jnp.clip(zblk, 0, n_blocks - 1) * te).astype(I32)
    return pstart, block_e, nused.astype(I32).reshape(1), zrow, nzero.astype(I32).reshape(1)


def _layer(h, p2, w_in, conv_w, w_attn_out, w_conv_out, w_out, ln1_g, ln1_b, w_router, router_bias,
           w_exp_gate, w_exp_up, w_exp_down, w_sh_gate, w_sh_up, w_sh_down, ln2_g, ln2_b,
           w_ple_gate, w_ple_proj, ln3_g, ln3_b, *, nbatch, seq, alpha):
    T, D = h.shape
    aw = w_attn_out.shape[0]
    cwid = w_conv_out.shape[0]
    bf = lambda w: w.astype(MXU_DTYPE)
    rowv = lambda g: g.reshape(1, -1)

    cos_t, sa_t, sb_t = _rope_tables(seq)
    q, k, v, km, yc, ga, gc = _inproj(h, bf(w_in), cos_t, sa_t, sb_t, conv_w, seq=seq, aw=aw, cwid=cwid)
    attn = _attention(q, k, v, km, nbatch=nbatch, seq=seq, aw=aw)

    wr_t = w_router.astype(F32).T
    wrh = wr_t.astype(MXU_DTYPE)
    wrl = (wr_t - wrh.astype(F32)).astype(MXU_DTYPE)
    h1, topi, topw, rnk, cnt = _post(
        attn, yc, ga, gc, h, bf(w_attn_out), bf(w_conv_out), bf(w_out), rowv(ln1_g), rowv(ln1_b),
        wrh, wrl, router_bias.astype(F32).reshape(-1, 1), alpha=alpha)

    n_blocks = (T * TOP_K) // TM_EXPERT + N_EXPERTS
    counts = cnt[:, 0].astype(I32)
    pstart, block_e, nused, zrow, nzero = _expert_layout(counts, n_blocks)
    dest = _slots(pstart, topi, rnk)
    xs = _dispatch(zrow, nzero, dest, h1, n_rows=n_blocks * TM_EXPERT)
    ys = _experts(block_e, nused, xs, w_exp_gate, w_exp_up, w_exp_down)

    wsgu = jnp.concatenate([bf(w_sh_gate), bf(w_sh_up)], axis=1)
    return _final(dest, topw, h1, p2, ys, wsgu, bf(w_sh_down), bf(w_ple_gate), bf(w_ple_proj),
                  rowv(ln2_g), rowv(ln2_b), rowv(ln3_g), rowv(ln3_b), alpha=alpha)


def kernel(x, p, w_in, conv_w, w_attn_out, w_conv_out, w_out, ln1_g, ln1_b, w_router, router_bias,
           w_exp_gate, w_exp_up, w_exp_down, w_sh_gate, w_sh_up, w_sh_down, ln2_g, ln2_b,
           w_ple_gate, w_ple_proj, ln3_g, ln3_b):
    nbatch, seq, d_model = x.shape
    depth = w_in.shape[0]
    alpha = (2 * depth) ** 0.25
    assert seq % MOBA_BLOCK == 0
    h = x.reshape(nbatch * seq, d_model)
    for i in range(depth):
        h = _layer(h, p[i].reshape(nbatch * seq, -1), w_in[i], conv_w[i], w_attn_out[i], w_conv_out[i],
                   w_out[i], ln1_g[i], ln1_b[i], w_router[i], router_bias[i],
                   w_exp_gate[i], w_exp_up[i], w_exp_down[i], w_sh_gate[i], w_sh_up[i], w_sh_down[i],
                   ln2_g[i], ln2_b[i], w_ple_gate[i], w_ple_proj[i], ln3_g[i], ln3_b[i],
                   nbatch=nbatch, seq=seq, alpha=alpha)
    return h.reshape(nbatch, seq, d_model)
```

```python
import functools
import math

import jax
import jax.numpy as jnp
from jax import lax
from jax.experimental import pallas as pl
from jax.experimental.pallas import tpu as pltpu

N_HEADS = 8
HEAD_DIM = 64
ROT_DIM = 16
ROPE_THETA = 500000.0
MOBA_BLOCK = 256
MOBA_TOPK = 3
CONV_K = 3
N_EXPERTS = 256
TOP_K = 8
N_GROUPS = 8
TOPK_GROUPS = 4
ROUTED_SCALE = 2.5
LN_EPS = 1e-5
NEG = -1e30

LANES = 128
SUBLANES = 8
HEADS_PER_LANE_GROUP = LANES // HEAD_DIM
VMEM_LIMIT_BYTES = 56 * 1024 * 1024

MXU_DTYPE = jnp.bfloat16
F32 = jnp.float32
I32 = jnp.int32
U32 = jnp.uint32

TM_INPROJ = 256
TM_POST = 256
TM_DISPATCH = 256
TM_EXPERT = 256
TM_FINAL = 256


def _dot(a, b):
    return jnp.dot(a, b, preferred_element_type=F32)


def _dot_nt(a, b):
    return lax.dot_general(a, b, (((1,), (1,)), ((), ())), preferred_element_type=F32)


def _split_bf16(a):
    hi = a.astype(MXU_DTYPE)
    lo = (a - hi.astype(F32)).astype(MXU_DTYPE)
    return hi, lo


TOK_ROWS = 4
HI_MASK = 0xFFFF0000


def _pack_pairs(f):
    half = f.shape[1] // 2
    bits = lambda t: lax.bitcast_convert_type(t.astype(jnp.bfloat16).astype(F32), U32)
    return (bits(f[:, half:]) & jnp.uint32(HI_MASK)) | lax.shift_right_logical(bits(f[:, :half]), jnp.uint32(16))


def _unpack_pairs(w):
    lo = lax.bitcast_convert_type(lax.shift_left(w, jnp.uint32(16)), F32)
    hi = lax.bitcast_convert_type(w & jnp.uint32(HI_MASK), F32)
    return lo, hi


def _tt_load(ref, n_tok):
    chunks = [ref[pl.ds(s, n_tok, stride=TOK_ROWS), :] for s in range(TOK_ROWS)]
    return jnp.concatenate(chunks, axis=1)


def _tt_store(ref, words):
    n_tok = words.shape[0]
    for s in range(TOK_ROWS):
        ref[pl.ds(s, n_tok, stride=TOK_ROWS), :] = words[:, s * LANES:(s + 1) * LANES]


def _tile_copy(src, dst, src_tok, dst_tok, sem):
    rows = lambda t: pl.ds(pl.multiple_of(t * TOK_ROWS, TOK_ROWS), TOK_ROWS)
    return pltpu.make_async_copy(src.at[rows(src_tok)], dst.at[rows(dst_tok)], sem)


def _layer_norm(t, g, b):
    mu = jnp.mean(t, axis=-1, keepdims=True)
    d = t - mu
    var = jnp.mean(d * d, axis=-1, keepdims=True)
    return d * lax.rsqrt(var + LN_EPS) * g + b


def _silu(t):
    return t * jax.nn.sigmoid(t)


def _inproj_kernel(x_ref, w_ref, cos_ref, sa_ref, sb_ref, cw_ref,
                   q_ref, k_ref, v_ref, km_ref, yc_ref, ga_ref, gc_ref, ubuf,
                   *, tiles_per_seq, aw, cwid, d_model):
    tm = x_ref.shape[0]
    j = pl.program_id(0) % tiles_per_seq
    xb = x_ref[...].astype(MXU_DTYPE)
    cos, sa, sb = cos_ref[...], sa_ref[...], sb_ref[...]

    def proj(c0, n):
        return _dot(xb, w_ref[:, c0:c0 + n])

    def rope(z):
        parts = []
        for g in range(aw // LANES):
            zg = z[:, g * LANES:(g + 1) * LANES]
            parts.append(zg * cos
                         + pltpu.roll(zg, LANES - ROT_DIM // 2, 1) * sa
                         + pltpu.roll(zg, ROT_DIM // 2, 1) * sb)
        return jnp.concatenate(parts, axis=1)

    q_ref[...] = (rope(proj(0, aw)) * (HEAD_DIM ** -0.5)).astype(q_ref.dtype)

    k = rope(proj(aw, aw))
    k_ref[...] = k.astype(k_ref.dtype)
    nb_seq = km_ref.shape[1]
    blocks_per_tile = tm // MOBA_BLOCK

    @pl.when(j == 0)
    def _():
        km_ref[...] = jnp.zeros(km_ref.shape, km_ref.dtype)

    rows = lax.broadcasted_iota(I32, (nb_seq, aw), 0)
    km = km_ref[0]
    for bi in range(blocks_per_tile):
        mean = jnp.sum(k[bi * MOBA_BLOCK:(bi + 1) * MOBA_BLOCK], axis=0, keepdims=True) * (1.0 / MOBA_BLOCK)
        km = jnp.where(rows == j * blocks_per_tile + bi, mean, km)
    km_ref[0] = km

    v_ref[...] = proj(2 * aw, aw).astype(v_ref.dtype)

    c0 = 3 * aw
    cb = proj(c0, cwid)
    u = proj(c0 + cwid, cwid) * proj(c0 + 2 * cwid, cwid)

    @pl.when(j == 0)
    def _():
        ubuf[0:SUBLANES, :] = jnp.zeros((SUBLANES, cwid), F32)

    ubuf[SUBLANES:SUBLANES + tm, :] = u
    um1 = ubuf[SUBLANES - 1:SUBLANES - 1 + tm, :]
    um2 = ubuf[SUBLANES - 2:SUBLANES - 2 + tm, :]
    conv = cw_ref[0:1, :] * um2 + cw_ref[1:2, :] * um1 + cw_ref[2:3, :] * u
    yc_ref[...] = (cb * conv).astype(yc_ref.dtype)
    ubuf[0:SUBLANES, :] = u[tm - SUBLANES:tm, :]

    g0 = c0 + 3 * cwid
    ga_ref[...] = jax.nn.sigmoid(proj(g0, d_model))
    gc_ref[...] = jax.nn.sigmoid(proj(g0 + d_model, d_model))


def _inproj(x2, w_in_b, cos_t, sa_t, sb_t, conv_w, *, seq, aw, cwid):
    T, D = x2.shape
    tm = TM_INPROJ
    assert seq % tm == 0 and tm % MOBA_BLOCK == 0 and T % seq == 0
    tiles_per_seq = seq // tm
    nb_seq = seq // MOBA_BLOCK
    nbatch = T // seq
    row = lambda i: (i, 0)
    tab = lambda i: (i % tiles_per_seq, 0)
    kern = functools.partial(_inproj_kernel, tiles_per_seq=tiles_per_seq, aw=aw, cwid=cwid, d_model=D)
    return pl.pallas_call(
        kern,
        grid=(T // tm,),
        in_specs=[
            pl.BlockSpec((tm, D), row),
            pl.BlockSpec(w_in_b.shape, lambda i: (0, 0)),
            pl.BlockSpec((tm, LANES), tab),
            pl.BlockSpec((tm, LANES), tab),
            pl.BlockSpec((tm, LANES), tab),
            pl.BlockSpec(conv_w.shape, lambda i: (0, 0)),
        ],
        out_specs=[
            pl.BlockSpec((tm, aw), row),
            pl.BlockSpec((tm, aw), row),
            pl.BlockSpec((tm, aw), row),
            pl.BlockSpec((1, nb_seq, aw), lambda i: (i // tiles_per_seq, 0, 0)),
            pl.BlockSpec((tm, cwid), row),
            pl.BlockSpec((tm, D), row),
            pl.BlockSpec((tm, D), row),
        ],
        out_shape=[
            jax.ShapeDtypeStruct((T, aw), MXU_DTYPE),
            jax.ShapeDtypeStruct((T, aw), MXU_DTYPE),
            jax.ShapeDtypeStruct((T, aw), MXU_DTYPE),
            jax.ShapeDtypeStruct((nbatch, nb_seq, aw), F32),
            jax.ShapeDtypeStruct((T, cwid), MXU_DTYPE),
            jax.ShapeDtypeStruct((T, D), F32),
            jax.ShapeDtypeStruct((T, D), F32),
        ],
        scratch_shapes=[pltpu.VMEM((tm + SUBLANES, cwid), F32)],
        compiler_params=pltpu.CompilerParams(
            dimension_semantics=("arbitrary",), vmem_limit_bytes=VMEM_LIMIT_BYTES),
        name="inproj",
    )(x2, w_in_b, cos_t, sa_t, sb_t, conv_w)


def _attn_kernel(q_ref, k_ref, v_ref, km_ref, o_ref, vt_ref, sel_ref):
    qb = pl.program_id(2)
    blk = MOBA_BLOCK
    nb = km_ref.shape[1]

    @pl.when(qb == 0)
    def _():
        for b in range(nb):
            vt_ref[b] = v_ref[0, b * blk:(b + 1) * blk, :].astype(F32).T.astype(vt_ref.dtype)

    q2 = q_ref[0]
    km_hi, km_lo = _split_bf16(km_ref[0])
    blk_id = lax.broadcasted_iota(I32, (nb, blk), 0).astype(F32)
    past = blk_id < qb.astype(F32)
    key_pos = lax.broadcasted_iota(I32, (blk, blk), 0)
    qry_pos = lax.broadcasted_iota(I32, (blk, blk), 1)
    own_start = pl.multiple_of(qb * blk, blk)

    heads = range(HEADS_PER_LANE_GROUP)
    q2t = q2.astype(F32).T
    dim = lax.broadcasted_iota(I32, q2t.shape, 0)
    qts = []
    for h in heads:
        in_head = (dim >= h * HEAD_DIM) & (dim < (h + 1) * HEAD_DIM)
        qt = jnp.where(in_head, q2t, 0.0).astype(MXU_DTYPE)
        qts.append(qt)

        gate = jnp.where(past, _dot(km_hi, qt) + _dot(km_lo, qt), -jnp.inf)
        sel = jnp.zeros((nb, blk), F32)
        for _ in range(MOBA_TOPK):
            mx = jnp.max(gate, axis=0, keepdims=True)
            idx = jnp.min(jnp.where(gate == mx, blk_id, float(nb)), axis=0, keepdims=True)
            pick = blk_id == idx
            sel = jnp.where(pick, jnp.where(past, 1.0, sel), sel)
            gate = jnp.where(pick, -jnp.inf, gate)
        sel_ref[h] = sel

    def v_t(jb, h):
        return vt_ref[jb, h * HEAD_DIM:(h + 1) * HEAD_DIM, :]

    k_own = k_ref[0, pl.ds(own_start, blk), :]
    scores = [_dot(k_own, qts[h]) for h in heads]
    state = []
    for h in heads:
        s = jnp.where(key_pos <= qry_pos, scores[h], NEG)
        m = jnp.max(s, axis=0, keepdims=True)
        p = jnp.exp(s - m)
        state += [m, jnp.sum(p, axis=0, keepdims=True), _dot(v_t(qb, h), p.astype(MXU_DTYPE))]

    def make_step(nk):
        def step(b0, carry):
            kb = k_ref[0, pl.ds(pl.multiple_of(b0 * blk, blk), nk * blk), :]
            scores = [_dot(kb, qts[h]) for h in heads]
            new = []
            for h in heads:
                m, l, acc = carry[3 * h:3 * h + 3]
                parts = [jnp.where(sel_ref[h, pl.ds(b0 + t, 1), :] > 0.5,
                                   scores[h][t * blk:(t + 1) * blk], NEG) for t in range(nk)]
                s = jnp.concatenate(parts, axis=0) if nk > 1 else parts[0]
                m_new = jnp.maximum(m, jnp.max(s, axis=0, keepdims=True))
                alpha = jnp.exp(m - m_new)
                p = jnp.exp(s - m_new)
                vt = [v_t(b0 + t, h) for t in range(nk)]
                vt = jnp.concatenate(vt, axis=1) if nk > 1 else vt[0]
                new += [m_new, alpha * l + jnp.sum(p, axis=0, keepdims=True),
                        alpha * acc + _dot(vt, p.astype(MXU_DTYPE))]
            return tuple(new)
        return step

    quad, pair, single = make_step(4), make_step(2), make_step(1)
    n_quads = lax.shift_right_logical(qb, 2)
    state = lax.fori_loop(0, n_quads, lambda i, c: quad(4 * i, c), tuple(state))
    state = lax.fori_loop(0, lax.shift_right_logical(qb, 1) & 1, lambda i, c: pair(4 * n_quads, c), state)
    state = lax.fori_loop(0, qb & 1, lambda i, c: single(qb - 1, c), state)
    outs = [state[3 * h + 2] / state[3 * h + 1] for h in heads]
    o_ref[0] = jnp.concatenate(outs, axis=0).T.astype(o_ref.dtype)


def _attention(q, k, v, km, *, nbatch, seq, aw):
    blk = MOBA_BLOCK
    nb = seq // blk
    assert nb % SUBLANES == 0
    q3, k3, v3 = (t.reshape(nbatch, seq, aw) for t in (q, k, v))
    out = pl.pallas_call(
        _attn_kernel,
        grid=(nbatch, aw // LANES, nb),
        in_specs=[
            pl.BlockSpec((1, blk, LANES), lambda b, g, i: (b, i, g)),
            pl.BlockSpec((1, seq, LANES), lambda b, g, i: (b, 0, g)),
            pl.BlockSpec((1, seq, LANES), lambda b, g, i: (b, 0, g)),
            pl.BlockSpec((1, nb, LANES), lambda b, g, i: (b, 0, g)),
        ],
        out_specs=pl.BlockSpec((1, blk, LANES), lambda b, g, i: (b, i, g)),
        out_shape=jax.ShapeDtypeStruct((nbatch, seq, aw), MXU_DTYPE),
        scratch_shapes=[
            pltpu.VMEM((nb, LANES, blk), MXU_DTYPE),
            pltpu.VMEM((HEADS_PER_LANE_GROUP, nb, blk), F32),
        ],
        compiler_params=pltpu.CompilerParams(
            dimension_semantics=("arbitrary", "arbitrary", "arbitrary"),
            vmem_limit_bytes=VMEM_LIMIT_BYTES),
        name="moba_attn",
    )(q3, k3, v3, km)
    return out.reshape(nbatch * seq, aw)


def _post_kernel(attn_ref, yc_ref, ga_ref, gc_ref, x_ref, wao_ref, wco_ref, wo_ref,
                 g1_ref, b1_ref, wrh_ref, wrl_ref, rb_ref,
                 h1_ref, h1w_ref, ti_ref, tw_ref, rk_ref, cnt_ref, cnt_acc, *, alpha):
    i = pl.program_id(0)
    tm = x_ref.shape[0]
    ne = wrh_ref.shape[0]

    y_attn = _dot(attn_ref[...], wao_ref[...])
    y_conv = _dot(yc_ref[...], wco_ref[...])
    merged = ga_ref[...] * y_attn + gc_ref[...] * y_conv
    mix = _dot(merged.astype(MXU_DTYPE), wo_ref[...])
    h1 = _layer_norm(alpha * x_ref[...] + mix, g1_ref[...], b1_ref[...])
    h1_ref[...] = h1
    _tt_store(h1w_ref, _pack_pairs(h1))

    h_hi, h_lo = _split_bf16(h1)
    logits = _dot_nt(wrh_ref[...], h_hi) + _dot_nt(wrh_ref[...], h_lo) + _dot_nt(wrl_ref[...], h_hi)
    scores = jax.nn.sigmoid(logits)
    choice = scores + rb_ref[...]

    gsz = ne // N_GROUPS
    gshape = (N_GROUPS, gsz, tm)
    c3 = choice.reshape(gshape)
    in_grp = lax.broadcasted_iota(I32, gshape, 1).astype(F32)
    m1 = jnp.max(c3, axis=1, keepdims=True)
    i1 = jnp.min(jnp.where(c3 == m1, in_grp, float(gsz)), axis=1, keepdims=True)
    m2 = jnp.max(jnp.where(in_grp == i1, -jnp.inf, c3), axis=1, keepdims=True)
    gscore = jnp.broadcast_to(m1 + m2, gshape).reshape(ne, tm)

    eid = lax.broadcasted_iota(I32, (ne, tm), 0).astype(F32)
    gid = lax.broadcasted_iota(I32, gshape, 0).astype(F32).reshape(ne, tm)
    cand = jnp.full((ne, tm), -jnp.inf, F32)
    for _ in range(TOPK_GROUPS):
        mx = jnp.max(gscore, axis=0, keepdims=True)
        idx = jnp.min(jnp.where(gscore == mx, gid, float(N_GROUPS)), axis=0, keepdims=True)
        pick = gid == idx
        cand = jnp.where(pick, choice, cand)
        gscore = jnp.where(pick, -jnp.inf, gscore)

    selmat = jnp.zeros((ne, tm), F32)
    idxs, svals = [], []
    for _ in range(TOP_K):
        mx = jnp.max(cand, axis=0, keepdims=True)
        idx = jnp.min(jnp.where(cand == mx, eid, float(ne)), axis=0, keepdims=True)
        pick = eid == idx
        svals.append(jnp.sum(jnp.where(pick, scores, 0.0), axis=0, keepdims=True))
        idxs.append(idx)
        selmat = jnp.where(pick, 1.0, selmat)
        cand = jnp.where(pick, -jnp.inf, cand)
    ssum = svals[0]
    for r in range(1, TOP_K):
        ssum = ssum + svals[r]

    @pl.when(i == 0)
    def _():
        cnt_acc[...] = jnp.zeros(cnt_acc.shape, F32)

    tr = lax.broadcasted_iota(I32, (tm, tm), 0)
    tc = lax.broadcasted_iota(I32, (tm, tm), 1)
    upper = jnp.where(tr < tc, 1.0, 0.0).astype(MXU_DTYPE)
    selb = selmat.astype(MXU_DTYPE)
    base = cnt_acc[...]
    rank = _dot(selb, upper) + jnp.concatenate([base] * (tm // LANES), axis=1)
    new_cnt = base + _dot(selb, jnp.ones((tm, LANES), MXU_DTYPE))
    cnt_acc[...] = new_cnt
    cnt_ref[...] = new_cnt

    for r in range(TOP_K):
        pick = eid == idxs[r]
        ti_ref[r:r + 1, :] = idxs[r].astype(I32)
        tw_ref[r:r + 1, :] = svals[r] / ssum * ROUTED_SCALE
        rk_ref[r:r + 1, :] = jnp.sum(jnp.where(pick, rank, 0.0), axis=0, keepdims=True).astype(I32)


def _post(attn, yc, ga, gc, x2, wao, wco, wo, g1, b1, wrh, wrl, rb, *, alpha):
    T, D = x2.shape
    tm = TM_POST
    assert D == 2 * TOK_ROWS * LANES
    aw, cwid = attn.shape[1], yc.shape[1]
    ne = wrh.shape[0]
    row = lambda i: (i, 0)
    full = lambda i: (0, 0)
    col = lambda i: (0, i)
    return pl.pallas_call(
        functools.partial(_post_kernel, alpha=alpha),
        grid=(T // tm,),
        in_specs=[
            pl.BlockSpec((tm, aw), row), pl.BlockSpec((tm, cwid), row),
            pl.BlockSpec((tm, D), row), pl.BlockSpec((tm, D), row), pl.BlockSpec((tm, D), row),
            pl.BlockSpec(wao.shape, full), pl.BlockSpec(wco.shape, full), pl.BlockSpec(wo.shape, full),
            pl.BlockSpec(g1.shape, full), pl.BlockSpec(b1.shape, full),
            pl.BlockSpec(wrh.shape, full), pl.BlockSpec(wrl.shape, full), pl.BlockSpec(rb.shape, full),
        ],
        out_specs=[
            pl.BlockSpec((tm, D), row),
            pl.BlockSpec((tm * TOK_ROWS, LANES), row),
            pl.BlockSpec((TOP_K, tm), col), pl.BlockSpec((TOP_K, tm), col), pl.BlockSpec((TOP_K, tm), col),
            pl.BlockSpec((ne, LANES), full),
        ],
        out_shape=[
            jax.ShapeDtypeStruct((T, D), F32),
            jax.ShapeDtypeStruct((T * TOK_ROWS, LANES), U32),
            jax.ShapeDtypeStruct((TOP_K, T), I32),
            jax.ShapeDtypeStruct((TOP_K, T), F32),
            jax.ShapeDtypeStruct((TOP_K, T), I32),
            jax.ShapeDtypeStruct((ne, LANES), F32),
        ],
        scratch_shapes=[pltpu.VMEM((ne, LANES), F32)],
        compiler_params=pltpu.CompilerParams(
            dimension_semantics=("arbitrary",), vmem_limit_bytes=VMEM_LIMIT_BYTES),
        name="post_route",
    )(attn, yc, ga, gc, x2, wao, wco, wo, g1, b1, wrh, wrl, rb)


def _slots_kernel(pstart_ref, ti_ref, rk_ref, dest_ref):
    ti = ti_ref[...]
    start = lax.fori_loop(0, N_EXPERTS, lambda e, acc: jnp.where(ti == e, pstart_ref[e], acc),
                          jnp.zeros(ti.shape, I32))
    dest_ref[...] = start + rk_ref[...]


def _slots(pstart, topi, rnk):
    T = topi.shape[1]
    tc = min(T, 2048)
    blk = pl.BlockSpec((TOP_K, tc), lambda i, *_: (0, i))
    return pl.pallas_call(
        _slots_kernel,
        grid_spec=pltpu.PrefetchScalarGridSpec(
            num_scalar_prefetch=1, grid=(T // tc,), in_specs=[blk, blk], out_specs=blk),
        out_shape=jax.ShapeDtypeStruct((TOP_K, T), I32),
        compiler_params=pltpu.CompilerParams(dimension_semantics=("arbitrary",)),
        name="slots",
    )(pstart, topi, rnk)


def _dispatch_kernel(zrow_ref, nzero_ref, dest_ref, h1_ref, xs_hbm, zbuf, zsem, sem):
    i = pl.program_id(0)
    tm = dest_ref.shape[1]
    zrows = zbuf.shape[0]

    def zero_copy(z):
        start = pl.multiple_of(zrow_ref[z] * TOK_ROWS, zrows)
        return pltpu.make_async_copy(zbuf, xs_hbm.at[pl.ds(start, zrows)], zsem)

    @pl.when(i == 0)
    def _():
        zbuf[...] = jnp.zeros(zbuf.shape, zbuf.dtype)
        nz = nzero_ref[0]
        lax.fori_loop(0, nz, lambda z, c: (zero_copy(z).start(), c)[1], 0)
        lax.fori_loop(0, nz, lambda z, c: (zero_copy(z).wait(), c)[1], 0)

    def copy(k, j):
        return _tile_copy(h1_ref, xs_hbm, j, dest_ref[k, j], sem)

    def issue(j, c):
        for k in range(TOP_K):
            copy(k, j).start(priority=k % 2)
        return c

    def drain(j, c):
        for k in range(TOP_K):
            copy(k, j).wait()
        return c

    lax.fori_loop(0, tm, issue, 0)
    lax.fori_loop(0, tm, drain, 0)


def _dispatch(zrow, nzero, dest, h1, *, n_rows):
    T = h1.shape[0] // TOK_ROWS
    tm = TM_DISPATCH
    return pl.pallas_call(
        _dispatch_kernel,
        grid_spec=pltpu.PrefetchScalarGridSpec(
            num_scalar_prefetch=2,
            grid=(T // tm,),
            in_specs=[pl.BlockSpec((TOP_K, tm), lambda i, *_: (0, i), memory_space=pltpu.SMEM),
                      pl.BlockSpec((tm * TOK_ROWS, LANES), lambda i, *_: (i, 0))],
            out_specs=pl.BlockSpec(memory_space=pl.ANY),
            scratch_shapes=[pltpu.VMEM((TM_EXPERT * TOK_ROWS, LANES), h1.dtype),
                            pltpu.SemaphoreType.DMA, pltpu.SemaphoreType.DMA],
        ),
        out_shape=jax.ShapeDtypeStruct((n_rows * TOK_ROWS, LANES), h1.dtype),
        compiler_params=pltpu.CompilerParams(
            dimension_semantics=("arbitrary",), vmem_limit_bytes=VMEM_LIMIT_BYTES),
        name="dispatch",
    )(zrow, nzero, dest, h1)


def _experts_kernel(be_ref, nused_ref, xs_ref, wg_ref, wu_ref, wd_ref, ys_ref, wgu_b, wd_b):
    i = pl.program_id(0)
    ed = wg_ref.shape[2]

    @pl.when(i < nused_ref[0])
    def _():
        prev = be_ref[jnp.maximum(i - 1, 0)]

        @pl.when((i == 0) | (be_ref[i] != prev))
        def _():
            wgu_b[:, 0:ed] = wg_ref[0].astype(MXU_DTYPE)
            wgu_b[:, ed:2 * ed] = wu_ref[0].astype(MXU_DTYPE)
            wd_b[...] = wd_ref[0].astype(MXU_DTYPE)

        te = xs_ref.shape[0] // TOK_ROWS
        x_lo, x_hi = _unpack_pairs(_tt_load(xs_ref, te))
        xb = jnp.concatenate([x_lo.astype(MXU_DTYPE), x_hi.astype(MXU_DTYPE)], axis=1)
        gu = _dot(xb, wgu_b[...])
        hb = _silu(gu[:, 0:ed]) * gu[:, ed:2 * ed]
        _tt_store(ys_ref, _pack_pairs(_dot(hb.astype(MXU_DTYPE), wd_b[...])))


def _experts(block_e, nused, xs, wg, wu, wd):
    te = TM_EXPERT
    D, ed = wg.shape[1], wg.shape[2]
    nblk = xs.shape[0] // (te * TOK_ROWS)
    blk_map = lambda i, be, nu: (jnp.minimum(i, nu[0] - 1), 0)
    w_map = lambda i, be, nu: (be[i], 0, 0)
    return pl.pallas_call(
        _experts_kernel,
        grid_spec=pltpu.PrefetchScalarGridSpec(
            num_scalar_prefetch=2,
            grid=(nblk,),
            in_specs=[
                pl.BlockSpec((te * TOK_ROWS, LANES), blk_map),
                pl.BlockSpec((1, D, ed), w_map),
                pl.BlockSpec((1, D, ed), w_map),
                pl.BlockSpec((1, ed, D), w_map),
            ],
            out_specs=pl.BlockSpec((te * TOK_ROWS, LANES), blk_map),
            scratch_shapes=[pltpu.VMEM((D, 2 * ed), MXU_DTYPE), pltpu.VMEM((ed, D), MXU_DTYPE)],
        ),
        out_shape=jax.ShapeDtypeStruct(xs.shape, xs.dtype),
        compiler_params=pltpu.CompilerParams(
            dimension_semantics=("arbitrary",), vmem_limit_bytes=VMEM_LIMIT_BYTES),
        name="experts",
    )(block_e, nused, xs, wg, wu, wd)


def _final_kernel(dest_ref, dest_next_ref, tw_ref, h1_ref, p_ref, ys_hbm, wsgu_ref, wsd_ref, wpg_ref,
                  wpp_ref, g2_ref, b2_ref, g3_ref, b3_ref, o_ref, gbuf_ref, routed_ref, sems, *, alpha):
    i = pl.program_id(0)
    n = pl.num_programs(0)
    tm = o_ref.shape[0]
    sd = wsd_ref.shape[0]
    cur = i % 2
    gbuf = gbuf_ref.at[cur]

    def gather(idx_ref, slot, wait):
        def copy(k, j):
            return _tile_copy(ys_hbm, gbuf_ref.at[slot, k], idx_ref[k, j], j, sems.at[slot])

        def body(j, c):
            for k in range(TOP_K):
                if wait:
                    copy(k, j).wait()
                else:
                    copy(k, j).start(priority=k % 2)
            return c

        lax.fori_loop(0, tm, body, 0)

    @pl.when(i == 0)
    def _():
        gather(dest_ref, 0, wait=False)

    @pl.when(i + 1 < n)
    def _():
        gather(dest_next_ref, 1 - cur, wait=False)

    h1 = h1_ref[...]
    hb = h1.astype(MXU_DTYPE)
    gu = _dot(hb, wsgu_ref[...])
    shared = _dot((_silu(gu[:, 0:sd]) * gu[:, sd:2 * sd]).astype(MXU_DTYPE), wsd_ref[...])

    gather(dest_ref, cur, wait=True)

    tw = tw_ref[...]
    wt = jnp.concatenate([tw, jnp.zeros((LANES - TOP_K, tm), F32)], axis=0).T
    half = routed_ref.shape[1] // 2
    for r in range(tm // SUBLANES):
        wr = wt[r * SUBLANES:(r + 1) * SUBLANES, :]
        wb = [jnp.broadcast_to(wr[:, k:k + 1], (SUBLANES, LANES)) for k in range(TOP_K)]
        for s in range(TOK_ROWS):
            rows = pl.ds(r * SUBLANES * TOK_ROWS + s, SUBLANES, stride=TOK_ROWS)
            acc_lo = acc_hi = None
            for k in range(TOP_K):
                y_lo, y_hi = _unpack_pairs(gbuf[k, rows, :])
                acc_lo = wb[k] * y_lo if k == 0 else acc_lo + wb[k] * y_lo
                acc_hi = wb[k] * y_hi if k == 0 else acc_hi + wb[k] * y_hi
            tok = slice(r * SUBLANES, (r + 1) * SUBLANES)
            routed_ref[tok, s * LANES:(s + 1) * LANES] = acc_lo
            routed_ref[tok, half + s * LANES:half + (s + 1) * LANES] = acc_hi
    routed = routed_ref[...]

    h2 = _layer_norm(alpha * h1 + (routed + shared), g2_ref[...], b2_ref[...])
    gate = jax.nn.sigmoid(_dot(h2.astype(MXU_DTYPE), wpg_ref[...]))
    ple = gate * _dot(p_ref[...].astype(MXU_DTYPE), wpp_ref[...])
    o_ref[...] = _layer_norm(alpha * h2 + ple, g3_ref[...], b3_ref[...])


def _final(dest, topw, h1, p2, ys, wsgu, wsd, wpg, wpp, g2, b2, g3, b3, *, alpha):
    T, D = p2.shape[0], wpg.shape[0]
    tm = TM_FINAL
    n_tiles = T // tm
    row = lambda i: (i, 0)
    full = lambda i: (0, 0)
    col = lambda i: (0, i)
    col_next = lambda i: (0, jnp.minimum(i + 1, n_tiles - 1))
    return pl.pallas_call(
        functools.partial(_final_kernel, alpha=alpha),
        grid=(n_tiles,),
        in_specs=[
            pl.BlockSpec((TOP_K, tm), col, memory_space=pltpu.SMEM),
            pl.BlockSpec((TOP_K, tm), col_next, memory_space=pltpu.SMEM),
            pl.BlockSpec((TOP_K, tm), col),
            pl.BlockSpec((tm, D), row),
            pl.BlockSpec((tm, p2.shape[1]), row),
            pl.BlockSpec(memory_space=pl.ANY),
            pl.BlockSpec(wsgu.shape, full), pl.BlockSpec(wsd.shape, full),
            pl.BlockSpec(wpg.shape, full), pl.BlockSpec(wpp.shape, full),
            pl.BlockSpec(g2.shape, full), pl.BlockSpec(b2.shape, full),
            pl.BlockSpec(g3.shape, full), pl.BlockSpec(b3.shape, full),
        ],
        out_specs=pl.BlockSpec((tm, D), row),
        out_shape=jax.ShapeDtypeStruct((T, D), F32),
        scratch_shapes=[pltpu.VMEM((2, TOP_K, tm * TOK_ROWS, LANES), ys.dtype), pltpu.VMEM((tm, D), F32),
                        pltpu.SemaphoreType.DMA((2,))],
        compiler_params=pltpu.CompilerParams(
            dimension_semantics=("arbitrary",), vmem_limit_bytes=VMEM_LIMIT_BYTES),
        name="final",
    )(dest, dest, topw, h1, p2, ys, wsgu, wsd, wpg, wpp, g2, b2, g3, b3)


def _rope_tables(seq):
    half = ROT_DIM // 2
    inv_freq = ROPE_THETA ** (-jnp.arange(0, ROT_DIM, 2, dtype=F32) / ROT_DIM)
    ang = jnp.arange(seq, dtype=I32).astype(F32)[:, None] * inv_freq[None, :]
    cos, sin = jnp.cos(ang), jnp.sin(ang)
    ones = jnp.ones((seq, HEAD_DIM - ROT_DIM), F32)
    zeros = jnp.zeros((seq, HEAD_DIM - ROT_DIM), F32)
    zh = jnp.zeros((seq, half), F32)
    cos_h = jnp.concatenate([cos, cos, ones], axis=1)
    sa_h = jnp.concatenate([-sin, zh, zeros], axis=1)
    sb_h = jnp.concatenate([zh, sin, zeros], axis=1)
    rep = lambda t: jnp.concatenate([t] * HEADS_PER_LANE_GROUP, axis=1)
    return rep(cos_h), rep(sa_h), rep(sb_h)


def _expert_layout(counts, n_blocks):
    te = TM_EXPERT
    nblk_e = (counts + te - 1) // te
    blk_end = jnp.cumsum(nblk_e)
    blk_start = blk_end - nblk_e
    nused = blk_end[-1]
    pstart = (blk_start * te).astype(I32)
    bid = jnp.arange(n_blocks, dtype=I32)
    block_e = jnp.searchsorted(blk_end, jnp.minimum(bid, nused - 1), side='right').astype(I32)
    block_e = jnp.minimum(block_e, N_EXPERTS - 1)
    partial = (counts % te) != 0
    order = jnp.argsort(jnp.logical_not(partial), stable=True).astype(I32)
    n_partial = jnp.sum(partial).astype(I32)
    last_blk = (blk_end - 1).astype(I32)
    zi = jnp.arange(n_blocks + N_EXPERTS, dtype=I32)
    zblk = jnp.where(zi < n_partial, last_blk[order[jnp.minimum(zi, N_EXPERTS - 1)]],
                     nused + (zi - n_partial))
    nzero = n_partial + (n_blocks - nused)
    zrow = (jnp.clip(zblk, 0, n_blocks - 1) * te).astype(I32)
    return pstart, block_e, nused.astype(I32).reshape(1), zrow, nzero.astype(I32).reshape(1)


def _layer(h, p2, w_in, conv_w, w_attn_out, w_conv_out, w_out, ln1_g, ln1_b, w_router, router_bias,
           w_exp_gate, w_exp_up, w_exp_down, w_sh_gate, w_sh_up, w_sh_down, ln2_g, ln2_b,
           w_ple_gate, w_ple_proj, ln3_g, ln3_b, *, nbatch, seq, alpha):
    T, D = h.shape
    aw = w_attn_out.shape[0]
    cwid = w_conv_out.shape[0]
    bf = lambda w: w.astype(MXU_DTYPE)
    rowv = lambda g: g.reshape(1, -1)

    cos_t, sa_t, sb_t = _rope_tables(seq)
    q, k, v, km, yc, ga, gc = _inproj(h, bf(w_in), cos_t, sa_t, sb_t, conv_w, seq=seq, aw=aw, cwid=cwid)
    attn = _attention(q, k, v, km, nbatch=nbatch, seq=seq, aw=aw)

    wr_t = w_router.astype(F32).T
    wrh = wr_t.astype(MXU_DTYPE)
    wrl = (wr_t - wrh.astype(F32)).astype(MXU_DTYPE)
    h1, h1w, topi, topw, rnk, cnt = _post(
        attn, yc, ga, gc, h, bf(w_attn_out), bf(w_conv_out), bf(w_out), rowv(ln1_g), rowv(ln1_b),
        wrh, wrl, router_bias.astype(F32).reshape(-1, 1), alpha=alpha)

    n_blocks = (T * TOP_K) // TM_EXPERT + N_EXPERTS
    counts = cnt[:, 0].astype(I32)
    pstart, block_e, nused, zrow, nzero = _expert_layout(counts, n_blocks)
    dest = _slots(pstart, topi, rnk)
    xs = _dispatch(zrow, nzero, dest, h1w, n_rows=n_blocks * TM_EXPERT)
    ys = _experts(block_e, nused, xs, w_exp_gate, w_exp_up, w_exp_down)

    wsgu = jnp.concatenate([bf(w_sh_gate), bf(w_sh_up)], axis=1)
    return _final(dest, topw, h1, p2, ys, wsgu, bf(w_sh_down), bf(w_ple_gate), bf(w_ple_proj),
                  rowv(ln2_g), rowv(ln2_b), rowv(ln3_g), rowv(ln3_b), alpha=alpha)


def kernel(x, p, w_in, conv_w, w_attn_out, w_conv_out, w_out, ln1_g, ln1_b, w_router, router_bias,
           w_exp_gate, w_exp_up, w_exp_down, w_sh_gate, w_sh_up, w_sh_down, ln2_g, ln2_b,
           w_ple_gate, w_ple_proj, ln3_g, ln3_b):
    nbatch, seq, d_model = x.shape
    depth = w_in.shape[0]
    alpha = (2 * depth) ** 0.25
    assert seq % MOBA_BLOCK == 0
    h = x.reshape(nbatch * seq, d_model)
    for i in range(depth):
        h = _layer(h, p[i].reshape(nbatch * seq, -1), w_in[i], conv_w[i], w_attn_out[i], w_conv_out[i],
                   w_out[i], ln1_g[i], ln1_b[i], w_router[i], router_bias[i],
                   w_exp_gate[i], w_exp_up[i], w_exp_down[i], w_sh_gate[i], w_sh_up[i], w_sh_down[i],
                   ln2_g[i], ln2_b[i], w_ple_gate[i], w_ple_proj[i], ln3_g[i], ln3_b[i],
                   nbatch=nbatch, seq=seq, alpha=alpha)
    return h.reshape(nbatch, seq, d_model)
```

```python
import functools
import math

import jax
import jax.numpy as jnp
from jax import lax
from jax.experimental import pallas as pl
from jax.experimental.pallas import tpu as pltpu

N_HEADS = 8
HEAD_DIM = 64
ROT_DIM = 16
ROPE_THETA = 500000.0
MOBA_BLOCK = 256
MOBA_TOPK = 3
CONV_K = 3
N_EXPERTS = 256
TOP_K = 8
N_GROUPS = 8
TOPK_GROUPS = 4
ROUTED_SCALE = 2.5
LN_EPS = 1e-5
NEG = -1e30

LANES = 128
SUBLANES = 8
HEADS_PER_LANE_GROUP = LANES // HEAD_DIM
VMEM_LIMIT_BYTES = 56 * 1024 * 1024

MXU_DTYPE = jnp.bfloat16
F32 = jnp.float32
I32 = jnp.int32
U32 = jnp.uint32

TM_INPROJ = 256
TM_POST = 256
TM_DISPATCH = 256
TM_EXPERT = 256
TM_FINAL = 256


def _dot(a, b):
    return jnp.dot(a, b, preferred_element_type=F32)


def _dot_nt(a, b):
    return lax.dot_general(a, b, (((1,), (1,)), ((), ())), preferred_element_type=F32)


def _split_bf16(a):
    hi = a.astype(MXU_DTYPE)
    lo = (a - hi.astype(F32)).astype(MXU_DTYPE)
    return hi, lo


TOK_ROWS = 4
HI_MASK = 0xFFFF0000


def _pack_pairs(f):
    half = f.shape[1] // 2
    bits = lambda t: lax.bitcast_convert_type(t.astype(jnp.bfloat16).astype(F32), U32)
    return (bits(f[:, half:]) & jnp.uint32(HI_MASK)) | lax.shift_right_logical(bits(f[:, :half]), jnp.uint32(16))


def _unpack_pairs(w):
    lo = lax.bitcast_convert_type(lax.shift_left(w, jnp.uint32(16)), F32)
    hi = lax.bitcast_convert_type(w & jnp.uint32(HI_MASK), F32)
    return lo, hi


def _tt_load(ref, n_tok):
    chunks = [ref[pl.ds(s, n_tok, stride=TOK_ROWS), :] for s in range(TOK_ROWS)]
    return jnp.concatenate(chunks, axis=1)


def _tt_store(ref, words):
    n_tok = words.shape[0]
    for s in range(TOK_ROWS):
        ref[pl.ds(s, n_tok, stride=TOK_ROWS), :] = words[:, s * LANES:(s + 1) * LANES]


def _tile_copy(src, dst, src_tok, dst_tok, sem):
    rows = lambda t: pl.ds(pl.multiple_of(t * TOK_ROWS, TOK_ROWS), TOK_ROWS)
    return pltpu.make_async_copy(src.at[rows(src_tok)], dst.at[rows(dst_tok)], sem)


def _layer_norm(t, g, b):
    mu = jnp.mean(t, axis=-1, keepdims=True)
    d = t - mu
    var = jnp.mean(d * d, axis=-1, keepdims=True)
    return d * lax.rsqrt(var + LN_EPS) * g + b


def _silu(t):
    return t * jax.nn.sigmoid(t)


def _inproj_kernel(x_ref, w_ref, cos_ref, sa_ref, sb_ref, cw_ref,
                   q_ref, k_ref, v_ref, km_ref, yc_ref, ga_ref, gc_ref, ubuf,
                   *, tiles_per_seq, aw, cwid, d_model):
    tm = x_ref.shape[0]
    j = pl.program_id(0) % tiles_per_seq
    xb = x_ref[...].astype(MXU_DTYPE)
    cos, sa, sb = cos_ref[...], sa_ref[...], sb_ref[...]

    def proj(c0, n):
        return _dot(xb, w_ref[:, c0:c0 + n])

    def rope(z):
        parts = []
        for g in range(aw // LANES):
            zg = z[:, g * LANES:(g + 1) * LANES]
            parts.append(zg * cos
                         + pltpu.roll(zg, LANES - ROT_DIM // 2, 1) * sa
                         + pltpu.roll(zg, ROT_DIM // 2, 1) * sb)
        return jnp.concatenate(parts, axis=1)

    q_ref[...] = (rope(proj(0, aw)) * (HEAD_DIM ** -0.5)).astype(q_ref.dtype)

    k = rope(proj(aw, aw))
    k_ref[...] = k.astype(k_ref.dtype)
    nb_seq = km_ref.shape[1]
    blocks_per_tile = tm // MOBA_BLOCK

    @pl.when(j == 0)
    def _():
        km_ref[...] = jnp.zeros(km_ref.shape, km_ref.dtype)

    rows = lax.broadcasted_iota(I32, (nb_seq, aw), 0)
    km = km_ref[0]
    for bi in range(blocks_per_tile):
        mean = jnp.sum(k[bi * MOBA_BLOCK:(bi + 1) * MOBA_BLOCK], axis=0, keepdims=True) * (1.0 / MOBA_BLOCK)
        km = jnp.where(rows == j * blocks_per_tile + bi, mean, km)
    km_ref[0] = km

    v_ref[...] = proj(2 * aw, aw).astype(v_ref.dtype)

    c0 = 3 * aw
    cb = proj(c0, cwid)
    u = proj(c0 + cwid, cwid) * proj(c0 + 2 * cwid, cwid)

    @pl.when(j == 0)
    def _():
        ubuf[0:SUBLANES, :] = jnp.zeros((SUBLANES, cwid), F32)

    ubuf[SUBLANES:SUBLANES + tm, :] = u
    um1 = ubuf[SUBLANES - 1:SUBLANES - 1 + tm, :]
    um2 = ubuf[SUBLANES - 2:SUBLANES - 2 + tm, :]
    conv = cw_ref[0:1, :] * um2 + cw_ref[1:2, :] * um1 + cw_ref[2:3, :] * u
    yc_ref[...] = (cb * conv).astype(yc_ref.dtype)
    ubuf[0:SUBLANES, :] = u[tm - SUBLANES:tm, :]

    g0 = c0 + 3 * cwid
    ga_ref[...] = jax.nn.sigmoid(proj(g0, d_model))
    gc_ref[...] = jax.nn.sigmoid(proj(g0 + d_model, d_model))


def _inproj(x2, w_in_b, cos_t, sa_t, sb_t, conv_w, *, seq, aw, cwid):
    T, D = x2.shape
    tm = TM_INPROJ
    assert seq % tm == 0 and tm % MOBA_BLOCK == 0 and T % seq == 0
    tiles_per_seq = seq // tm
    nb_seq = seq // MOBA_BLOCK
    nbatch = T // seq
    row = lambda i: (i, 0)
    tab = lambda i: (i % tiles_per_seq, 0)
    kern = functools.partial(_inproj_kernel, tiles_per_seq=tiles_per_seq, aw=aw, cwid=cwid, d_model=D)
    return pl.pallas_call(
        kern,
        grid=(T // tm,),
        in_specs=[
            pl.BlockSpec((tm, D), row),
            pl.BlockSpec(w_in_b.shape, lambda i: (0, 0)),
            pl.BlockSpec((tm, LANES), tab),
            pl.BlockSpec((tm, LANES), tab),
            pl.BlockSpec((tm, LANES), tab),
            pl.BlockSpec(conv_w.shape, lambda i: (0, 0)),
        ],
        out_specs=[
            pl.BlockSpec((tm, aw), row),
            pl.BlockSpec((tm, aw), row),
            pl.BlockSpec((tm, aw), row),
            pl.BlockSpec((1, nb_seq, aw), lambda i: (i // tiles_per_seq, 0, 0)),
            pl.BlockSpec((tm, cwid), row),
            pl.BlockSpec((tm, D), row),
            pl.BlockSpec((tm, D), row),
        ],
        out_shape=[
            jax.ShapeDtypeStruct((T, aw), MXU_DTYPE),
            jax.ShapeDtypeStruct((T, aw), MXU_DTYPE),
            jax.ShapeDtypeStruct((T, aw), MXU_DTYPE),
            jax.ShapeDtypeStruct((nbatch, nb_seq, aw), F32),
            jax.ShapeDtypeStruct((T, cwid), MXU_DTYPE),
            jax.ShapeDtypeStruct((T, D), F32),
            jax.ShapeDtypeStruct((T, D), F32),
        ],
        scratch_shapes=[pltpu.VMEM((tm + SUBLANES, cwid), F32)],
        compiler_params=pltpu.CompilerParams(
            dimension_semantics=("arbitrary",), vmem_limit_bytes=VMEM_LIMIT_BYTES),
        name="inproj",
    )(x2, w_in_b, cos_t, sa_t, sb_t, conv_w)


def _attn_kernel(q_ref, k_ref, v_ref, km_ref, o_ref, vt_ref, sel_ref, s_ref, m_ref, l_ref, acc_ref):
    qb = pl.program_id(1)
    blk = MOBA_BLOCK
    nb = km_ref.shape[1]
    n_groups = q_ref.shape[2] // LANES
    hpg = HEADS_PER_LANE_GROUP
    lanes = lambda g: slice(g * LANES, (g + 1) * LANES)

    @pl.when(qb == 0)
    def _():
        for b in range(nb):
            for g in range(n_groups):
                vt_ref[b, lanes(g), :] = v_ref[0, b * blk:(b + 1) * blk, lanes(g)].astype(F32).T.astype(vt_ref.dtype)

    blk_id = lax.broadcasted_iota(I32, (nb, blk), 0).astype(F32)
    past = blk_id < qb.astype(F32)
    dim = lax.broadcasted_iota(I32, (LANES, blk), 0)

    qts = []
    for g in range(n_groups):
        q2t = q_ref[0, :, lanes(g)].astype(F32).T
        km_hi, km_lo = _split_bf16(km_ref[0, :, lanes(g)])
        for h in range(hpg):
            in_head = (dim >= h * HEAD_DIM) & (dim < (h + 1) * HEAD_DIM)
            qt = jnp.where(in_head, q2t, 0.0).astype(MXU_DTYPE)
            qts.append(qt)
            gate = jnp.where(past, _dot(km_hi, qt) + _dot(km_lo, qt), -jnp.inf)
            sel = jnp.zeros((nb, blk), F32)
            for _ in range(MOBA_TOPK):
                mx = jnp.max(gate, axis=0, keepdims=True)
                idx = jnp.min(jnp.where(gate == mx, blk_id, float(nb)), axis=0, keepdims=True)
                pick = blk_id == idx
                sel = jnp.where(pick, jnp.where(past, 1.0, sel), sel)
                gate = jnp.where(pick, -jnp.inf, gate)
            sel_ref[g * hpg + h] = sel

    def v_t(jb, hd):
        return vt_ref[jb, hd * HEAD_DIM:(hd + 1) * HEAD_DIM, :]

    def scores_to_scratch(g, kg, nk):
        for h in range(hpg):
            s_ref[g % 2, h, 0:nk * blk, :] = _dot(kg, qts[g * hpg + h])

    def run_groups(key_rows, nk, update):
        scores_to_scratch(0, k_ref[0, key_rows, lanes(0)], nk)
        for g in range(n_groups):
            if g + 1 < n_groups:
                scores_to_scratch(g + 1, k_ref[0, key_rows, lanes(g + 1)], nk)
            for h in range(hpg):
                update(g * hpg + h, s_ref[g % 2, h, 0:nk * blk, :])

    key_pos = lax.broadcasted_iota(I32, (blk, blk), 0)
    qry_pos = lax.broadcasted_iota(I32, (blk, blk), 1)

    def init(hd, s_raw):
        s = jnp.where(key_pos <= qry_pos, s_raw, NEG)
        m = jnp.max(s, axis=0, keepdims=True)
        p = jnp.exp(s - m)
        m_ref[hd] = m
        l_ref[hd] = jnp.sum(p, axis=0, keepdims=True)
        acc_ref[hd] = _dot(v_t(qb, hd), p.astype(MXU_DTYPE))

    run_groups(pl.ds(pl.multiple_of(qb * blk, blk), blk), 1, init)

    def make_step(nk):
        def step(b0):
            def update(hd, s_raw):
                parts = [jnp.where(sel_ref[hd, pl.ds(b0 + t, 1), :] > 0.5,
                                   s_raw[t * blk:(t + 1) * blk], NEG) for t in range(nk)]
                s = jnp.concatenate(parts, axis=0) if nk > 1 else parts[0]
                m = m_ref[hd]
                m_new = jnp.maximum(m, jnp.max(s, axis=0, keepdims=True))
                alpha = jnp.exp(m - m_new)
                p = jnp.exp(s - m_new)
                vt = [v_t(b0 + t, hd) for t in range(nk)]
                vt = jnp.concatenate(vt, axis=1) if nk > 1 else vt[0]
                m_ref[hd] = m_new
                l_ref[hd] = alpha * l_ref[hd] + jnp.sum(p, axis=0, keepdims=True)
                acc_ref[hd] = alpha * acc_ref[hd] + _dot(vt, p.astype(MXU_DTYPE))

            run_groups(pl.ds(pl.multiple_of(b0 * blk, blk), nk * blk), nk, update)
        return step

    pair, single = make_step(2), make_step(1)
    n_pairs = lax.shift_right_logical(qb, 1)
    lax.fori_loop(0, n_pairs, lambda i, c: (pair(2 * i), c)[1], 0)
    lax.fori_loop(0, qb & 1, lambda i, c: (single(qb - 1), c)[1], 0)

    for g in range(n_groups):
        outs = [acc_ref[g * hpg + h] / l_ref[g * hpg + h] for h in range(hpg)]
        o_ref[0, :, lanes(g)] = jnp.concatenate(outs, axis=0).T.astype(o_ref.dtype)


def _attention(q, k, v, km, *, nbatch, seq, aw):
    blk = MOBA_BLOCK
    nb = seq // blk
    n_heads = aw // HEAD_DIM
    max_nk = 2
    assert nb % SUBLANES == 0
    q3, k3, v3 = (t.reshape(nbatch, seq, aw) for t in (q, k, v))
    out = pl.pallas_call(
        _attn_kernel,
        grid=(nbatch, nb),
        in_specs=[
            pl.BlockSpec((1, blk, aw), lambda b, i: (b, i, 0)),
            pl.BlockSpec((1, seq, aw), lambda b, i: (b, 0, 0)),
            pl.BlockSpec((1, seq, aw), lambda b, i: (b, 0, 0)),
            pl.BlockSpec((1, nb, aw), lambda b, i: (b, 0, 0)),
        ],
        out_specs=pl.BlockSpec((1, blk, aw), lambda b, i: (b, i, 0)),
        out_shape=jax.ShapeDtypeStruct((nbatch, seq, aw), MXU_DTYPE),
        scratch_shapes=[
            pltpu.VMEM((nb, aw, blk), MXU_DTYPE),
            pltpu.VMEM((n_heads, nb, blk), F32),
            pltpu.VMEM((2, HEADS_PER_LANE_GROUP, max_nk * blk, blk), F32),
            pltpu.VMEM((n_heads, 1, blk), F32),
            pltpu.VMEM((n_heads, 1, blk), F32),
            pltpu.VMEM((n_heads, HEAD_DIM, blk), F32),
        ],
        compiler_params=pltpu.CompilerParams(
            dimension_semantics=("arbitrary", "arbitrary"), vmem_limit_bytes=VMEM_LIMIT_BYTES),
        name="moba_attn",
    )(q3, k3, v3, km)
    return out.reshape(nbatch * seq, aw)


def _post_kernel(attn_ref, yc_ref, ga_ref, gc_ref, x_ref, wao_ref, wco_ref, wo_ref,
                 g1_ref, b1_ref, wrh_ref, wrl_ref, rb_ref,
                 h1_ref, h1w_ref, ti_ref, tw_ref, rk_ref, cnt_ref, cnt_acc, *, alpha):
    i = pl.program_id(0)
    tm = x_ref.shape[0]
    ne = wrh_ref.shape[0]

    y_attn = _dot(attn_ref[...], wao_ref[...])
    y_conv = _dot(yc_ref[...], wco_ref[...])
    merged = ga_ref[...] * y_attn + gc_ref[...] * y_conv
    mix = _dot(merged.astype(MXU_DTYPE), wo_ref[...])
    h1 = _layer_norm(alpha * x_ref[...] + mix, g1_ref[...], b1_ref[...])
    h1_ref[...] = h1
    _tt_store(h1w_ref, _pack_pairs(h1))

    h_hi, h_lo = _split_bf16(h1)
    logits = _dot_nt(wrh_ref[...], h_hi) + _dot_nt(wrh_ref[...], h_lo) + _dot_nt(wrl_ref[...], h_hi)
    scores = jax.nn.sigmoid(logits)
    choice = scores + rb_ref[...]

    gsz = ne // N_GROUPS
    gshape = (N_GROUPS, gsz, tm)
    c3 = choice.reshape(gshape)
    in_grp = lax.broadcasted_iota(I32, gshape, 1).astype(F32)
    m1 = jnp.max(c3, axis=1, keepdims=True)
    i1 = jnp.min(jnp.where(c3 == m1, in_grp, float(gsz)), axis=1, keepdims=True)
    m2 = jnp.max(jnp.where(in_grp == i1, -jnp.inf, c3), axis=1, keepdims=True)
    gscore = jnp.broadcast_to(m1 + m2, gshape).reshape(ne, tm)

    eid = lax.broadcasted_iota(I32, (ne, tm), 0).astype(F32)
    gid = lax.broadcasted_iota(I32, gshape, 0).astype(F32).reshape(ne, tm)
    cand = jnp.full((ne, tm), -jnp.inf, F32)
    for _ in range(TOPK_GROUPS):
        mx = jnp.max(gscore, axis=0, keepdims=True)
        idx = jnp.min(jnp.where(gscore == mx, gid, float(N_GROUPS)), axis=0, keepdims=True)
        pick = gid == idx
        cand = jnp.where(pick, choice, cand)
        gscore = jnp.where(pick, -jnp.inf, gscore)

    selmat = jnp.zeros((ne, tm), F32)
    idxs, svals = [], []
    for _ in range(TOP_K):
        mx = jnp.max(cand, axis=0, keepdims=True)
        idx = jnp.min(jnp.where(cand == mx, eid, float(ne)), axis=0, keepdims=True)
        pick = eid == idx
        svals.append(jnp.sum(jnp.where(pick, scores, 0.0), axis=0, keepdims=True))
        idxs.append(idx)
        selmat = jnp.where(pick, 1.0, selmat)
        cand = jnp.where(pick, -jnp.inf, cand)
    ssum = svals[0]
    for r in range(1, TOP_K):
        ssum = ssum + svals[r]

    @pl.when(i == 0)
    def _():
        cnt_acc[...] = jnp.zeros(cnt_acc.shape, F32)

    tr = lax.broadcasted_iota(I32, (tm, tm), 0)
    tc = lax.broadcasted_iota(I32, (tm, tm), 1)
    upper = jnp.where(tr < tc, 1.0, 0.0).astype(MXU_DTYPE)
    selb = selmat.astype(MXU_DTYPE)
    base = cnt_acc[...]
    rank = _dot(selb, upper) + jnp.concatenate([base] * (tm // LANES), axis=1)
    new_cnt = base + _dot(selb, jnp.ones((tm, LANES), MXU_DTYPE))
    cnt_acc[...] = new_cnt
    cnt_ref[...] = new_cnt

    for r in range(TOP_K):
        pick = eid == idxs[r]
        ti_ref[r:r + 1, :] = idxs[r].astype(I32)
        tw_ref[r:r + 1, :] = svals[r] / ssum * ROUTED_SCALE
        rk_ref[r:r + 1, :] = jnp.sum(jnp.where(pick, rank, 0.0), axis=0, keepdims=True).astype(I32)


def _post(attn, yc, ga, gc, x2, wao, wco, wo, g1, b1, wrh, wrl, rb, *, alpha):
    T, D = x2.shape
    tm = TM_POST
    assert D == 2 * TOK_ROWS * LANES
    aw, cwid = attn.shape[1], yc.shape[1]
    ne = wrh.shape[0]
    row = lambda i: (i, 0)
    full = lambda i: (0, 0)
    col = lambda i: (0, i)
    return pl.pallas_call(
        functools.partial(_post_kernel, alpha=alpha),
        grid=(T // tm,),
        in_specs=[
            pl.BlockSpec((tm, aw), row), pl.BlockSpec((tm, cwid), row),
            pl.BlockSpec((tm, D), row), pl.BlockSpec((tm, D), row), pl.BlockSpec((tm, D), row),
            pl.BlockSpec(wao.shape, full), pl.BlockSpec(wco.shape, full), pl.BlockSpec(wo.shape, full),
            pl.BlockSpec(g1.shape, full), pl.BlockSpec(b1.shape, full),
            pl.BlockSpec(wrh.shape, full), pl.BlockSpec(wrl.shape, full), pl.BlockSpec(rb.shape, full),
        ],
        out_specs=[
            pl.BlockSpec((tm, D), row),
            pl.BlockSpec((tm * TOK_ROWS, LANES), row),
            pl.BlockSpec((TOP_K, tm), col), pl.BlockSpec((TOP_K, tm), col), pl.BlockSpec((TOP_K, tm), col),
            pl.BlockSpec((ne, LANES), full),
        ],
        out_shape=[
            jax.ShapeDtypeStruct((T, D), F32),
            jax.ShapeDtypeStruct((T * TOK_ROWS, LANES), U32),
            jax.ShapeDtypeStruct((TOP_K, T), I32),
            jax.ShapeDtypeStruct((TOP_K, T), F32),
            jax.ShapeDtypeStruct((TOP_K, T), I32),
            jax.ShapeDtypeStruct((ne, LANES), F32),
        ],
        scratch_shapes=[pltpu.VMEM((ne, LANES), F32)],
        compiler_params=pltpu.CompilerParams(
            dimension_semantics=("arbitrary",), vmem_limit_bytes=VMEM_LIMIT_BYTES),
        name="post_route",
    )(attn, yc, ga, gc, x2, wao, wco, wo, g1, b1, wrh, wrl, rb)


def _slots_kernel(pstart_ref, ti_ref, rk_ref, dest_ref):
    ti = ti_ref[...]
    start = lax.fori_loop(0, N_EXPERTS, lambda e, acc: jnp.where(ti == e, pstart_ref[e], acc),
                          jnp.zeros(ti.shape, I32))
    dest_ref[...] = start + rk_ref[...]


def _slots(pstart, topi, rnk):
    T = topi.shape[1]
    tc = min(T, 2048)
    blk = pl.BlockSpec((TOP_K, tc), lambda i, *_: (0, i))
    return pl.pallas_call(
        _slots_kernel,
        grid_spec=pltpu.PrefetchScalarGridSpec(
            num_scalar_prefetch=1, grid=(T // tc,), in_specs=[blk, blk], out_specs=blk),
        out_shape=jax.ShapeDtypeStruct((TOP_K, T), I32),
        compiler_params=pltpu.CompilerParams(dimension_semantics=("arbitrary",)),
        name="slots",
    )(pstart, topi, rnk)


def _dispatch_kernel(zrow_ref, nzero_ref, dest_ref, h1_ref, xs_hbm, zbuf, zsem, sem):
    i = pl.program_id(0)
    tm = dest_ref.shape[1]
    zrows = zbuf.shape[0]

    def zero_copy(z):
        start = pl.multiple_of(zrow_ref[z] * TOK_ROWS, zrows)
        return pltpu.make_async_copy(zbuf, xs_hbm.at[pl.ds(start, zrows)], zsem)

    @pl.when(i == 0)
    def _():
        zbuf[...] = jnp.zeros(zbuf.shape, zbuf.dtype)
        nz = nzero_ref[0]
        lax.fori_loop(0, nz, lambda z, c: (zero_copy(z).start(), c)[1], 0)
        lax.fori_loop(0, nz, lambda z, c: (zero_copy(z).wait(), c)[1], 0)

    def copy(k, j):
        return _tile_copy(h1_ref, xs_hbm, j, dest_ref[k, j], sem)

    def issue(j, c):
        for k in range(TOP_K):
            copy(k, j).start(priority=k % 2)
        return c

    def drain(j, c):
        for k in range(TOP_K):
            copy(k, j).wait()
        return c

    lax.fori_loop(0, tm, issue, 0)
    lax.fori_loop(0, tm, drain, 0)


def _dispatch(zrow, nzero, dest, h1, *, n_rows):
    T = h1.shape[0] // TOK_ROWS
    tm = TM_DISPATCH
    return pl.pallas_call(
        _dispatch_kernel,
        grid_spec=pltpu.PrefetchScalarGridSpec(
            num_scalar_prefetch=2,
            grid=(T // tm,),
            in_specs=[pl.BlockSpec((TOP_K, tm), lambda i, *_: (0, i), memory_space=pltpu.SMEM),
                      pl.BlockSpec((tm * TOK_ROWS, LANES), lambda i, *_: (i, 0))],
            out_specs=pl.BlockSpec(memory_space=pl.ANY),
            scratch_shapes=[pltpu.VMEM((TM_EXPERT * TOK_ROWS, LANES), h1.dtype),
                            pltpu.SemaphoreType.DMA, pltpu.SemaphoreType.DMA],
        ),
        out_shape=jax.ShapeDtypeStruct((n_rows * TOK_ROWS, LANES), h1.dtype),
        compiler_params=pltpu.CompilerParams(
            dimension_semantics=("arbitrary",), vmem_limit_bytes=VMEM_LIMIT_BYTES),
        name="dispatch",
    )(zrow, nzero, dest, h1)


def _experts_kernel(be_ref, nused_ref, xs_ref, wg_ref, wu_ref, wd_ref, ys_ref, wgu_b, wd_b):
    i = pl.program_id(0)
    ed = wg_ref.shape[2]

    @pl.when(i < nused_ref[0])
    def _():
        prev = be_ref[jnp.maximum(i - 1, 0)]

        @pl.when((i == 0) | (be_ref[i] != prev))
        def _():
            wgu_b[:, 0:ed] = wg_ref[0].astype(MXU_DTYPE)
            wgu_b[:, ed:2 * ed] = wu_ref[0].astype(MXU_DTYPE)
            wd_b[...] = wd_ref[0].astype(MXU_DTYPE)

        te = xs_ref.shape[0] // TOK_ROWS
        x_lo, x_hi = _unpack_pairs(_tt_load(xs_ref, te))
        xb = jnp.concatenate([x_lo.astype(MXU_DTYPE), x_hi.astype(MXU_DTYPE)], axis=1)
        gu = _dot(xb, wgu_b[...])
        hb = _silu(gu[:, 0:ed]) * gu[:, ed:2 * ed]
        _tt_store(ys_ref, _pack_pairs(_dot(hb.astype(MXU_DTYPE), wd_b[...])))


def _experts(block_e, nused, xs, wg, wu, wd):
    te = TM_EXPERT
    D, ed = wg.shape[1], wg.shape[2]
    nblk = xs.shape[0] // (te * TOK_ROWS)
    blk_map = lambda i, be, nu: (jnp.minimum(i, nu[0] - 1), 0)
    w_map = lambda i, be, nu: (be[i], 0, 0)
    return pl.pallas_call(
        _experts_kernel,
        grid_spec=pltpu.PrefetchScalarGridSpec(
            num_scalar_prefetch=2,
            grid=(nblk,),
            in_specs=[
                pl.BlockSpec((te * TOK_ROWS, LANES), blk_map),
                pl.BlockSpec((1, D, ed), w_map),
                pl.BlockSpec((1, D, ed), w_map),
                pl.BlockSpec((1, ed, D), w_map),
            ],
            out_specs=pl.BlockSpec((te * TOK_ROWS, LANES), blk_map),
            scratch_shapes=[pltpu.VMEM((D, 2 * ed), MXU_DTYPE), pltpu.VMEM((ed, D), MXU_DTYPE)],
        ),
        out_shape=jax.ShapeDtypeStruct(xs.shape, xs.dtype),
        compiler_params=pltpu.CompilerParams(
            dimension_semantics=("arbitrary",), vmem_limit_bytes=VMEM_LIMIT_BYTES),
        name="experts",
    )(block_e, nused, xs, wg, wu, wd)


def _final_kernel(dest_ref, dest_next_ref, tw_ref, h1_ref, p_ref, ys_hbm, wsgu_ref, wsd_ref, wpg_ref,
                  wpp_ref, g2_ref, b2_ref, g3_ref, b3_ref, o_ref, gbuf_ref, routed_ref, sems, *, alpha):
    i = pl.program_id(0)
    n = pl.num_programs(0)
    tm = o_ref.shape[0]
    sd = wsd_ref.shape[0]
    cur = i % 2
    gbuf = gbuf_ref.at[cur]

    def gather(idx_ref, slot, wait):
        def copy(k, j):
            return _tile_copy(ys_hbm, gbuf_ref.at[slot, k], idx_ref[k, j], j, sems.at[slot])

        def body(j, c):
            for k in range(TOP_K):
                if wait:
                    copy(k, j).wait()
                else:
                    copy(k, j).start(priority=k % 2)
            return c

        lax.fori_loop(0, tm, body, 0)

    @pl.when(i == 0)
    def _():
        gather(dest_ref, 0, wait=False)

    @pl.when(i + 1 < n)
    def _():
        gather(dest_next_ref, 1 - cur, wait=False)

    h1 = h1_ref[...]
    hb = h1.astype(MXU_DTYPE)
    gu = _dot(hb, wsgu_ref[...])
    shared = _dot((_silu(gu[:, 0:sd]) * gu[:, sd:2 * sd]).astype(MXU_DTYPE), wsd_ref[...])

    gather(dest_ref, cur, wait=True)

    tw = tw_ref[...]
    wt = jnp.concatenate([tw, jnp.zeros((LANES - TOP_K, tm), F32)], axis=0).T
    half = routed_ref.shape[1] // 2
    for r in range(tm // SUBLANES):
        wr = wt[r * SUBLANES:(r + 1) * SUBLANES, :]
        wb = [jnp.broadcast_to(wr[:, k:k + 1], (SUBLANES, LANES)) for k in range(TOP_K)]
        for s in range(TOK_ROWS):
            rows = pl.ds(r * SUBLANES * TOK_ROWS + s, SUBLANES, stride=TOK_ROWS)
            acc_lo = acc_hi = None
            for k in range(TOP_K):
                y_lo, y_hi = _unpack_pairs(gbuf[k, rows, :])
                acc_lo = wb[k] * y_lo if k == 0 else acc_lo + wb[k] * y_lo
                acc_hi = wb[k] * y_hi if k == 0 else acc_hi + wb[k] * y_hi
            tok = slice(r * SUBLANES, (r + 1) * SUBLANES)
            routed_ref[tok, s * LANES:(s + 1) * LANES] = acc_lo
            routed_ref[tok, half + s * LANES:half + (s + 1) * LANES] = acc_hi
    routed = routed_ref[...]

    h2 = _layer_norm(alpha * h1 + (routed + shared), g2_ref[...], b2_ref[...])
    gate = jax.nn.sigmoid(_dot(h2.astype(MXU_DTYPE), wpg_ref[...]))
    ple = gate * _dot(p_ref[...].astype(MXU_DTYPE), wpp_ref[...])
    o_ref[...] = _layer_norm(alpha * h2 + ple, g3_ref[...], b3_ref[...])


def _final(dest, topw, h1, p2, ys, wsgu, wsd, wpg, wpp, g2, b2, g3, b3, *, alpha):
    T, D = p2.shape[0], wpg.shape[0]
    tm = TM_FINAL
    n_tiles = T // tm
    row = lambda i: (i, 0)
    full = lambda i: (0, 0)
    col = lambda i: (0, i)
    col_next = lambda i: (0, jnp.minimum(i + 1, n_tiles - 1))
    return pl.pallas_call(
        functools.partial(_final_kernel, alpha=alpha),
        grid=(n_tiles,),
        in_specs=[
            pl.BlockSpec((TOP_K, tm), col, memory_space=pltpu.SMEM),
            pl.BlockSpec((TOP_K, tm), col_next, memory_space=pltpu.SMEM),
            pl.BlockSpec((TOP_K, tm), col),
            pl.BlockSpec((tm, D), row),
            pl.BlockSpec((tm, p2.shape[1]), row),
            pl.BlockSpec(memory_space=pl.ANY),
            pl.BlockSpec(wsgu.shape, full), pl.BlockSpec(wsd.shape, full),
            pl.BlockSpec(wpg.shape, full), pl.BlockSpec(wpp.shape, full),
            pl.BlockSpec(g2.shape, full), pl.BlockSpec(b2.shape, full),
            pl.BlockSpec(g3.shape, full), pl.BlockSpec(b3.shape, full),
        ],
        out_specs=pl.BlockSpec((tm, D), row),
        out_shape=jax.ShapeDtypeStruct((T, D), F32),
        scratch_shapes=[pltpu.VMEM((2, TOP_K, tm * TOK_ROWS, LANES), ys.dtype), pltpu.VMEM((tm, D), F32),
                        pltpu.SemaphoreType.DMA((2,))],
        compiler_params=pltpu.CompilerParams(
            dimension_semantics=("arbitrary",), vmem_limit_bytes=VMEM_LIMIT_BYTES),
        name="final",
    )(dest, dest, topw, h1, p2, ys, wsgu, wsd, wpg, wpp, g2, b2, g3, b3)


def _rope_tables(seq):
    half = ROT_DIM // 2
    inv_freq = ROPE_THETA ** (-jnp.arange(0, ROT_DIM, 2, dtype=F32) / ROT_DIM)
    ang = jnp.arange(seq, dtype=I32).astype(F32)[:, None] * inv_freq[None, :]
    cos, sin = jnp.cos(ang), jnp.sin(ang)
    ones = jnp.ones((seq, HEAD_DIM - ROT_DIM), F32)
    zeros = jnp.zeros((seq, HEAD_DIM - ROT_DIM), F32)
    zh = jnp.zeros((seq, half), F32)
    cos_h = jnp.concatenate([cos, cos, ones], axis=1)
    sa_h = jnp.concatenate([-sin, zh, zeros], axis=1)
    sb_h = jnp.concatenate([zh, sin, zeros], axis=1)
    rep = lambda t: jnp.concatenate([t] * HEADS_PER_LANE_GROUP, axis=1)
    return rep(cos_h), rep(sa_h), rep(sb_h)


def _expert_layout(counts, n_blocks):
    te = TM_EXPERT
    nblk_e = (counts + te - 1) // te
    blk_end = jnp.cumsum(nblk_e)
    blk_start = blk_end - nblk_e
    nused = blk_end[-1]
    pstart = (blk_start * te).astype(I32)
    bid = jnp.arange(n_blocks, dtype=I32)
    block_e = jnp.sum(blk_end[None, :] <= jnp.minimum(bid, nused - 1)[:, None], axis=1).astype(I32)
    block_e = jnp.minimum(block_e, N_EXPERTS - 1)
    partial = (counts % te) != 0
    order = jnp.argsort(jnp.logical_not(partial), stable=True).astype(I32)
    n_partial = jnp.sum(partial).astype(I32)
    last_blk = (blk_end - 1).astype(I32)
    zi = jnp.arange(n_blocks + N_EXPERTS, dtype=I32)
    zblk = jnp.where(zi < n_partial, last_blk[order[jnp.minimum(zi, N_EXPERTS - 1)]],
                     nused + (zi - n_partial))
    nzero = n_partial + (n_blocks - nused)
    zrow = (jnp.clip(zblk, 0, n_blocks - 1) * te).astype(I32)
    return pstart, block_e, nused.astype(I32).reshape(1), zrow, nzero.astype(I32).reshape(1)


def _layer(h, p2, w_in, conv_w, w_attn_out, w_conv_out, w_out, ln1_g, ln1_b, w_router, router_bias,
           w_exp_gate, w_exp_up, w_exp_down, w_sh_gate, w_sh_up, w_sh_down, ln2_g, ln2_b,
           w_ple_gate, w_ple_proj, ln3_g, ln3_b, *, nbatch, seq, alpha):
    T, D = h.shape
    aw = w_attn_out.shape[0]
    cwid = w_conv_out.shape[0]
    bf = lambda w: w.astype(MXU_DTYPE)
    rowv = lambda g: g.reshape(1, -1)

    cos_t, sa_t, sb_t = _rope_tables(seq)
    q, k, v, km, yc, ga, gc = _inproj(h, bf(w_in), cos_t, sa_t, sb_t, conv_w, seq=seq, aw=aw, cwid=cwid)
    attn = _attention(q, k, v, km, nbatch=nbatch, seq=seq, aw=aw)

    wr_t = w_router.astype(F32).T
    wrh = wr_t.astype(MXU_DTYPE)
    wrl = (wr_t - wrh.astype(F32)).astype(MXU_DTYPE)
    h1, h1w, topi, topw, rnk, cnt = _post(
        attn, yc, ga, gc, h, bf(w_attn_out), bf(w_conv_out), bf(w_out), rowv(ln1_g), rowv(ln1_b),
        wrh, wrl, router_bias.astype(F32).reshape(-1, 1), alpha=alpha)

    n_blocks = (T * TOP_K) // TM_EXPERT + N_EXPERTS
    counts = cnt[:, 0].astype(I32)
    pstart, block_e, nused, zrow, nzero = _expert_layout(counts, n_blocks)
    dest = _slots(pstart, topi, rnk)
    xs = _dispatch(zrow, nzero, dest, h1w, n_rows=n_blocks * TM_EXPERT)
    ys = _experts(block_e, nused, xs, w_exp_gate, w_exp_up, w_exp_down)

    wsgu = jnp.concatenate([bf(w_sh_gate), bf(w_sh_up)], axis=1)
    return _final(dest, topw, h1, p2, ys, wsgu, bf(w_sh_down), bf(w_ple_gate), bf(w_ple_proj),
                  rowv(ln2_g), rowv(ln2_b), rowv(ln3_g), rowv(ln3_b), alpha=alpha)


def kernel(x, p, w_in, conv_w, w_attn_out, w_conv_out, w_out, ln1_g, ln1_b, w_router, router_bias,
           w_exp_gate, w_exp_up, w_exp_down, w_sh_gate, w_sh_up, w_sh_down, ln2_g, ln2_b,
           w_ple_gate, w_ple_proj, ln3_g, ln3_b):
    nbatch, seq, d_model = x.shape
    depth = w_in.shape[0]
    alpha = (2 * depth) ** 0.25
    assert seq % MOBA_BLOCK == 0
    h = x.reshape(nbatch * seq, d_model)
    for i in range(depth):
        h = _layer(h, p[i].reshape(nbatch * seq, -1), w_in[i], conv_w[i], w_attn_out[i], w_conv_out[i],
                   w_out[i], ln1_g[i], ln1_b[i], w_router[i], router_bias[i],
                   w_exp_gate[i], w_exp_up[i], w_exp_down[i], w_sh_gate[i], w_sh_up[i], w_sh_down[i],
                   ln2_g[i], ln2_b[i], w_ple_gate[i], w_ple_proj[i], ln3_g[i], ln3_b[i],
                   nbatch=nbatch, seq=seq, alpha=alpha)
    return h.reshape(nbatch, seq, d_model)
```

```python
import functools
import math

import jax
import jax.numpy as jnp
from jax import lax
from jax.experimental import pallas as pl
from jax.experimental.pallas import tpu as pltpu

N_HEADS = 8
HEAD_DIM = 64
ROT_DIM = 16
ROPE_THETA = 500000.0
MOBA_BLOCK = 256
MOBA_TOPK = 3
CONV_K = 3
N_EXPERTS = 256
TOP_K = 8
N_GROUPS = 8
TOPK_GROUPS = 4
ROUTED_SCALE = 2.5
LN_EPS = 1e-5
NEG = -1e30

LANES = 128
SUBLANES = 8
HEADS_PER_LANE_GROUP = LANES // HEAD_DIM
VMEM_LIMIT_BYTES = 56 * 1024 * 1024

MXU_DTYPE = jnp.bfloat16
F32 = jnp.float32
I32 = jnp.int32
U32 = jnp.uint32

TM_INPROJ = 256
TM_POST = 256
TM_DISPATCH = 256
TM_EXPERT = 256
TM_FINAL = 256


def _dot(a, b):
    return jnp.dot(a, b, preferred_element_type=F32)


def _dot_nt(a, b):
    return lax.dot_general(a, b, (((1,), (1,)), ((), ())), preferred_element_type=F32)


def _split_bf16(a):
    hi = a.astype(MXU_DTYPE)
    lo = (a - hi.astype(F32)).astype(MXU_DTYPE)
    return hi, lo


TOK_ROWS = 4
HI_MASK = 0xFFFF0000


def _pack_pairs(f):
    half = f.shape[1] // 2
    bits = lambda t: lax.bitcast_convert_type(t.astype(jnp.bfloat16).astype(F32), U32)
    return (bits(f[:, half:]) & jnp.uint32(HI_MASK)) | lax.shift_right_logical(bits(f[:, :half]), jnp.uint32(16))


def _unpack_pairs(w):
    lo = lax.bitcast_convert_type(lax.shift_left(w, jnp.uint32(16)), F32)
    hi = lax.bitcast_convert_type(w & jnp.uint32(HI_MASK), F32)
    return lo, hi


def _tt_load(ref, n_tok):
    chunks = [ref[pl.ds(s, n_tok, stride=TOK_ROWS), :] for s in range(TOK_ROWS)]
    return jnp.concatenate(chunks, axis=1)


def _tt_store(ref, words):
    n_tok = words.shape[0]
    for s in range(TOK_ROWS):
        ref[pl.ds(s, n_tok, stride=TOK_ROWS), :] = words[:, s * LANES:(s + 1) * LANES]


def _tile_copy(src, dst, src_tok, dst_tok, sem):
    rows = lambda t: pl.ds(pl.multiple_of(t * TOK_ROWS, TOK_ROWS), TOK_ROWS)
    return pltpu.make_async_copy(src.at[rows(src_tok)], dst.at[rows(dst_tok)], sem)


def _layer_norm(t, g, b):
    mu = jnp.mean(t, axis=-1, keepdims=True)
    d = t - mu
    var = jnp.mean(d * d, axis=-1, keepdims=True)
    return d * lax.rsqrt(var + LN_EPS) * g + b


def _silu(t):
    return t * jax.nn.sigmoid(t)


def _inproj_kernel(x_ref, w_ref, cos_ref, sa_ref, sb_ref, cw_ref,
                   q_ref, k_ref, v_ref, km_ref, yc_ref, ga_ref, gc_ref, ubuf,
                   *, tiles_per_seq, aw, cwid, d_model):
    tm = x_ref.shape[0]
    j = pl.program_id(0) % tiles_per_seq
    xb = x_ref[...].astype(MXU_DTYPE)
    cos, sa, sb = cos_ref[...], sa_ref[...], sb_ref[...]

    def proj(c0, n):
        return _dot(xb, w_ref[:, c0:c0 + n])

    def rope(z):
        parts = []
        for g in range(aw // LANES):
            zg = z[:, g * LANES:(g + 1) * LANES]
            parts.append(zg * cos
                         + pltpu.roll(zg, LANES - ROT_DIM // 2, 1) * sa
                         + pltpu.roll(zg, ROT_DIM // 2, 1) * sb)
        return jnp.concatenate(parts, axis=1)

    q_ref[...] = (rope(proj(0, aw)) * (HEAD_DIM ** -0.5)).astype(q_ref.dtype)

    k = rope(proj(aw, aw))
    k_ref[...] = k.astype(k_ref.dtype)
    nb_seq = km_ref.shape[1]
    blocks_per_tile = tm // MOBA_BLOCK

    @pl.when(j == 0)
    def _():
        km_ref[...] = jnp.zeros(km_ref.shape, km_ref.dtype)

    rows = lax.broadcasted_iota(I32, (nb_seq, aw), 0)
    km = km_ref[0]
    for bi in range(blocks_per_tile):
        mean = jnp.sum(k[bi * MOBA_BLOCK:(bi + 1) * MOBA_BLOCK], axis=0, keepdims=True) * (1.0 / MOBA_BLOCK)
        km = jnp.where(rows == j * blocks_per_tile + bi, mean, km)
    km_ref[0] = km

    v_ref[...] = proj(2 * aw, aw).astype(v_ref.dtype)

    c0 = 3 * aw
    cb = proj(c0, cwid)
    u = proj(c0 + cwid, cwid) * proj(c0 + 2 * cwid, cwid)

    @pl.when(j == 0)
    def _():
        ubuf[0:SUBLANES, :] = jnp.zeros((SUBLANES, cwid), F32)

    ubuf[SUBLANES:SUBLANES + tm, :] = u
    um1 = ubuf[SUBLANES - 1:SUBLANES - 1 + tm, :]
    um2 = ubuf[SUBLANES - 2:SUBLANES - 2 + tm, :]
    conv = cw_ref[0:1, :] * um2 + cw_ref[1:2, :] * um1 + cw_ref[2:3, :] * u
    yc_ref[...] = (cb * conv).astype(yc_ref.dtype)
    ubuf[0:SUBLANES, :] = u[tm - SUBLANES:tm, :]

    g0 = c0 + 3 * cwid
    ga_ref[...] = jax.nn.sigmoid(proj(g0, d_model))
    gc_ref[...] = jax.nn.sigmoid(proj(g0 + d_model, d_model))


def _inproj(x2, w_in_b, cos_t, sa_t, sb_t, conv_w, *, seq, aw, cwid):
    T, D = x2.shape
    tm = TM_INPROJ
    assert seq % tm == 0 and tm % MOBA_BLOCK == 0 and T % seq == 0
    tiles_per_seq = seq // tm
    nb_seq = seq // MOBA_BLOCK
    nbatch = T // seq
    row = lambda i: (i, 0)
    tab = lambda i: (i % tiles_per_seq, 0)
    kern = functools.partial(_inproj_kernel, tiles_per_seq=tiles_per_seq, aw=aw, cwid=cwid, d_model=D)
    return pl.pallas_call(
        kern,
        grid=(T // tm,),
        in_specs=[
            pl.BlockSpec((tm, D), row),
            pl.BlockSpec(w_in_b.shape, lambda i: (0, 0)),
            pl.BlockSpec((tm, LANES), tab),
            pl.BlockSpec((tm, LANES), tab),
            pl.BlockSpec((tm, LANES), tab),
            pl.BlockSpec(conv_w.shape, lambda i: (0, 0)),
        ],
        out_specs=[
            pl.BlockSpec((tm, aw), row),
            pl.BlockSpec((tm, aw), row),
            pl.BlockSpec((tm, aw), row),
            pl.BlockSpec((1, nb_seq, aw), lambda i: (i // tiles_per_seq, 0, 0)),
            pl.BlockSpec((tm, cwid), row),
            pl.BlockSpec((tm, D), row),
            pl.BlockSpec((tm, D), row),
        ],
        out_shape=[
            jax.ShapeDtypeStruct((T, aw), MXU_DTYPE),
            jax.ShapeDtypeStruct((T, aw), MXU_DTYPE),
            jax.ShapeDtypeStruct((T, aw), MXU_DTYPE),
            jax.ShapeDtypeStruct((nbatch, nb_seq, aw), F32),
            jax.ShapeDtypeStruct((T, cwid), MXU_DTYPE),
            jax.ShapeDtypeStruct((T, D), F32),
            jax.ShapeDtypeStruct((T, D), F32),
        ],
        scratch_shapes=[pltpu.VMEM((tm + SUBLANES, cwid), F32)],
        compiler_params=pltpu.CompilerParams(
            dimension_semantics=("arbitrary",), vmem_limit_bytes=VMEM_LIMIT_BYTES),
        name="inproj",
    )(x2, w_in_b, cos_t, sa_t, sb_t, conv_w)


def _attn_kernel(q_ref, k_ref, v_ref, km_ref, o_ref, vt_ref, sel_ref, s_ref, m_ref, l_ref, acc_ref):
    qb = pl.program_id(1)
    blk = MOBA_BLOCK
    nb = km_ref.shape[1]
    n_groups = q_ref.shape[2] // LANES
    hpg = HEADS_PER_LANE_GROUP
    lanes = lambda g: slice(g * LANES, (g + 1) * LANES)

    @pl.when(qb == 0)
    def _():
        for b in range(nb):
            for g in range(n_groups):
                vt_ref[b, lanes(g), :] = v_ref[0, b * blk:(b + 1) * blk, lanes(g)].astype(F32).T.astype(vt_ref.dtype)

    blk_id = lax.broadcasted_iota(I32, (nb, blk), 0).astype(F32)
    past = blk_id < qb.astype(F32)
    dim = lax.broadcasted_iota(I32, (LANES, blk), 0)

    qts = []
    for g in range(n_groups):
        q2t = q_ref[0, :, lanes(g)].astype(F32).T
        km_hi, km_lo = _split_bf16(km_ref[0, :, lanes(g)])
        for h in range(hpg):
            in_head = (dim >= h * HEAD_DIM) & (dim < (h + 1) * HEAD_DIM)
            qt = jnp.where(in_head, q2t, 0.0).astype(MXU_DTYPE)
            qts.append(qt)
            gate = jnp.where(past, _dot(km_hi, qt) + _dot(km_lo, qt), -jnp.inf)
            sel = jnp.zeros((nb, blk), F32)
            for _ in range(MOBA_TOPK):
                mx = jnp.max(gate, axis=0, keepdims=True)
                idx = jnp.min(jnp.where(gate == mx, blk_id, float(nb)), axis=0, keepdims=True)
                pick = blk_id == idx
                sel = jnp.where(pick, jnp.where(past, 1.0, sel), sel)
                gate = jnp.where(pick, -jnp.inf, gate)
            sel_ref[g * hpg + h] = sel

    def v_t(jb, hd):
        return vt_ref[jb, hd * HEAD_DIM:(hd + 1) * HEAD_DIM, :]

    def scores_to_scratch(g, kg, nk):
        for h in range(hpg):
            s_ref[g % 2, h, 0:nk * blk, :] = _dot(kg, qts[g * hpg + h])

    def run_groups(key_rows, nk, update):
        scores_to_scratch(0, k_ref[0, key_rows, lanes(0)], nk)
        for g in range(n_groups):
            if g + 1 < n_groups:
                scores_to_scratch(g + 1, k_ref[0, key_rows, lanes(g + 1)], nk)
            for h in range(hpg):
                update(g * hpg + h, s_ref[g % 2, h, 0:nk * blk, :])

    key_pos = lax.broadcasted_iota(I32, (blk, blk), 0)
    qry_pos = lax.broadcasted_iota(I32, (blk, blk), 1)

    def init(hd, s_raw):
        s = jnp.where(key_pos <= qry_pos, s_raw, NEG)
        m = jnp.max(s, axis=0, keepdims=True)
        p = jnp.exp(s - m)
        m_ref[hd] = m
        l_ref[hd] = jnp.sum(p, axis=0, keepdims=True)
        acc_ref[hd] = _dot(v_t(qb, hd), p.astype(MXU_DTYPE))

    run_groups(pl.ds(pl.multiple_of(qb * blk, blk), blk), 1, init)

    def make_step(nk):
        def step(b0):
            def update(hd, s_raw):
                parts = [jnp.where(sel_ref[hd, pl.ds(b0 + t, 1), :] > 0.5,
                                   s_raw[t * blk:(t + 1) * blk], NEG) for t in range(nk)]
                s = jnp.concatenate(parts, axis=0) if nk > 1 else parts[0]
                m = m_ref[hd]
                m_new = jnp.maximum(m, jnp.max(s, axis=0, keepdims=True))
                alpha = jnp.exp(m - m_new)
                p = jnp.exp(s - m_new)
                vt = [v_t(b0 + t, hd) for t in range(nk)]
                vt = jnp.concatenate(vt, axis=1) if nk > 1 else vt[0]
                m_ref[hd] = m_new
                l_ref[hd] = alpha * l_ref[hd] + jnp.sum(p, axis=0, keepdims=True)
                acc_ref[hd] = alpha * acc_ref[hd] + _dot(vt, p.astype(MXU_DTYPE))

            run_groups(pl.ds(pl.multiple_of(b0 * blk, blk), nk * blk), nk, update)
        return step

    pair, single = make_step(2), make_step(1)
    n_pairs = lax.shift_right_logical(qb, 1)
    lax.fori_loop(0, n_pairs, lambda i, c: (pair(2 * i), c)[1], 0)
    lax.fori_loop(0, qb & 1, lambda i, c: (single(qb - 1), c)[1], 0)

    for g in range(n_groups):
        outs = [acc_ref[g * hpg + h] / l_ref[g * hpg + h] for h in range(hpg)]
        o_ref[0, :, lanes(g)] = jnp.concatenate(outs, axis=0).T.astype(o_ref.dtype)


def _attention(q, k, v, km, *, nbatch, seq, aw):
    blk = MOBA_BLOCK
    nb = seq // blk
    n_heads = aw // HEAD_DIM
    max_nk = 2
    assert nb % SUBLANES == 0
    q3, k3, v3 = (t.reshape(nbatch, seq, aw) for t in (q, k, v))
    out = pl.pallas_call(
        _attn_kernel,
        grid=(nbatch, nb),
        in_specs=[
            pl.BlockSpec((1, blk, aw), lambda b, i: (b, i, 0)),
            pl.BlockSpec((1, seq, aw), lambda b, i: (b, 0, 0)),
            pl.BlockSpec((1, seq, aw), lambda b, i: (b, 0, 0)),
            pl.BlockSpec((1, nb, aw), lambda b, i: (b, 0, 0)),
        ],
        out_specs=pl.BlockSpec((1, blk, aw), lambda b, i: (b, i, 0)),
        out_shape=jax.ShapeDtypeStruct((nbatch, seq, aw), MXU_DTYPE),
        scratch_shapes=[
            pltpu.VMEM((nb, aw, blk), MXU_DTYPE),
            pltpu.VMEM((n_heads, nb, blk), F32),
            pltpu.VMEM((2, HEADS_PER_LANE_GROUP, max_nk * blk, blk), F32),
            pltpu.VMEM((n_heads, 1, blk), F32),
            pltpu.VMEM((n_heads, 1, blk), F32),
            pltpu.VMEM((n_heads, HEAD_DIM, blk), F32),
        ],
        compiler_params=pltpu.CompilerParams(
            dimension_semantics=("arbitrary", "arbitrary"), vmem_limit_bytes=VMEM_LIMIT_BYTES),
        name="moba_attn",
    )(q3, k3, v3, km)
    return out.reshape(nbatch * seq, aw)


def _post_kernel(attn_ref, yc_ref, ga_ref, gc_ref, x_ref, wao_ref, wco_ref, wo_ref,
                 g1_ref, b1_ref, wrh_ref, wrl_ref, rb_ref,
                 h1_ref, h1w_ref, ti_ref, tw_ref, rk_ref, cnt_ref, cnt_acc, *, alpha):
    i = pl.program_id(0)
    tm = x_ref.shape[0]
    ne = wrh_ref.shape[0]

    y_attn = _dot(attn_ref[...], wao_ref[...])
    y_conv = _dot(yc_ref[...], wco_ref[...])
    merged = ga_ref[...] * y_attn + gc_ref[...] * y_conv
    mix = _dot(merged.astype(MXU_DTYPE), wo_ref[...])
    h1 = _layer_norm(alpha * x_ref[...] + mix, g1_ref[...], b1_ref[...])
    h1_ref[...] = h1
    _tt_store(h1w_ref, _pack_pairs(h1))

    h_hi, h_lo = _split_bf16(h1)
    logits = _dot_nt(wrh_ref[...], h_hi) + _dot_nt(wrh_ref[...], h_lo) + _dot_nt(wrl_ref[...], h_hi)
    scores = jax.nn.sigmoid(logits)
    choice = scores + rb_ref[...]

    gsz = ne // N_GROUPS
    gshape = (N_GROUPS, gsz, tm)
    c3 = choice.reshape(gshape)
    in_grp = lax.broadcasted_iota(I32, gshape, 1).astype(F32)
    m1 = jnp.max(c3, axis=1, keepdims=True)
    i1 = jnp.min(jnp.where(c3 == m1, in_grp, float(gsz)), axis=1, keepdims=True)
    m2 = jnp.max(jnp.where(in_grp == i1, -jnp.inf, c3), axis=1, keepdims=True)
    gscore = jnp.broadcast_to(m1 + m2, gshape).reshape(ne, tm)

    eid = lax.broadcasted_iota(I32, (ne, tm), 0).astype(F32)
    gid = lax.broadcasted_iota(I32, gshape, 0).astype(F32).reshape(ne, tm)
    cand = jnp.full((ne, tm), -jnp.inf, F32)
    for _ in range(TOPK_GROUPS):
        mx = jnp.max(gscore, axis=0, keepdims=True)
        idx = jnp.min(jnp.where(gscore == mx, gid, float(N_GROUPS)), axis=0, keepdims=True)
        pick = gid == idx
        cand = jnp.where(pick, choice, cand)
        gscore = jnp.where(pick, -jnp.inf, gscore)

    selmat = jnp.zeros((ne, tm), F32)
    idxs, svals = [], []
    for _ in range(TOP_K):
        mx = jnp.max(cand, axis=0, keepdims=True)
        idx = jnp.min(jnp.where(cand == mx, eid, float(ne)), axis=0, keepdims=True)
        pick = eid == idx
        svals.append(jnp.sum(jnp.where(pick, scores, 0.0), axis=0, keepdims=True))
        idxs.append(idx)
        selmat = jnp.where(pick, 1.0, selmat)
        cand = jnp.where(pick, -jnp.inf, cand)
    ssum = svals[0]
    for r in range(1, TOP_K):
        ssum = ssum + svals[r]

    @pl.when(i == 0)
    def _():
        cnt_acc[...] = jnp.zeros(cnt_acc.shape, F32)

    tr = lax.broadcasted_iota(I32, (tm, tm), 0)
    tc = lax.broadcasted_iota(I32, (tm, tm), 1)
    upper = jnp.where(tr < tc, 1.0, 0.0).astype(MXU_DTYPE)
    selb = selmat.astype(MXU_DTYPE)
    base = cnt_acc[...]
    rank = _dot(selb, upper) + jnp.concatenate([base] * (tm // LANES), axis=1)
    new_cnt = base + _dot(selb, jnp.ones((tm, LANES), MXU_DTYPE))
    cnt_acc[...] = new_cnt
    cnt_ref[...] = new_cnt

    for r in range(TOP_K):
        pick = eid == idxs[r]
        ti_ref[r:r + 1, :] = idxs[r].astype(I32)
        tw_ref[r:r + 1, :] = svals[r] / ssum * ROUTED_SCALE
        rk_ref[r:r + 1, :] = jnp.sum(jnp.where(pick, rank, 0.0), axis=0, keepdims=True).astype(I32)


def _post(attn, yc, ga, gc, x2, wao, wco, wo, g1, b1, wrh, wrl, rb, *, alpha):
    T, D = x2.shape
    tm = TM_POST
    assert D == 2 * TOK_ROWS * LANES
    aw, cwid = attn.shape[1], yc.shape[1]
    ne = wrh.shape[0]
    row = lambda i: (i, 0)
    full = lambda i: (0, 0)
    col = lambda i: (0, i)
    return pl.pallas_call(
        functools.partial(_post_kernel, alpha=alpha),
        grid=(T // tm,),
        in_specs=[
            pl.BlockSpec((tm, aw), row), pl.BlockSpec((tm, cwid), row),
            pl.BlockSpec((tm, D), row), pl.BlockSpec((tm, D), row), pl.BlockSpec((tm, D), row),
            pl.BlockSpec(wao.shape, full), pl.BlockSpec(wco.shape, full), pl.BlockSpec(wo.shape, full),
            pl.BlockSpec(g1.shape, full), pl.BlockSpec(b1.shape, full),
            pl.BlockSpec(wrh.shape, full), pl.BlockSpec(wrl.shape, full), pl.BlockSpec(rb.shape, full),
        ],
        out_specs=[
            pl.BlockSpec((tm, D), row),
            pl.BlockSpec((tm * TOK_ROWS, LANES), row),
            pl.BlockSpec((TOP_K, tm), col), pl.BlockSpec((TOP_K, tm), col), pl.BlockSpec((TOP_K, tm), col),
            pl.BlockSpec((ne, LANES), full),
        ],
        out_shape=[
            jax.ShapeDtypeStruct((T, D), F32),
            jax.ShapeDtypeStruct((T * TOK_ROWS, LANES), U32),
            jax.ShapeDtypeStruct((TOP_K, T), I32),
            jax.ShapeDtypeStruct((TOP_K, T), F32),
            jax.ShapeDtypeStruct((TOP_K, T), I32),
            jax.ShapeDtypeStruct((ne, LANES), F32),
        ],
        scratch_shapes=[pltpu.VMEM((ne, LANES), F32)],
        compiler_params=pltpu.CompilerParams(
            dimension_semantics=("arbitrary",), vmem_limit_bytes=VMEM_LIMIT_BYTES),
        name="post_route",
    )(attn, yc, ga, gc, x2, wao, wco, wo, g1, b1, wrh, wrl, rb)


def _slots_kernel(pstart_ref, ti_ref, rk_ref, dest_ref):
    ti = ti_ref[...]
    start = lax.fori_loop(0, N_EXPERTS, lambda e, acc: jnp.where(ti == e, pstart_ref[e], acc),
                          jnp.zeros(ti.shape, I32))
    dest_ref[...] = start + rk_ref[...]


def _slots(pstart, topi, rnk):
    T = topi.shape[1]
    tc = min(T, 2048)
    blk = pl.BlockSpec((TOP_K, tc), lambda i, *_: (0, i))
    return pl.pallas_call(
        _slots_kernel,
        grid_spec=pltpu.PrefetchScalarGridSpec(
            num_scalar_prefetch=1, grid=(T // tc,), in_specs=[blk, blk], out_specs=blk),
        out_shape=jax.ShapeDtypeStruct((TOP_K, T), I32),
        compiler_params=pltpu.CompilerParams(dimension_semantics=("arbitrary",)),
        name="slots",
    )(pstart, topi, rnk)


def _dispatch_kernel(zrow_ref, nzero_ref, dest_ref, h1_ref, xs_hbm, zbuf, zsem, sem):
    i = pl.program_id(0)
    tm = dest_ref.shape[1]
    zrows = zbuf.shape[0]

    def zero_copy(z):
        start = pl.multiple_of(zrow_ref[z] * TOK_ROWS, zrows)
        return pltpu.make_async_copy(zbuf, xs_hbm.at[pl.ds(start, zrows)], zsem)

    @pl.when(i == 0)
    def _():
        zbuf[...] = jnp.zeros(zbuf.shape, zbuf.dtype)
        nz = nzero_ref[0]
        lax.fori_loop(0, nz, lambda z, c: (zero_copy(z).start(), c)[1], 0)
        lax.fori_loop(0, nz, lambda z, c: (zero_copy(z).wait(), c)[1], 0)

    def copy(k, j):
        return _tile_copy(h1_ref, xs_hbm, j, dest_ref[k, j], sem)

    def issue(j, c):
        for k in range(TOP_K):
            copy(k, j).start(priority=k % 2)
        return c

    def drain(j, c):
        for k in range(TOP_K):
            copy(k, j).wait()
        return c

    lax.fori_loop(0, tm, issue, 0)
    lax.fori_loop(0, tm, drain, 0)


def _dispatch(zrow, nzero, dest, h1, *, n_rows):
    T = h1.shape[0] // TOK_ROWS
    tm = TM_DISPATCH
    return pl.pallas_call(
        _dispatch_kernel,
        grid_spec=pltpu.PrefetchScalarGridSpec(
            num_scalar_prefetch=2,
            grid=(T // tm,),
            in_specs=[pl.BlockSpec((TOP_K, tm), lambda i, *_: (0, i), memory_space=pltpu.SMEM),
                      pl.BlockSpec((tm * TOK_ROWS, LANES), lambda i, *_: (i, 0))],
            out_specs=pl.BlockSpec(memory_space=pl.ANY),
            scratch_shapes=[pltpu.VMEM((TM_EXPERT * TOK_ROWS, LANES), h1.dtype),
                            pltpu.SemaphoreType.DMA, pltpu.SemaphoreType.DMA],
        ),
        out_shape=jax.ShapeDtypeStruct((n_rows * TOK_ROWS, LANES), h1.dtype),
        compiler_params=pltpu.CompilerParams(
            dimension_semantics=("arbitrary",), vmem_limit_bytes=VMEM_LIMIT_BYTES),
        name="dispatch",
    )(zrow, nzero, dest, h1)


def _experts_kernel(be_ref, nused_ref, xs_ref, wg_ref, wu_ref, wd_ref, ys_ref, wgu_b, wd_b):
    i = pl.program_id(0)
    ed = wg_ref.shape[2]

    @pl.when(i < nused_ref[0])
    def _():
        prev = be_ref[jnp.maximum(i - 1, 0)]

        @pl.when((i == 0) | (be_ref[i] != prev))
        def _():
            wgu_b[:, 0:ed] = wg_ref[0].astype(MXU_DTYPE)
            wgu_b[:, ed:2 * ed] = wu_ref[0].astype(MXU_DTYPE)
            wd_b[...] = wd_ref[0].astype(MXU_DTYPE)

        te = xs_ref.shape[0] // TOK_ROWS
        x_lo, x_hi = _unpack_pairs(_tt_load(xs_ref, te))
        xb = jnp.concatenate([x_lo.astype(MXU_DTYPE), x_hi.astype(MXU_DTYPE)], axis=1)
        gu = _dot(xb, wgu_b[...])
        hb = _silu(gu[:, 0:ed]) * gu[:, ed:2 * ed]
        _tt_store(ys_ref, _pack_pairs(_dot(hb.astype(MXU_DTYPE), wd_b[...])))


def _experts(block_e, nused, xs, wg, wu, wd):
    te = TM_EXPERT
    D, ed = wg.shape[1], wg.shape[2]
    nblk = xs.shape[0] // (te * TOK_ROWS)
    blk_map = lambda i, be, nu: (jnp.minimum(i, nu[0] - 1), 0)
    w_map = lambda i, be, nu: (be[i], 0, 0)
    return pl.pallas_call(
        _experts_kernel,
        grid_spec=pltpu.PrefetchScalarGridSpec(
            num_scalar_prefetch=2,
            grid=(nblk,),
            in_specs=[
                pl.BlockSpec((te * TOK_ROWS, LANES), blk_map),
                pl.BlockSpec((1, D, ed), w_map),
                pl.BlockSpec((1, D, ed), w_map),
                pl.BlockSpec((1, ed, D), w_map),
            ],
            out_specs=pl.BlockSpec((te * TOK_ROWS, LANES), blk_map),
            scratch_shapes=[pltpu.VMEM((D, 2 * ed), MXU_DTYPE), pltpu.VMEM((ed, D), MXU_DTYPE)],
        ),
        out_shape=jax.ShapeDtypeStruct(xs.shape, xs.dtype),
        compiler_params=pltpu.CompilerParams(
            dimension_semantics=("arbitrary",), vmem_limit_bytes=VMEM_LIMIT_BYTES),
        name="experts",
    )(block_e, nused, xs, wg, wu, wd)


def _final_kernel(dest_ref, dest_next_ref, tw_ref, h1_ref, p_ref, ys_hbm, wsgu_ref, wsd_ref, wpg_ref,
                  wpp_ref, g2_ref, b2_ref, g3_ref, b3_ref, o_ref, gbuf_a, gbuf_b, routed_ref, sems, *, alpha):
    i = pl.program_id(0)
    n = pl.num_programs(0)
    tm = o_ref.shape[0]
    sd = wsd_ref.shape[0]
    half = routed_ref.shape[1] // 2
    bufs = (gbuf_a, gbuf_b)

    def copy(idx_ref, slot, k, j):
        return _tile_copy(ys_hbm, bufs[slot].at[k], idx_ref[k, j], j, sems.at[slot])

    def gather_loop(idx_ref, slot, wait):
        def body(j, c):
            for k in range(TOP_K):
                if wait:
                    copy(idx_ref, slot, k, j).wait()
                else:
                    copy(idx_ref, slot, k, j).start(priority=k % 2)
            return c
        lax.fori_loop(0, tm, body, 0)

    @pl.when(i == 0)
    def _():
        gather_loop(dest_ref, 0, wait=False)

    def tile(cur):
        nxt = 1 - cur
        gbuf = bufs[cur]
        h1 = h1_ref[...]
        gu = _dot(h1.astype(MXU_DTYPE), wsgu_ref[...])
        shared = _dot((_silu(gu[:, 0:sd]) * gu[:, sd:2 * sd]).astype(MXU_DTYPE), wsd_ref[...])

        gather_loop(dest_ref, cur, wait=True)

        tw = tw_ref[...]
        wt = jnp.concatenate([tw, jnp.zeros((LANES - TOP_K, tm), F32)], axis=0).T
        for r in range(tm // SUBLANES):
            for j in range(r * SUBLANES, (r + 1) * SUBLANES):
                for k in range(TOP_K):
                    copy(dest_next_ref, nxt, k, j).start(priority=k % 2)
            wr = wt[r * SUBLANES:(r + 1) * SUBLANES, :]
            wb = [jnp.broadcast_to(wr[:, k:k + 1], (SUBLANES, LANES)) for k in range(TOP_K)]
            for s in range(TOK_ROWS):
                rows = pl.ds(r * SUBLANES * TOK_ROWS + s, SUBLANES, stride=TOK_ROWS)
                acc_lo = acc_hi = None
                for k in range(TOP_K):
                    y_lo, y_hi = _unpack_pairs(gbuf[k, rows, :])
                    acc_lo = wb[k] * y_lo if k == 0 else acc_lo + wb[k] * y_lo
                    acc_hi = wb[k] * y_hi if k == 0 else acc_hi + wb[k] * y_hi
                tok = slice(r * SUBLANES, (r + 1) * SUBLANES)
                routed_ref[tok, s * LANES:(s + 1) * LANES] = acc_lo
                routed_ref[tok, half + s * LANES:half + (s + 1) * LANES] = acc_hi
        routed = routed_ref[...]

        h2 = _layer_norm(alpha * h1 + (routed + shared), g2_ref[...], b2_ref[...])
        gate = jax.nn.sigmoid(_dot(h2.astype(MXU_DTYPE), wpg_ref[...]))
        ple = gate * _dot(p_ref[...].astype(MXU_DTYPE), wpp_ref[...])
        o_ref[...] = _layer_norm(alpha * h2 + ple, g3_ref[...], b3_ref[...])

        @pl.when(i == n - 1)
        def _():
            gather_loop(dest_next_ref, nxt, wait=True)

    for parity in range(2):
        pl.when(i % 2 == parity)(functools.partial(tile, parity))


def _final(dest, topw, h1, p2, ys, wsgu, wsd, wpg, wpp, g2, b2, g3, b3, *, alpha):
    T, D = p2.shape[0], wpg.shape[0]
    tm = TM_FINAL
    n_tiles = T // tm
    row = lambda i: (i, 0)
    full = lambda i: (0, 0)
    col = lambda i: (0, i)
    col_next = lambda i: (0, jnp.minimum(i + 1, n_tiles - 1))
    return pl.pallas_call(
        functools.partial(_final_kernel, alpha=alpha),
        grid=(n_tiles,),
        in_specs=[
            pl.BlockSpec((TOP_K, tm), col, memory_space=pltpu.SMEM),
            pl.BlockSpec((TOP_K, tm), col_next, memory_space=pltpu.SMEM),
            pl.BlockSpec((TOP_K, tm), col),
            pl.BlockSpec((tm, D), row),
            pl.BlockSpec((tm, p2.shape[1]), row),
            pl.BlockSpec(memory_space=pl.ANY),
            pl.BlockSpec(wsgu.shape, full), pl.BlockSpec(wsd.shape, full),
            pl.BlockSpec(wpg.shape, full), pl.BlockSpec(wpp.shape, full),
            pl.BlockSpec(g2.shape, full), pl.BlockSpec(b2.shape, full),
            pl.BlockSpec(g3.shape, full), pl.BlockSpec(b3.shape, full),
        ],
        out_specs=pl.BlockSpec((tm, D), row),
        out_shape=jax.ShapeDtypeStruct((T, D), F32),
        scratch_shapes=[pltpu.VMEM((TOP_K, tm * TOK_ROWS, LANES), ys.dtype),
                        pltpu.VMEM((TOP_K, tm * TOK_ROWS, LANES), ys.dtype),
                        pltpu.VMEM((tm, D), F32), pltpu.SemaphoreType.DMA((2,))],
        compiler_params=pltpu.CompilerParams(
            dimension_semantics=("arbitrary",), vmem_limit_bytes=VMEM_LIMIT_BYTES),
        name="final",
    )(dest, dest, topw, h1, p2, ys, wsgu, wsd, wpg, wpp, g2, b2, g3, b3)


def _rope_tables(seq):
    half = ROT_DIM // 2
    inv_freq = ROPE_THETA ** (-jnp.arange(0, ROT_DIM, 2, dtype=F32) / ROT_DIM)
    ang = jnp.arange(seq, dtype=I32).astype(F32)[:, None] * inv_freq[None, :]
    cos, sin = jnp.cos(ang), jnp.sin(ang)
    ones = jnp.ones((seq, HEAD_DIM - ROT_DIM), F32)
    zeros = jnp.zeros((seq, HEAD_DIM - ROT_DIM), F32)
    zh = jnp.zeros((seq, half), F32)
    cos_h = jnp.concatenate([cos, cos, ones], axis=1)
    sa_h = jnp.concatenate([-sin, zh, zeros], axis=1)
    sb_h = jnp.concatenate([zh, sin, zeros], axis=1)
    rep = lambda t: jnp.concatenate([t] * HEADS_PER_LANE_GROUP, axis=1)
    return rep(cos_h), rep(sa_h), rep(sb_h)


def _expert_layout(counts, n_blocks):
    te = TM_EXPERT
    nblk_e = (counts + te - 1) // te
    blk_end = jnp.cumsum(nblk_e)
    blk_start = blk_end - nblk_e
    nused = blk_end[-1]
    pstart = (blk_start * te).astype(I32)
    bid = jnp.arange(n_blocks, dtype=I32)
    block_e = jnp.sum(blk_end[None, :] <= jnp.minimum(bid, nused - 1)[:, None], axis=1).astype(I32)
    block_e = jnp.minimum(block_e, N_EXPERTS - 1)
    partial = (counts % te) != 0
    order = jnp.argsort(jnp.logical_not(partial), stable=True).astype(I32)
    n_partial = jnp.sum(partial).astype(I32)
    last_blk = (blk_end - 1).astype(I32)
    zi = jnp.arange(n_blocks + N_EXPERTS, dtype=I32)
    zblk = jnp.where(zi < n_partial, last_blk[order[jnp.minimum(zi, N_EXPERTS - 1)]],
                     nused + (zi - n_partial))
    nzero = n_partial + (n_blocks - nused)
    zrow = (jnp.clip(zblk, 0, n_blocks - 1) * te).astype(I32)
    return pstart, block_e, nused.astype(I32).reshape(1), zrow, nzero.astype(I32).reshape(1)


def _layer(h, p2, w_in, conv_w, w_attn_out, w_conv_out, w_out, ln1_g, ln1_b, w_router, router_bias,
           w_exp_gate, w_exp_up, w_exp_down, w_sh_gate, w_sh_up, w_sh_down, ln2_g, ln2_b,
           w_ple_gate, w_ple_proj, ln3_g, ln3_b, *, nbatch, seq, alpha):
    T, D = h.shape
    aw = w_attn_out.shape[0]
    cwid = w_conv_out.shape[0]
    bf = lambda w: w.astype(MXU_DTYPE)
    rowv = lambda g: g.reshape(1, -1)

    cos_t, sa_t, sb_t = _rope_tables(seq)
    q, k, v, km, yc, ga, gc = _inproj(h, bf(w_in), cos_t, sa_t, sb_t, conv_w, seq=seq, aw=aw, cwid=cwid)
    attn = _attention(q, k, v, km, nbatch=nbatch, seq=seq, aw=aw)

    wr_t = w_router.astype(F32).T
    wrh = wr_t.astype(MXU_DTYPE)
    wrl = (wr_t - wrh.astype(F32)).astype(MXU_DTYPE)
    h1, h1w, topi, topw, rnk, cnt = _post(
        attn, yc, ga, gc, h, bf(w_attn_out), bf(w_conv_out), bf(w_out), rowv(ln1_g), rowv(ln1_b),
        wrh, wrl, router_bias.astype(F32).reshape(-1, 1), alpha=alpha)

    n_blocks = (T * TOP_K) // TM_EXPERT + N_EXPERTS
    counts = cnt[:, 0].astype(I32)
    pstart, block_e, nused, zrow, nzero = _expert_layout(counts, n_blocks)
    dest = _slots(pstart, topi, rnk)
    xs = _dispatch(zrow, nzero, dest, h1w, n_rows=n_blocks * TM_EXPERT)
    ys = _experts(block_e, nused, xs, w_exp_gate, w_exp_up, w_exp_down)

    wsgu = jnp.concatenate([bf(w_sh_gate), bf(w_sh_up)], axis=1)
    return _final(dest, topw, h1, p2, ys, wsgu, bf(w_sh_down), bf(w_ple_gate), bf(w_ple_proj),
                  rowv(ln2_g), rowv(ln2_b), rowv(ln3_g), rowv(ln3_b), alpha=alpha)


def kernel(x, p, w_in, conv_w, w_attn_out, w_conv_out, w_out, ln1_g, ln1_b, w_router, router_bias,
           w_exp_gate, w_exp_up, w_exp_down, w_sh_gate, w_sh_up, w_sh_down, ln2_g, ln2_b,
           w_ple_gate, w_ple_proj, ln3_g, ln3_b):
    nbatch, seq, d_model = x.shape
    depth = w_in.shape[0]
    alpha = (2 * depth) ** 0.25
    assert seq % MOBA_BLOCK == 0
    h = x.reshape(nbatch * seq, d_model)
    for i in range(depth):
        h = _layer(h, p[i].reshape(nbatch * seq, -1), w_in[i], conv_w[i], w_attn_out[i], w_conv_out[i],
                   w_out[i], ln1_g[i], ln1_b[i], w_router[i], router_bias[i],
                   w_exp_gate[i], w_exp_up[i], w_exp_down[i], w_sh_gate[i], w_sh_up[i], w_sh_down[i],
                   ln2_g[i], ln2_b[i], w_ple_gate[i], w_ple_proj[i], ln3_g[i], ln3_b[i],
                   nbatch=nbatch, seq=seq, alpha=alpha)
    return h.reshape(nbatch, seq, d_model)
```

```python
import functools
import math

import jax
import jax.numpy as jnp
from jax import lax
from jax.experimental import pallas as pl
from jax.experimental.pallas import tpu as pltpu

N_HEADS = 8
HEAD_DIM = 64
ROT_DIM = 16
ROPE_THETA = 500000.0
MOBA_BLOCK = 256
MOBA_TOPK = 3
CONV_K = 3
N_EXPERTS = 256
TOP_K = 8
N_GROUPS = 8
TOPK_GROUPS = 4
ROUTED_SCALE = 2.5
LN_EPS = 1e-5
NEG = -1e30

LANES = 128
SUBLANES = 8
HEADS_PER_LANE_GROUP = LANES // HEAD_DIM
VMEM_LIMIT_BYTES = 56 * 1024 * 1024

MXU_DTYPE = jnp.bfloat16
F32 = jnp.float32
I32 = jnp.int32
U32 = jnp.uint32

TM_INPROJ = 256
TM_POST = 256
TM_DISPATCH = 256
TM_EXPERT = 512
TM_FINAL = 256


def _dot(a, b):
    return jnp.dot(a, b, preferred_element_type=F32)


def _dot_nt(a, b):
    return lax.dot_general(a, b, (((1,), (1,)), ((), ())), preferred_element_type=F32)


def _split_bf16(a):
    hi = a.astype(MXU_DTYPE)
    lo = (a - hi.astype(F32)).astype(MXU_DTYPE)
    return hi, lo


TOK_ROWS = 4
HI_MASK = 0xFFFF0000


def _pack_pairs(f):
    half = f.shape[1] // 2
    bits = lambda t: lax.bitcast_convert_type(t.astype(jnp.bfloat16).astype(F32), U32)
    return (bits(f[:, half:]) & jnp.uint32(HI_MASK)) | lax.shift_right_logical(bits(f[:, :half]), jnp.uint32(16))


def _unpack_pairs(w):
    lo = lax.bitcast_convert_type(lax.shift_left(w, jnp.uint32(16)), F32)
    hi = lax.bitcast_convert_type(w & jnp.uint32(HI_MASK), F32)
    return lo, hi


def _tt_load(ref, n_tok):
    chunks = [ref[pl.ds(s, n_tok, stride=TOK_ROWS), :] for s in range(TOK_ROWS)]
    return jnp.concatenate(chunks, axis=1)


def _tt_store(ref, words):
    n_tok = words.shape[0]
    for s in range(TOK_ROWS):
        ref[pl.ds(s, n_tok, stride=TOK_ROWS), :] = words[:, s * LANES:(s + 1) * LANES]


def _tile_copy(src, dst, src_tok, dst_tok, sem):
    rows = lambda t: pl.ds(pl.multiple_of(t * TOK_ROWS, TOK_ROWS), TOK_ROWS)
    return pltpu.make_async_copy(src.at[rows(src_tok)], dst.at[rows(dst_tok)], sem)


def _layer_norm(t, g, b):
    mu = jnp.mean(t, axis=-1, keepdims=True)
    d = t - mu
    var = jnp.mean(d * d, axis=-1, keepdims=True)
    return d * lax.rsqrt(var + LN_EPS) * g + b


def _silu(t):
    return t * jax.nn.sigmoid(t)


def _inproj_kernel(x_ref, w_ref, cos_ref, sa_ref, sb_ref, cw_ref,
                   q_ref, k_ref, v_ref, km_ref, yc_ref, ga_ref, gc_ref, ubuf,
                   *, tiles_per_seq, aw, cwid, d_model):
    tm = x_ref.shape[0]
    j = pl.program_id(0) % tiles_per_seq
    xb = x_ref[...].astype(MXU_DTYPE)
    cos, sa, sb = cos_ref[...], sa_ref[...], sb_ref[...]

    def proj(c0, n):
        return _dot(xb, w_ref[:, c0:c0 + n])

    def rope(z):
        parts = []
        for g in range(aw // LANES):
            zg = z[:, g * LANES:(g + 1) * LANES]
            parts.append(zg * cos
                         + pltpu.roll(zg, LANES - ROT_DIM // 2, 1) * sa
                         + pltpu.roll(zg, ROT_DIM // 2, 1) * sb)
        return jnp.concatenate(parts, axis=1)

    q_ref[...] = (rope(proj(0, aw)) * (HEAD_DIM ** -0.5)).astype(q_ref.dtype)

    k = rope(proj(aw, aw))
    k_ref[...] = k.astype(k_ref.dtype)
    nb_seq = km_ref.shape[1]
    blocks_per_tile = tm // MOBA_BLOCK

    @pl.when(j == 0)
    def _():
        km_ref[...] = jnp.zeros(km_ref.shape, km_ref.dtype)

    rows = lax.broadcasted_iota(I32, (nb_seq, aw), 0)
    km = km_ref[0]
    for bi in range(blocks_per_tile):
        mean = jnp.sum(k[bi * MOBA_BLOCK:(bi + 1) * MOBA_BLOCK], axis=0, keepdims=True) * (1.0 / MOBA_BLOCK)
        km = jnp.where(rows == j * blocks_per_tile + bi, mean, km)
    km_ref[0] = km

    v_ref[...] = proj(2 * aw, aw).astype(v_ref.dtype)

    c0 = 3 * aw
    cb = proj(c0, cwid)
    u = proj(c0 + cwid, cwid) * proj(c0 + 2 * cwid, cwid)

    @pl.when(j == 0)
    def _():
        ubuf[0:SUBLANES, :] = jnp.zeros((SUBLANES, cwid), F32)

    ubuf[SUBLANES:SUBLANES + tm, :] = u
    um1 = ubuf[SUBLANES - 1:SUBLANES - 1 + tm, :]
    um2 = ubuf[SUBLANES - 2:SUBLANES - 2 + tm, :]
    conv = cw_ref[0:1, :] * um2 + cw_ref[1:2, :] * um1 + cw_ref[2:3, :] * u
    yc_ref[...] = (cb * conv).astype(yc_ref.dtype)
    ubuf[0:SUBLANES, :] = u[tm - SUBLANES:tm, :]

    g0 = c0 + 3 * cwid
    ga_ref[...] = jax.nn.sigmoid(proj(g0, d_model))
    gc_ref[...] = jax.nn.sigmoid(proj(g0 + d_model, d_model))


def _inproj(x2, w_in_b, cos_t, sa_t, sb_t, conv_w, *, seq, aw, cwid):
    T, D = x2.shape
    tm = TM_INPROJ
    assert seq % tm == 0 and tm % MOBA_BLOCK == 0 and T % seq == 0
    tiles_per_seq = seq // tm
    nb_seq = seq // MOBA_BLOCK
    nbatch = T // seq
    row = lambda i: (i, 0)
    tab = lambda i: (i % tiles_per_seq, 0)
    kern = functools.partial(_inproj_kernel, tiles_per_seq=tiles_per_seq, aw=aw, cwid=cwid, d_model=D)
    return pl.pallas_call(
        kern,
        grid=(T // tm,),
        in_specs=[
            pl.BlockSpec((tm, D), row),
            pl.BlockSpec(w_in_b.shape, lambda i: (0, 0)),
            pl.BlockSpec((tm, LANES), tab),
            pl.BlockSpec((tm, LANES), tab),
            pl.BlockSpec((tm, LANES), tab),
            pl.BlockSpec(conv_w.shape, lambda i: (0, 0)),
        ],
        out_specs=[
            pl.BlockSpec((tm, aw), row),
            pl.BlockSpec((tm, aw), row),
            pl.BlockSpec((tm, aw), row),
            pl.BlockSpec((1, nb_seq, aw), lambda i: (i // tiles_per_seq, 0, 0)),
            pl.BlockSpec((tm, cwid), row),
            pl.BlockSpec((tm, D), row),
            pl.BlockSpec((tm, D), row),
        ],
        out_shape=[
            jax.ShapeDtypeStruct((T, aw), MXU_DTYPE),
            jax.ShapeDtypeStruct((T, aw), MXU_DTYPE),
            jax.ShapeDtypeStruct((T, aw), MXU_DTYPE),
            jax.ShapeDtypeStruct((nbatch, nb_seq, aw), F32),
            jax.ShapeDtypeStruct((T, cwid), MXU_DTYPE),
            jax.ShapeDtypeStruct((T, D), F32),
            jax.ShapeDtypeStruct((T, D), F32),
        ],
        scratch_shapes=[pltpu.VMEM((tm + SUBLANES, cwid), F32)],
        compiler_params=pltpu.CompilerParams(
            dimension_semantics=("arbitrary",), vmem_limit_bytes=VMEM_LIMIT_BYTES),
        name="inproj",
    )(x2, w_in_b, cos_t, sa_t, sb_t, conv_w)


def _attn_kernel(q_ref, k_ref, v_ref, km_ref, o_ref, vt_ref, sel_ref, s_ref, m_ref, l_ref, acc_ref):
    qb = pl.program_id(1)
    blk = MOBA_BLOCK
    nb = km_ref.shape[1]
    n_groups = q_ref.shape[2] // LANES
    hpg = HEADS_PER_LANE_GROUP
    lanes = lambda g: slice(g * LANES, (g + 1) * LANES)

    @pl.when(qb == 0)
    def _():
        for b in range(nb):
            for g in range(n_groups):
                vt_ref[b, lanes(g), :] = v_ref[0, b * blk:(b + 1) * blk, lanes(g)].astype(F32).T.astype(vt_ref.dtype)

    blk_id = lax.broadcasted_iota(I32, (nb, blk), 0).astype(F32)
    past = blk_id < qb.astype(F32)
    dim = lax.broadcasted_iota(I32, (LANES, blk), 0)

    qts = []
    for g in range(n_groups):
        q2t = q_ref[0, :, lanes(g)].astype(F32).T
        km_hi, km_lo = _split_bf16(km_ref[0, :, lanes(g)])
        for h in range(hpg):
            in_head = (dim >= h * HEAD_DIM) & (dim < (h + 1) * HEAD_DIM)
            qt = jnp.where(in_head, q2t, 0.0).astype(MXU_DTYPE)
            qts.append(qt)
            gate = jnp.where(past, _dot(km_hi, qt) + _dot(km_lo, qt), -jnp.inf)
            sel = jnp.zeros((nb, blk), F32)
            for _ in range(MOBA_TOPK):
                mx = jnp.max(gate, axis=0, keepdims=True)
                idx = jnp.min(jnp.where(gate == mx, blk_id, float(nb)), axis=0, keepdims=True)
                pick = blk_id == idx
                sel = jnp.where(pick, jnp.where(past, 1.0, sel), sel)
                gate = jnp.where(pick, -jnp.inf, gate)
            sel_ref[g * hpg + h] = sel

    def v_t(jb, hd):
        return vt_ref[jb, hd * HEAD_DIM:(hd + 1) * HEAD_DIM, :]

    def scores_to_scratch(g, kg, nk):
        for h in range(hpg):
            s_ref[g % 2, h, 0:nk * blk, :] = _dot(kg, qts[g * hpg + h])

    def run_groups(key_rows, nk, update):
        scores_to_scratch(0, k_ref[0, key_rows, lanes(0)], nk)
        for g in range(n_groups):
            if g + 1 < n_groups:
                scores_to_scratch(g + 1, k_ref[0, key_rows, lanes(g + 1)], nk)
            for h in range(hpg):
                update(g * hpg + h, s_ref[g % 2, h, 0:nk * blk, :])

    key_pos = lax.broadcasted_iota(I32, (blk, blk), 0)
    qry_pos = lax.broadcasted_iota(I32, (blk, blk), 1)

    def init(hd, s_raw):
        s = jnp.where(key_pos <= qry_pos, s_raw, NEG)
        m = jnp.max(s, axis=0, keepdims=True)
        p = jnp.exp(s - m)
        m_ref[hd] = m
        l_ref[hd] = jnp.sum(p, axis=0, keepdims=True)
        acc_ref[hd] = _dot(v_t(qb, hd), p.astype(MXU_DTYPE))

    run_groups(pl.ds(pl.multiple_of(qb * blk, blk), blk), 1, init)

    def make_step(nk):
        def step(b0):
            def update(hd, s_raw):
                parts = [jnp.where(sel_ref[hd, pl.ds(b0 + t, 1), :] > 0.5,
                                   s_raw[t * blk:(t + 1) * blk], NEG) for t in range(nk)]
                s = jnp.concatenate(parts, axis=0) if nk > 1 else parts[0]
                m = m_ref[hd]
                m_new = jnp.maximum(m, jnp.max(s, axis=0, keepdims=True))
                alpha = jnp.exp(m - m_new)
                p = jnp.exp(s - m_new)
                vt = [v_t(b0 + t, hd) for t in range(nk)]
                vt = jnp.concatenate(vt, axis=1) if nk > 1 else vt[0]
                m_ref[hd] = m_new
                l_ref[hd] = alpha * l_ref[hd] + jnp.sum(p, axis=0, keepdims=True)
                acc_ref[hd] = alpha * acc_ref[hd] + _dot(vt, p.astype(MXU_DTYPE))

            run_groups(pl.ds(pl.multiple_of(b0 * blk, blk), nk * blk), nk, update)
        return step

    pair, single = make_step(2), make_step(1)
    n_pairs = lax.shift_right_logical(qb, 1)
    lax.fori_loop(0, n_pairs, lambda i, c: (pair(2 * i), c)[1], 0)
    lax.fori_loop(0, qb & 1, lambda i, c: (single(qb - 1), c)[1], 0)

    for g in range(n_groups):
        outs = [acc_ref[g * hpg + h] / l_ref[g * hpg + h] for h in range(hpg)]
        o_ref[0, :, lanes(g)] = jnp.concatenate(outs, axis=0).T.astype(o_ref.dtype)


def _attention(q, k, v, km, *, nbatch, seq, aw):
    blk = MOBA_BLOCK
    nb = seq // blk
    n_heads = aw // HEAD_DIM
    max_nk = 2
    assert nb % SUBLANES == 0
    q3, k3, v3 = (t.reshape(nbatch, seq, aw) for t in (q, k, v))
    out = pl.pallas_call(
        _attn_kernel,
        grid=(nbatch, nb),
        in_specs=[
            pl.BlockSpec((1, blk, aw), lambda b, i: (b, i, 0)),
            pl.BlockSpec((1, seq, aw), lambda b, i: (b, 0, 0)),
            pl.BlockSpec((1, seq, aw), lambda b, i: (b, 0, 0)),
            pl.BlockSpec((1, nb, aw), lambda b, i: (b, 0, 0)),
        ],
        out_specs=pl.BlockSpec((1, blk, aw), lambda b, i: (b, i, 0)),
        out_shape=jax.ShapeDtypeStruct((nbatch, seq, aw), MXU_DTYPE),
        scratch_shapes=[
            pltpu.VMEM((nb, aw, blk), MXU_DTYPE),
            pltpu.VMEM((n_heads, nb, blk), F32),
            pltpu.VMEM((2, HEADS_PER_LANE_GROUP, max_nk * blk, blk), F32),
            pltpu.VMEM((n_heads, 1, blk), F32),
            pltpu.VMEM((n_heads, 1, blk), F32),
            pltpu.VMEM((n_heads, HEAD_DIM, blk), F32),
        ],
        compiler_params=pltpu.CompilerParams(
            dimension_semantics=("arbitrary", "arbitrary"), vmem_limit_bytes=VMEM_LIMIT_BYTES),
        name="moba_attn",
    )(q3, k3, v3, km)
    return out.reshape(nbatch * seq, aw)


def _post_kernel(attn_ref, yc_ref, ga_ref, gc_ref, x_ref, wao_ref, wco_ref, wo_ref,
                 g1_ref, b1_ref, wrh_ref, wrl_ref, rb_ref,
                 h1_ref, h1w_ref, ti_ref, tw_ref, rk_ref, cnt_ref, cnt_acc, *, alpha):
    i = pl.program_id(0)
    tm = x_ref.shape[0]
    ne = wrh_ref.shape[0]

    y_attn = _dot(attn_ref[...], wao_ref[...])
    y_conv = _dot(yc_ref[...], wco_ref[...])
    merged = ga_ref[...] * y_attn + gc_ref[...] * y_conv
    mix = _dot(merged.astype(MXU_DTYPE), wo_ref[...])
    h1 = _layer_norm(alpha * x_ref[...] + mix, g1_ref[...], b1_ref[...])
    h1_ref[...] = h1
    _tt_store(h1w_ref, _pack_pairs(h1))

    h_hi, h_lo = _split_bf16(h1)
    logits = _dot_nt(wrh_ref[...], h_hi) + _dot_nt(wrh_ref[...], h_lo) + _dot_nt(wrl_ref[...], h_hi)
    scores = jax.nn.sigmoid(logits)
    choice = scores + rb_ref[...]

    gsz = ne // N_GROUPS
    gshape = (N_GROUPS, gsz, tm)
    c3 = choice.reshape(gshape)
    in_grp = lax.broadcasted_iota(I32, gshape, 1).astype(F32)
    m1 = jnp.max(c3, axis=1, keepdims=True)
    i1 = jnp.min(jnp.where(c3 == m1, in_grp, float(gsz)), axis=1, keepdims=True)
    m2 = jnp.max(jnp.where(in_grp == i1, -jnp.inf, c3), axis=1, keepdims=True)
    gscore = jnp.broadcast_to(m1 + m2, gshape).reshape(ne, tm)

    eid = lax.broadcasted_iota(I32, (ne, tm), 0).astype(F32)
    gid = lax.broadcasted_iota(I32, gshape, 0).astype(F32).reshape(ne, tm)
    cand = jnp.full((ne, tm), -jnp.inf, F32)
    for _ in range(TOPK_GROUPS):
        mx = jnp.max(gscore, axis=0, keepdims=True)
        idx = jnp.min(jnp.where(gscore == mx, gid, float(N_GROUPS)), axis=0, keepdims=True)
        pick = gid == idx
        cand = jnp.where(pick, choice, cand)
        gscore = jnp.where(pick, -jnp.inf, gscore)

    selmat = jnp.zeros((ne, tm), F32)
    idxs, svals = [], []
    for _ in range(TOP_K):
        mx = jnp.max(cand, axis=0, keepdims=True)
        idx = jnp.min(jnp.where(cand == mx, eid, float(ne)), axis=0, keepdims=True)
        pick = eid == idx
        svals.append(jnp.sum(jnp.where(pick, scores, 0.0), axis=0, keepdims=True))
        idxs.append(idx)
        selmat = jnp.where(pick, 1.0, selmat)
        cand = jnp.where(pick, -jnp.inf, cand)
    ssum = svals[0]
    for r in range(1, TOP_K):
        ssum = ssum + svals[r]

    @pl.when(i == 0)
    def _():
        cnt_acc[...] = jnp.zeros(cnt_acc.shape, F32)

    tr = lax.broadcasted_iota(I32, (tm, tm), 0)
    tc = lax.broadcasted_iota(I32, (tm, tm), 1)
    upper = jnp.where(tr < tc, 1.0, 0.0).astype(MXU_DTYPE)
    selb = selmat.astype(MXU_DTYPE)
    base = cnt_acc[...]
    rank = _dot(selb, upper) + jnp.concatenate([base] * (tm // LANES), axis=1)
    new_cnt = base + _dot(selb, jnp.ones((tm, LANES), MXU_DTYPE))
    cnt_acc[...] = new_cnt
    cnt_ref[...] = new_cnt

    for r in range(TOP_K):
        pick = eid == idxs[r]
        ti_ref[r:r + 1, :] = idxs[r].astype(I32)
        tw_ref[r:r + 1, :] = svals[r] / ssum * ROUTED_SCALE
        rk_ref[r:r + 1, :] = jnp.sum(jnp.where(pick, rank, 0.0), axis=0, keepdims=True).astype(I32)


def _post(attn, yc, ga, gc, x2, wao, wco, wo, g1, b1, wrh, wrl, rb, *, alpha):
    T, D = x2.shape
    tm = TM_POST
    assert D == 2 * TOK_ROWS * LANES
    aw, cwid = attn.shape[1], yc.shape[1]
    ne = wrh.shape[0]
    row = lambda i: (i, 0)
    full = lambda i: (0, 0)
    col = lambda i: (0, i)
    return pl.pallas_call(
        functools.partial(_post_kernel, alpha=alpha),
        grid=(T // tm,),
        in_specs=[
            pl.BlockSpec((tm, aw), row), pl.BlockSpec((tm, cwid), row),
            pl.BlockSpec((tm, D), row), pl.BlockSpec((tm, D), row), pl.BlockSpec((tm, D), row),
            pl.BlockSpec(wao.shape, full), pl.BlockSpec(wco.shape, full), pl.BlockSpec(wo.shape, full),
            pl.BlockSpec(g1.shape, full), pl.BlockSpec(b1.shape, full),
            pl.BlockSpec(wrh.shape, full), pl.BlockSpec(wrl.shape, full), pl.BlockSpec(rb.shape, full),
        ],
        out_specs=[
            pl.BlockSpec((tm, D), row),
            pl.BlockSpec((tm * TOK_ROWS, LANES), row),
            pl.BlockSpec((TOP_K, tm), col), pl.BlockSpec((TOP_K, tm), col), pl.BlockSpec((TOP_K, tm), col),
            pl.BlockSpec((ne, LANES), full),
        ],
        out_shape=[
            jax.ShapeDtypeStruct((T, D), F32),
            jax.ShapeDtypeStruct((T * TOK_ROWS, LANES), U32),
            jax.ShapeDtypeStruct((TOP_K, T), I32),
            jax.ShapeDtypeStruct((TOP_K, T), F32),
            jax.ShapeDtypeStruct((TOP_K, T), I32),
            jax.ShapeDtypeStruct((ne, LANES), F32),
        ],
        scratch_shapes=[pltpu.VMEM((ne, LANES), F32)],
        compiler_params=pltpu.CompilerParams(
            dimension_semantics=("arbitrary",), vmem_limit_bytes=VMEM_LIMIT_BYTES),
        name="post_route",
    )(attn, yc, ga, gc, x2, wao, wco, wo, g1, b1, wrh, wrl, rb)


def _slots_kernel(pstart_ref, ti_ref, rk_ref, dest_ref):
    ti = ti_ref[...]
    start = lax.fori_loop(0, N_EXPERTS, lambda e, acc: jnp.where(ti == e, pstart_ref[e], acc),
                          jnp.zeros(ti.shape, I32))
    dest_ref[...] = start + rk_ref[...]


def _slots(pstart, topi, rnk):
    T = topi.shape[1]
    tc = min(T, 2048)
    blk = pl.BlockSpec((TOP_K, tc), lambda i, *_: (0, i))
    return pl.pallas_call(
        _slots_kernel,
        grid_spec=pltpu.PrefetchScalarGridSpec(
            num_scalar_prefetch=1, grid=(T // tc,), in_specs=[blk, blk], out_specs=blk),
        out_shape=jax.ShapeDtypeStruct((TOP_K, T), I32),
        compiler_params=pltpu.CompilerParams(dimension_semantics=("arbitrary",)),
        name="slots",
    )(pstart, topi, rnk)


def _dispatch_kernel(zrow_ref, nzero_ref, dest_ref, h1_ref, xs_hbm, zbuf, zsem, sem):
    i = pl.program_id(0)
    tm = dest_ref.shape[1]
    zrows = zbuf.shape[0]

    def zero_copy(z):
        start = pl.multiple_of(zrow_ref[z] * TOK_ROWS, zrows)
        return pltpu.make_async_copy(zbuf, xs_hbm.at[pl.ds(start, zrows)], zsem)

    @pl.when(i == 0)
    def _():
        zbuf[...] = jnp.zeros(zbuf.shape, zbuf.dtype)
        nz = nzero_ref[0]
        lax.fori_loop(0, nz, lambda z, c: (zero_copy(z).start(), c)[1], 0)
        lax.fori_loop(0, nz, lambda z, c: (zero_copy(z).wait(), c)[1], 0)

    def copy(k, j):
        return _tile_copy(h1_ref, xs_hbm, j, dest_ref[k, j], sem)

    def issue(j, c):
        for k in range(TOP_K):
            copy(k, j).start(priority=k % 2)
        return c

    def drain(j, c):
        for k in range(TOP_K):
            copy(k, j).wait()
        return c

    lax.fori_loop(0, tm, issue, 0)
    lax.fori_loop(0, tm, drain, 0)


def _dispatch(zrow, nzero, dest, h1, *, n_rows):
    T = h1.shape[0] // TOK_ROWS
    tm = TM_DISPATCH
    return pl.pallas_call(
        _dispatch_kernel,
        grid_spec=pltpu.PrefetchScalarGridSpec(
            num_scalar_prefetch=2,
            grid=(T // tm,),
            in_specs=[pl.BlockSpec((TOP_K, tm), lambda i, *_: (0, i), memory_space=pltpu.SMEM),
                      pl.BlockSpec((tm * TOK_ROWS, LANES), lambda i, *_: (i, 0))],
            out_specs=pl.BlockSpec(memory_space=pl.ANY),
            scratch_shapes=[pltpu.VMEM((TM_EXPERT * TOK_ROWS, LANES), h1.dtype),
                            pltpu.SemaphoreType.DMA, pltpu.SemaphoreType.DMA],
        ),
        out_shape=jax.ShapeDtypeStruct((n_rows * TOK_ROWS, LANES), h1.dtype),
        compiler_params=pltpu.CompilerParams(
            dimension_semantics=("arbitrary",), vmem_limit_bytes=VMEM_LIMIT_BYTES),
        name="dispatch",
    )(zrow, nzero, dest, h1)


def _experts_kernel(be_ref, nused_ref, bvalid_ref, xs_ref, wg_ref, wu_ref, wd_ref, ys_ref, wgu_b, wd_b):
    i = pl.program_id(0)
    ed = wg_ref.shape[2]
    te = xs_ref.shape[0] // TOK_ROWS

    def swiglu_rows(n_tok):
        x_lo, x_hi = _unpack_pairs(_tt_load(xs_ref, n_tok))
        xb = jnp.concatenate([x_lo.astype(MXU_DTYPE), x_hi.astype(MXU_DTYPE)], axis=1)
        gu = _dot(xb, wgu_b[...])
        hb = _silu(gu[:, 0:ed]) * gu[:, ed:2 * ed]
        _tt_store(ys_ref, _pack_pairs(_dot(hb.astype(MXU_DTYPE), wd_b[...])))

    @pl.when(i < nused_ref[0])
    def _():
        prev = be_ref[jnp.maximum(i - 1, 0)]

        @pl.when((i == 0) | (be_ref[i] != prev))
        def _():
            wgu_b[:, 0:ed] = wg_ref[0].astype(MXU_DTYPE)
            wgu_b[:, ed:2 * ed] = wu_ref[0].astype(MXU_DTYPE)
            wd_b[...] = wd_ref[0].astype(MXU_DTYPE)

        nvalid = bvalid_ref[i]

        @pl.when(nvalid > te // 2)
        def _():
            swiglu_rows(te)

        @pl.when(nvalid <= te // 2)
        def _():
            swiglu_rows(te // 2)
            rest = ys_ref.shape[0] // 2
            ys_ref[rest:, :] = jnp.zeros((rest, ys_ref.shape[1]), ys_ref.dtype)


def _experts(block_e, nused, bvalid, xs, wg, wu, wd):
    te = TM_EXPERT
    D, ed = wg.shape[1], wg.shape[2]
    nblk = xs.shape[0] // (te * TOK_ROWS)
    blk_map = lambda i, be, nu, bv: (jnp.minimum(i, nu[0] - 1), 0)
    w_map = lambda i, be, nu, bv: (be[i], 0, 0)
    return pl.pallas_call(
        _experts_kernel,
        grid_spec=pltpu.PrefetchScalarGridSpec(
            num_scalar_prefetch=3,
            grid=(nblk,),
            in_specs=[
                pl.BlockSpec((te * TOK_ROWS, LANES), blk_map),
                pl.BlockSpec((1, D, ed), w_map),
                pl.BlockSpec((1, D, ed), w_map),
                pl.BlockSpec((1, ed, D), w_map),
            ],
            out_specs=pl.BlockSpec((te * TOK_ROWS, LANES), blk_map),
            scratch_shapes=[pltpu.VMEM((D, 2 * ed), MXU_DTYPE), pltpu.VMEM((ed, D), MXU_DTYPE)],
        ),
        out_shape=jax.ShapeDtypeStruct(xs.shape, xs.dtype),
        compiler_params=pltpu.CompilerParams(
            dimension_semantics=("arbitrary",), vmem_limit_bytes=VMEM_LIMIT_BYTES),
        name="experts",
    )(block_e, nused, bvalid, xs, wg, wu, wd)


def _final_kernel(dest_ref, dest_next_ref, tw_ref, h1_ref, p_ref, ys_hbm, wsgu_ref, wsd_ref, wpg_ref,
                  wpp_ref, g2_ref, b2_ref, g3_ref, b3_ref, o_ref, gbuf_a, gbuf_b, routed_ref, sems, *, alpha):
    i = pl.program_id(0)
    n = pl.num_programs(0)
    tm = o_ref.shape[0]
    sd = wsd_ref.shape[0]
    half = routed_ref.shape[1] // 2
    bufs = (gbuf_a, gbuf_b)

    def copy(idx_ref, slot, k, j):
        return _tile_copy(ys_hbm, bufs[slot].at[k], idx_ref[k, j], j, sems.at[slot])

    def gather_loop(idx_ref, slot, wait):
        def body(j, c):
            for k in range(TOP_K):
                if wait:
                    copy(idx_ref, slot, k, j).wait()
                else:
                    copy(idx_ref, slot, k, j).start(priority=k % 2)
            return c
        lax.fori_loop(0, tm, body, 0)

    @pl.when(i == 0)
    def _():
        gather_loop(dest_ref, 0, wait=False)

    def tile(cur):
        nxt = 1 - cur
        gbuf = bufs[cur]
        h1 = h1_ref[...]
        gu = _dot(h1.astype(MXU_DTYPE), wsgu_ref[...])
        shared = _dot((_silu(gu[:, 0:sd]) * gu[:, sd:2 * sd]).astype(MXU_DTYPE), wsd_ref[...])

        gather_loop(dest_ref, cur, wait=True)

        tw = tw_ref[...]
        wt = jnp.concatenate([tw, jnp.zeros((LANES - TOP_K, tm), F32)], axis=0).T
        for r in range(tm // SUBLANES):
            for j in range(r * SUBLANES, (r + 1) * SUBLANES):
                for k in range(TOP_K):
                    copy(dest_next_ref, nxt, k, j).start(priority=k % 2)
            wr = wt[r * SUBLANES:(r + 1) * SUBLANES, :]
            wb = [jnp.broadcast_to(wr[:, k:k + 1], (SUBLANES, LANES)) for k in range(TOP_K)]
            for s in range(TOK_ROWS):
                rows = pl.ds(r * SUBLANES * TOK_ROWS + s, SUBLANES, stride=TOK_ROWS)
                acc_lo = acc_hi = None
                for k in range(TOP_K):
                    y_lo, y_hi = _unpack_pairs(gbuf[k, rows, :])
                    acc_lo = wb[k] * y_lo if k == 0 else acc_lo + wb[k] * y_lo
                    acc_hi = wb[k] * y_hi if k == 0 else acc_hi + wb[k] * y_hi
                tok = slice(r * SUBLANES, (r + 1) * SUBLANES)
                routed_ref[tok, s * LANES:(s + 1) * LANES] = acc_lo
                routed_ref[tok, half + s * LANES:half + (s + 1) * LANES] = acc_hi
        routed = routed_ref[...]

        h2 = _layer_norm(alpha * h1 + (routed + shared), g2_ref[...], b2_ref[...])
        gate = jax.nn.sigmoid(_dot(h2.astype(MXU_DTYPE), wpg_ref[...]))
        ple = gate * _dot(p_ref[...].astype(MXU_DTYPE), wpp_ref[...])
        o_ref[...] = _layer_norm(alpha * h2 + ple, g3_ref[...], b3_ref[...])

        @pl.when(i == n - 1)
        def _():
            gather_loop(dest_next_ref, nxt, wait=True)

    for parity in range(2):
        pl.when(i % 2 == parity)(functools.partial(tile, parity))


def _final(dest, topw, h1, p2, ys, wsgu, wsd, wpg, wpp, g2, b2, g3, b3, *, alpha):
    T, D = p2.shape[0], wpg.shape[0]
    tm = TM_FINAL
    n_tiles = T // tm
    row = lambda i: (i, 0)
    full = lambda i: (0, 0)
    col = lambda i: (0, i)
    col_next = lambda i: (0, jnp.minimum(i + 1, n_tiles - 1))
    return pl.pallas_call(
        functools.partial(_final_kernel, alpha=alpha),
        grid=(n_tiles,),
        in_specs=[
            pl.BlockSpec((TOP_K, tm), col, memory_space=pltpu.SMEM),
            pl.BlockSpec((TOP_K, tm), col_next, memory_space=pltpu.SMEM),
            pl.BlockSpec((TOP_K, tm), col),
            pl.BlockSpec((tm, D), row),
            pl.BlockSpec((tm, p2.shape[1]), row),
            pl.BlockSpec(memory_space=pl.ANY),
            pl.BlockSpec(wsgu.shape, full), pl.BlockSpec(wsd.shape, full),
            pl.BlockSpec(wpg.shape, full), pl.BlockSpec(wpp.shape, full),
            pl.BlockSpec(g2.shape, full), pl.BlockSpec(b2.shape, full),
            pl.BlockSpec(g3.shape, full), pl.BlockSpec(b3.shape, full),
        ],
        out_specs=pl.BlockSpec((tm, D), row),
        out_shape=jax.ShapeDtypeStruct((T, D), F32),
        scratch_shapes=[pltpu.VMEM((TOP_K, tm * TOK_ROWS, LANES), ys.dtype),
                        pltpu.VMEM((TOP_K, tm * TOK_ROWS, LANES), ys.dtype),
                        pltpu.VMEM((tm, D), F32), pltpu.SemaphoreType.DMA((2,))],
        compiler_params=pltpu.CompilerParams(
            dimension_semantics=("arbitrary",), vmem_limit_bytes=VMEM_LIMIT_BYTES),
        name="final",
    )(dest, dest, topw, h1, p2, ys, wsgu, wsd, wpg, wpp, g2, b2, g3, b3)


def _rope_tables(seq):
    half = ROT_DIM // 2
    inv_freq = ROPE_THETA ** (-jnp.arange(0, ROT_DIM, 2, dtype=F32) / ROT_DIM)
    ang = jnp.arange(seq, dtype=I32).astype(F32)[:, None] * inv_freq[None, :]
    cos, sin = jnp.cos(ang), jnp.sin(ang)
    ones = jnp.ones((seq, HEAD_DIM - ROT_DIM), F32)
    zeros = jnp.zeros((seq, HEAD_DIM - ROT_DIM), F32)
    zh = jnp.zeros((seq, half), F32)
    cos_h = jnp.concatenate([cos, cos, ones], axis=1)
    sa_h = jnp.concatenate([-sin, zh, zeros], axis=1)
    sb_h = jnp.concatenate([zh, sin, zeros], axis=1)
    rep = lambda t: jnp.concatenate([t] * HEADS_PER_LANE_GROUP, axis=1)
    return rep(cos_h), rep(sa_h), rep(sb_h)


def _expert_layout(counts, n_blocks):
    te = TM_EXPERT
    nblk_e = (counts + te - 1) // te
    blk_end = jnp.cumsum(nblk_e)
    blk_start = blk_end - nblk_e
    nused = blk_end[-1]
    pstart = (blk_start * te).astype(I32)
    bid = jnp.arange(n_blocks, dtype=I32)
    block_e = jnp.sum(blk_end[None, :] <= jnp.minimum(bid, nused - 1)[:, None], axis=1).astype(I32)
    block_e = jnp.minimum(block_e, N_EXPERTS - 1)
    bvalid = jnp.clip(counts[block_e] - (bid - blk_start[block_e]) * te, 0, te).astype(I32)
    partial = (counts % te) != 0
    order = jnp.argsort(jnp.logical_not(partial), stable=True).astype(I32)
    n_partial = jnp.sum(partial).astype(I32)
    last_blk = (blk_end - 1).astype(I32)
    zi = jnp.arange(n_blocks + N_EXPERTS, dtype=I32)
    zblk = jnp.where(zi < n_partial, last_blk[order[jnp.minimum(zi, N_EXPERTS - 1)]],
                     nused + (zi - n_partial))
    nzero = n_partial + (n_blocks - nused)
    zrow = (jnp.clip(zblk, 0, n_blocks - 1) * te).astype(I32)
    return pstart, block_e, bvalid, nused.astype(I32).reshape(1), zrow, nzero.astype(I32).reshape(1)


def _layer(h, p2, w_in, conv_w, w_attn_out, w_conv_out, w_out, ln1_g, ln1_b, w_router, router_bias,
           w_exp_gate, w_exp_up, w_exp_down, w_sh_gate, w_sh_up, w_sh_down, ln2_g, ln2_b,
           w_ple_gate, w_ple_proj, ln3_g, ln3_b, *, nbatch, seq, alpha):
    T, D = h.shape
    aw = w_attn_out.shape[0]
    cwid = w_conv_out.shape[0]
    bf = lambda w: w.astype(MXU_DTYPE)
    rowv = lambda g: g.reshape(1, -1)

    cos_t, sa_t, sb_t = _rope_tables(seq)
    q, k, v, km, yc, ga, gc = _inproj(h, bf(w_in), cos_t, sa_t, sb_t, conv_w, seq=seq, aw=aw, cwid=cwid)
    attn = _attention(q, k, v, km, nbatch=nbatch, seq=seq, aw=aw)

    wr_t = w_router.astype(F32).T
    wrh = wr_t.astype(MXU_DTYPE)
    wrl = (wr_t - wrh.astype(F32)).astype(MXU_DTYPE)
    h1, h1w, topi, topw, rnk, cnt = _post(
        attn, yc, ga, gc, h, bf(w_attn_out), bf(w_conv_out), bf(w_out), rowv(ln1_g), rowv(ln1_b),
        wrh, wrl, router_bias.astype(F32).reshape(-1, 1), alpha=alpha)

    n_blocks = (T * TOP_K) // TM_EXPERT + N_EXPERTS
    counts = cnt[:, 0].astype(I32)
    pstart, block_e, bvalid, nused, zrow, nzero = _expert_layout(counts, n_blocks)
    dest = _slots(pstart, topi, rnk)
    xs = _dispatch(zrow, nzero, dest, h1w, n_rows=n_blocks * TM_EXPERT)
    ys = _experts(block_e, nused, bvalid, xs, w_exp_gate, w_exp_up, w_exp_down)

    wsgu = jnp.concatenate([bf(w_sh_gate), bf(w_sh_up)], axis=1)
    return _final(dest, topw, h1, p2, ys, wsgu, bf(w_sh_down), bf(w_ple_gate), bf(w_ple_proj),
                  rowv(ln2_g), rowv(ln2_b), rowv(ln3_g), rowv(ln3_b), alpha=alpha)


def kernel(x, p, w_in, conv_w, w_attn_out, w_conv_out, w_out, ln1_g, ln1_b, w_router, router_bias,
           w_exp_gate, w_exp_up, w_exp_down, w_sh_gate, w_sh_up, w_sh_down, ln2_g, ln2_b,
           w_ple_gate, w_ple_proj, ln3_g, ln3_b):
    nbatch, seq, d_model = x.shape
    depth = w_in.shape[0]
    alpha = (2 * depth) ** 0.25
    assert seq % MOBA_BLOCK == 0
    h = x.reshape(nbatch * seq, d_model)
    for i in range(depth):
        h = _layer(h, p[i].reshape(nbatch * seq, -1), w_in[i], conv_w[i], w_attn_out[i], w_conv_out[i],
                   w_out[i], ln1_g[i], ln1_b[i], w_router[i], router_bias[i],
                   w_exp_gate[i], w_exp_up[i], w_exp_down[i], w_sh_gate[i], w_sh_up[i], w_sh_down[i],
                   ln2_g[i], ln2_b[i], w_ple_gate[i], w_ple_proj[i], ln3_g[i], ln3_b[i],
                   nbatch=nbatch, seq=seq, alpha=alpha)
    return h.reshape(nbatch, seq, d_model)
```

```python
import functools

import jax
import jax.numpy as jnp
import numpy as np
from jax import lax
from jax.experimental import pallas as pl
from jax.experimental.pallas import tpu as pltpu

N_HEADS = 8
HEAD_DIM = 64
ROT_DIM = 16
ROPE_THETA = 500000.0
MOBA_BLOCK = 256
MOBA_TOPK = 3
CONV_K = 3
N_EXPERTS = 256
TOP_K = 8
N_GROUPS = 8
TOPK_GROUPS = 4
ROUTED_SCALE = 2.5
LN_EPS = 1e-5
NEG = -1e30

LANES = 128
SUBLANES = 8
HEADS_PER_LANE_GROUP = LANES // HEAD_DIM
VMEM_LIMIT_BYTES = 56 * 1024 * 1024

MXU_DTYPE = jnp.bfloat16
F32 = jnp.float32
I32 = jnp.int32
U32 = jnp.uint32

TM_INPROJ = 512
TM_POST = 512
TM_DISPATCH = 256
TM_EXPERT = 512
TM_FINAL = 256


def _dot(a, b):
    return jnp.dot(a, b, preferred_element_type=F32)


def _dot_nt(a, b):
    return lax.dot_general(a, b, (((1,), (1,)), ((), ())), preferred_element_type=F32)


def _split_bf16(a):
    hi = a.astype(MXU_DTYPE)
    lo = (a - hi.astype(F32)).astype(MXU_DTYPE)
    return hi, lo


TOK_ROWS = 4
HI_MASK = 0xFFFF0000


def _pack_pairs(f):
    half = f.shape[1] // 2
    bits = lambda t: lax.bitcast_convert_type(t.astype(jnp.bfloat16).astype(F32), U32)
    return (bits(f[:, half:]) & jnp.uint32(HI_MASK)) | lax.shift_right_logical(bits(f[:, :half]), jnp.uint32(16))


def _unpack_pairs(w):
    lo = lax.bitcast_convert_type(lax.shift_left(w, jnp.uint32(16)), F32)
    hi = lax.bitcast_convert_type(w & jnp.uint32(HI_MASK), F32)
    return lo, hi


def _tt_load(ref, n_tok):
    chunks = [ref[pl.ds(s, n_tok, stride=TOK_ROWS), :] for s in range(TOK_ROWS)]
    return jnp.concatenate(chunks, axis=1)


def _tt_store(ref, words):
    n_tok = words.shape[0]
    for s in range(TOK_ROWS):
        ref[pl.ds(s, n_tok, stride=TOK_ROWS), :] = words[:, s * LANES:(s + 1) * LANES]


def _tile_copy(src, dst, src_tok, dst_tok, sem):
    rows = lambda t: pl.ds(pl.multiple_of(t * TOK_ROWS, TOK_ROWS), TOK_ROWS)
    return pltpu.make_async_copy(src.at[rows(src_tok)], dst.at[rows(dst_tok)], sem)


def _layer_norm(t, g, b):
    mu = jnp.mean(t, axis=-1, keepdims=True)
    d = t - mu
    var = jnp.mean(d * d, axis=-1, keepdims=True)
    return d * lax.rsqrt(var + LN_EPS) * g + b


def _silu(t):
    return t * jax.nn.sigmoid(t)


def _inproj_kernel(x_ref, w_ref, cos_ref, sa_ref, sb_ref, cw_ref,
                   q_ref, k_ref, v_ref, km_ref, yc_ref, ga_ref, gc_ref, ubuf,
                   *, tiles_per_seq, aw, cwid, d_model):
    tm = x_ref.shape[0]
    j = pl.program_id(0) % tiles_per_seq
    xb = x_ref[...].astype(MXU_DTYPE)
    cos, sa, sb = cos_ref[...], sa_ref[...], sb_ref[...]

    def proj(c0, n):
        return _dot(xb, w_ref[:, c0:c0 + n])

    def rope(z):
        parts = []
        for g in range(aw // LANES):
            zg = z[:, g * LANES:(g + 1) * LANES]
            parts.append(zg * cos
                         + pltpu.roll(zg, LANES - ROT_DIM // 2, 1) * sa
                         + pltpu.roll(zg, ROT_DIM // 2, 1) * sb)
        return jnp.concatenate(parts, axis=1)

    q_ref[...] = (rope(proj(0, aw)) * (HEAD_DIM ** -0.5)).astype(q_ref.dtype)

    k = rope(proj(aw, aw))
    k_ref[...] = k.astype(k_ref.dtype)
    nb_seq = km_ref.shape[1]
    blocks_per_tile = tm // MOBA_BLOCK

    @pl.when(j == 0)
    def _():
        km_ref[...] = jnp.zeros(km_ref.shape, km_ref.dtype)

    rows = lax.broadcasted_iota(I32, (nb_seq, aw), 0)
    km = km_ref[0]
    for bi in range(blocks_per_tile):
        mean = jnp.sum(k[bi * MOBA_BLOCK:(bi + 1) * MOBA_BLOCK], axis=0, keepdims=True) * (1.0 / MOBA_BLOCK)
        km = jnp.where(rows == j * blocks_per_tile + bi, mean, km)
    km_ref[0] = km

    v_ref[...] = proj(2 * aw, aw).astype(v_ref.dtype)

    c0 = 3 * aw
    cb = proj(c0, cwid)
    u = proj(c0 + cwid, cwid) * proj(c0 + 2 * cwid, cwid)

    @pl.when(j == 0)
    def _():
        ubuf[0:SUBLANES, :] = jnp.zeros((SUBLANES, cwid), F32)

    ubuf[SUBLANES:SUBLANES + tm, :] = u
    um1 = ubuf[SUBLANES - 1:SUBLANES - 1 + tm, :]
    um2 = ubuf[SUBLANES - 2:SUBLANES - 2 + tm, :]
    conv = cw_ref[0:1, :] * um2 + cw_ref[1:2, :] * um1 + cw_ref[2:3, :] * u
    yc_ref[...] = (cb * conv).astype(yc_ref.dtype)
    ubuf[0:SUBLANES, :] = u[tm - SUBLANES:tm, :]

    g0 = c0 + 3 * cwid
    ga_ref[...] = jax.nn.sigmoid(proj(g0, d_model))
    gc_ref[...] = jax.nn.sigmoid(proj(g0 + d_model, d_model))


def _inproj(x2, w_in_b, cos_t, sa_t, sb_t, conv_w, *, seq, aw, cwid):
    T, D = x2.shape
    tm = TM_INPROJ
    assert seq % tm == 0 and tm % MOBA_BLOCK == 0 and T % seq == 0
    tiles_per_seq = seq // tm
    nb_seq = seq // MOBA_BLOCK
    nbatch = T // seq
    row = lambda i: (i, 0)
    tab = lambda i: (i % tiles_per_seq, 0)
    kern = functools.partial(_inproj_kernel, tiles_per_seq=tiles_per_seq, aw=aw, cwid=cwid, d_model=D)
    return pl.pallas_call(
        kern,
        grid=(T // tm,),
        in_specs=[
            pl.BlockSpec((tm, D), row),
            pl.BlockSpec(w_in_b.shape, lambda i: (0, 0)),
            pl.BlockSpec((tm, LANES), tab),
            pl.BlockSpec((tm, LANES), tab),
            pl.BlockSpec((tm, LANES), tab),
            pl.BlockSpec(conv_w.shape, lambda i: (0, 0)),
        ],
        out_specs=[
            pl.BlockSpec((tm, aw), row),
            pl.BlockSpec((tm, aw), row),
            pl.BlockSpec((tm, aw), row),
            pl.BlockSpec((1, nb_seq, aw), lambda i: (i // tiles_per_seq, 0, 0)),
            pl.BlockSpec((tm, cwid), row),
            pl.BlockSpec((tm, D), row),
            pl.BlockSpec((tm, D), row),
        ],
        out_shape=[
            jax.ShapeDtypeStruct((T, aw), MXU_DTYPE),
            jax.ShapeDtypeStruct((T, aw), MXU_DTYPE),
            jax.ShapeDtypeStruct((T, aw), MXU_DTYPE),
            jax.ShapeDtypeStruct((nbatch, nb_seq, aw), F32),
            jax.ShapeDtypeStruct((T, cwid), MXU_DTYPE),
            jax.ShapeDtypeStruct((T, D), F32),
            jax.ShapeDtypeStruct((T, D), F32),
        ],
        scratch_shapes=[pltpu.VMEM((tm + SUBLANES, cwid), F32)],
        compiler_params=pltpu.CompilerParams(
            dimension_semantics=("arbitrary",), vmem_limit_bytes=VMEM_LIMIT_BYTES),
        name="inproj",
    )(x2, w_in_b, cos_t, sa_t, sb_t, conv_w)


def _attn_kernel(q_ref, k_ref, v_ref, km_ref, o_ref, vt_ref, sel_ref, s_ref, m_ref, l_ref, acc_ref):
    qb = pl.program_id(1)
    blk = MOBA_BLOCK
    nb = km_ref.shape[1]
    n_groups = q_ref.shape[2] // LANES
    hpg = HEADS_PER_LANE_GROUP
    lanes = lambda g: slice(g * LANES, (g + 1) * LANES)

    @pl.when(qb == 0)
    def _():
        for b in range(nb):
            for g in range(n_groups):
                vt_ref[b, lanes(g), :] = v_ref[0, b * blk:(b + 1) * blk, lanes(g)].astype(F32).T.astype(vt_ref.dtype)

    blk_id = lax.broadcasted_iota(I32, (nb, blk), 0).astype(F32)
    past = blk_id < qb.astype(F32)
    dim = lax.broadcasted_iota(I32, (LANES, blk), 0)

    qts = []
    for g in range(n_groups):
        q2t = q_ref[0, :, lanes(g)].astype(F32).T
        km_hi, km_lo = _split_bf16(km_ref[0, :, lanes(g)])
        for h in range(hpg):
            in_head = (dim >= h * HEAD_DIM) & (dim < (h + 1) * HEAD_DIM)
            qt = jnp.where(in_head, q2t, 0.0).astype(MXU_DTYPE)
            qts.append(qt)
            gate = jnp.where(past, _dot(km_hi, qt) + _dot(km_lo, qt), -jnp.inf)
            sel = jnp.zeros((nb, blk), F32)
            for _ in range(MOBA_TOPK):
                mx = jnp.max(gate, axis=0, keepdims=True)
                idx = jnp.min(jnp.where(gate == mx, blk_id, float(nb)), axis=0, keepdims=True)
                pick = blk_id == idx
                sel = jnp.where(pick, jnp.where(past, 1.0, sel), sel)
                gate = jnp.where(pick, -jnp.inf, gate)
            sel_ref[g * hpg + h] = sel

    def v_t(jb, hd):
        return vt_ref[jb, hd * HEAD_DIM:(hd + 1) * HEAD_DIM, :]

    def scores_to_scratch(g, kg, nk):
        for h in range(hpg):
            s_ref[g % 2, h, 0:nk * blk, :] = _dot(kg, qts[g * hpg + h])

    def run_groups(key_rows, nk, update):
        scores_to_scratch(0, k_ref[0, key_rows, lanes(0)], nk)
        for g in range(n_groups):
            if g + 1 < n_groups:
                scores_to_scratch(g + 1, k_ref[0, key_rows, lanes(g + 1)], nk)
            for h in range(hpg):
                update(g * hpg + h, s_ref[g % 2, h, 0:nk * blk, :])

    key_pos = lax.broadcasted_iota(I32, (blk, blk), 0)
    qry_pos = lax.broadcasted_iota(I32, (blk, blk), 1)

    def init(hd, s_raw):
        s = jnp.where(key_pos <= qry_pos, s_raw, NEG)
        m = jnp.max(s, axis=0, keepdims=True)
        p = jnp.exp(s - m)
        m_ref[hd] = m
        l_ref[hd] = jnp.sum(p, axis=0, keepdims=True)
        acc_ref[hd] = _dot(v_t(qb, hd), p.astype(MXU_DTYPE))

    run_groups(pl.ds(pl.multiple_of(qb * blk, blk), blk), 1, init)

    def make_step(nk):
        def step(b0):
            def update(hd, s_raw):
                parts = [jnp.where(sel_ref[hd, pl.ds(b0 + t, 1), :] > 0.5,
                                   s_raw[t * blk:(t + 1) * blk], NEG) for t in range(nk)]
                s = jnp.concatenate(parts, axis=0) if nk > 1 else parts[0]
                m = m_ref[hd]
                m_new = jnp.maximum(m, jnp.max(s, axis=0, keepdims=True))
                alpha = jnp.exp(m - m_new)
                p = jnp.exp(s - m_new)
                vt = [v_t(b0 + t, hd) for t in range(nk)]
                vt = jnp.concatenate(vt, axis=1) if nk > 1 else vt[0]
                m_ref[hd] = m_new
                l_ref[hd] = alpha * l_ref[hd] + jnp.sum(p, axis=0, keepdims=True)
                acc_ref[hd] = alpha * acc_ref[hd] + _dot(vt, p.astype(MXU_DTYPE))

            run_groups(pl.ds(pl.multiple_of(b0 * blk, blk), nk * blk), nk, update)
        return step

    pair, single = make_step(2), make_step(1)
    n_pairs = lax.shift_right_logical(qb, 1)
    lax.fori_loop(0, n_pairs, lambda i, c: (pair(2 * i), c)[1], 0)
    lax.fori_loop(0, qb & 1, lambda i, c: (single(qb - 1), c)[1], 0)

    for g in range(n_groups):
        outs = [acc_ref[g * hpg + h] / l_ref[g * hpg + h] for h in range(hpg)]
        o_ref[0, :, lanes(g)] = jnp.concatenate(outs, axis=0).T.astype(o_ref.dtype)


def _attention(q, k, v, km, *, nbatch, seq, aw):
    blk = MOBA_BLOCK
    nb = seq // blk
    n_heads = aw // HEAD_DIM
    max_nk = 2
    assert nb % SUBLANES == 0
    q3, k3, v3 = (t.reshape(nbatch, seq, aw) for t in (q, k, v))
    out = pl.pallas_call(
        _attn_kernel,
        grid=(nbatch, nb),
        in_specs=[
            pl.BlockSpec((1, blk, aw), lambda b, i: (b, i, 0)),
            pl.BlockSpec((1, seq, aw), lambda b, i: (b, 0, 0)),
            pl.BlockSpec((1, seq, aw), lambda b, i: (b, 0, 0)),
            pl.BlockSpec((1, nb, aw), lambda b, i: (b, 0, 0)),
        ],
        out_specs=pl.BlockSpec((1, blk, aw), lambda b, i: (b, i, 0)),
        out_shape=jax.ShapeDtypeStruct((nbatch, seq, aw), MXU_DTYPE),
        scratch_shapes=[
            pltpu.VMEM((nb, aw, blk), MXU_DTYPE),
            pltpu.VMEM((n_heads, nb, blk), F32),
            pltpu.VMEM((2, HEADS_PER_LANE_GROUP, max_nk * blk, blk), F32),
            pltpu.VMEM((n_heads, 1, blk), F32),
            pltpu.VMEM((n_heads, 1, blk), F32),
            pltpu.VMEM((n_heads, HEAD_DIM, blk), F32),
        ],
        compiler_params=pltpu.CompilerParams(
            dimension_semantics=("arbitrary", "arbitrary"), vmem_limit_bytes=VMEM_LIMIT_BYTES),
        name="moba_attn",
    )(q3, k3, v3, km)
    return out.reshape(nbatch * seq, aw)


def _post_kernel(attn_ref, yc_ref, ga_ref, gc_ref, x_ref, wao_ref, wco_ref, wo_ref,
                 g1_ref, b1_ref, wrh_ref, wrl_ref, rb_ref,
                 h1_ref, h1w_ref, ti_ref, tw_ref, rk_ref, cnt_ref, cnt_acc, *, alpha):
    i = pl.program_id(0)
    tm = x_ref.shape[0]
    ne = wrh_ref.shape[0]

    y_attn = _dot(attn_ref[...], wao_ref[...])
    y_conv = _dot(yc_ref[...], wco_ref[...])
    merged = ga_ref[...] * y_attn + gc_ref[...] * y_conv
    mix = _dot(merged.astype(MXU_DTYPE), wo_ref[...])
    h1 = _layer_norm(alpha * x_ref[...] + mix, g1_ref[...], b1_ref[...])
    h1_ref[...] = h1
    _tt_store(h1w_ref, _pack_pairs(h1))

    h_hi, h_lo = _split_bf16(h1)
    logits = _dot_nt(wrh_ref[...], h_hi) + _dot_nt(wrh_ref[...], h_lo) + _dot_nt(wrl_ref[...], h_hi)
    scores = jax.nn.sigmoid(logits)
    choice = scores + rb_ref[...]

    gsz = ne // N_GROUPS
    gshape = (N_GROUPS, gsz, tm)
    c3 = choice.reshape(gshape)
    in_grp = lax.broadcasted_iota(I32, gshape, 1).astype(F32)
    m1 = jnp.max(c3, axis=1, keepdims=True)
    i1 = jnp.min(jnp.where(c3 == m1, in_grp, float(gsz)), axis=1, keepdims=True)
    m2 = jnp.max(jnp.where(in_grp == i1, -jnp.inf, c3), axis=1, keepdims=True)
    gscore = jnp.broadcast_to(m1 + m2, gshape).reshape(ne, tm)

    eid = lax.broadcasted_iota(I32, (ne, tm), 0).astype(F32)
    gid = lax.broadcasted_iota(I32, gshape, 0).astype(F32).reshape(ne, tm)
    cand = jnp.full((ne, tm), -jnp.inf, F32)
    for _ in range(TOPK_GROUPS):
        mx = jnp.max(gscore, axis=0, keepdims=True)
        idx = jnp.min(jnp.where(gscore == mx, gid, float(N_GROUPS)), axis=0, keepdims=True)
        pick = gid == idx
        cand = jnp.where(pick, choice, cand)
        gscore = jnp.where(pick, -jnp.inf, gscore)

    selmat = jnp.zeros((ne, tm), F32)
    idxs, svals = [], []
    for _ in range(TOP_K):
        mx = jnp.max(cand, axis=0, keepdims=True)
        idx = jnp.min(jnp.where(cand == mx, eid, float(ne)), axis=0, keepdims=True)
        pick = eid == idx
        svals.append(jnp.sum(jnp.where(pick, scores, 0.0), axis=0, keepdims=True))
        idxs.append(idx)
        selmat = jnp.where(pick, 1.0, selmat)
        cand = jnp.where(pick, -jnp.inf, cand)
    ssum = svals[0]
    for r in range(1, TOP_K):
        ssum = ssum + svals[r]

    @pl.when(i == 0)
    def _():
        cnt_acc[...] = jnp.zeros(cnt_acc.shape, F32)

    tr = lax.broadcasted_iota(I32, (tm, tm), 0)
    tc = lax.broadcasted_iota(I32, (tm, tm), 1)
    upper = jnp.where(tr < tc, 1.0, 0.0).astype(MXU_DTYPE)
    selb = selmat.astype(MXU_DTYPE)
    base = cnt_acc[...]
    rank = _dot(selb, upper) + jnp.concatenate([base] * (tm // LANES), axis=1)
    new_cnt = base + _dot(selb, jnp.ones((tm, LANES), MXU_DTYPE))
    cnt_acc[...] = new_cnt
    cnt_ref[...] = new_cnt

    for r in range(TOP_K):
        pick = eid == idxs[r]
        ti_ref[r:r + 1, :] = idxs[r].astype(I32)
        tw_ref[r:r + 1, :] = svals[r] / ssum * ROUTED_SCALE
        rk_ref[r:r + 1, :] = jnp.sum(jnp.where(pick, rank, 0.0), axis=0, keepdims=True).astype(I32)


def _post(attn, yc, ga, gc, x2, wao, wco, wo, g1, b1, wrh, wrl, rb, *, alpha):
    T, D = x2.shape
    tm = TM_POST
    assert D == 2 * TOK_ROWS * LANES
    aw, cwid = attn.shape[1], yc.shape[1]
    ne = wrh.shape[0]
    row = lambda i: (i, 0)
    full = lambda i: (0, 0)
    col = lambda i: (0, i)
    return pl.pallas_call(
        functools.partial(_post_kernel, alpha=alpha),
        grid=(T // tm,),
        in_specs=[
            pl.BlockSpec((tm, aw), row), pl.BlockSpec((tm, cwid), row),
            pl.BlockSpec((tm, D), row), pl.BlockSpec((tm, D), row), pl.BlockSpec((tm, D), row),
            pl.BlockSpec(wao.shape, full), pl.BlockSpec(wco.shape, full), pl.BlockSpec(wo.shape, full),
            pl.BlockSpec(g1.shape, full), pl.BlockSpec(b1.shape, full),
            pl.BlockSpec(wrh.shape, full), pl.BlockSpec(wrl.shape, full), pl.BlockSpec(rb.shape, full),
        ],
        out_specs=[
            pl.BlockSpec((tm, D), row),
            pl.BlockSpec((tm * TOK_ROWS, LANES), row),
            pl.BlockSpec((TOP_K, tm), col), pl.BlockSpec((TOP_K, tm), col), pl.BlockSpec((TOP_K, tm), col),
            pl.BlockSpec((ne, LANES), full),
        ],
        out_shape=[
            jax.ShapeDtypeStruct((T, D), F32),
            jax.ShapeDtypeStruct((T * TOK_ROWS, LANES), U32),
            jax.ShapeDtypeStruct((TOP_K, T), I32),
            jax.ShapeDtypeStruct((TOP_K, T), F32),
            jax.ShapeDtypeStruct((TOP_K, T), I32),
            jax.ShapeDtypeStruct((ne, LANES), F32),
        ],
        scratch_shapes=[pltpu.VMEM((ne, LANES), F32)],
        compiler_params=pltpu.CompilerParams(
            dimension_semantics=("arbitrary",), vmem_limit_bytes=VMEM_LIMIT_BYTES),
        name="post_route",
    )(attn, yc, ga, gc, x2, wao, wco, wo, g1, b1, wrh, wrl, rb)


def _slots_kernel(pstart_ref, ti_ref, rk_ref, dest_ref):
    ti = ti_ref[...]
    start = lax.fori_loop(0, N_EXPERTS, lambda e, acc: jnp.where(ti == e, pstart_ref[e], acc),
                          jnp.zeros(ti.shape, I32))
    dest_ref[...] = start + rk_ref[...]


def _slots(pstart, topi, rnk):
    T = topi.shape[1]
    tc = min(T, 2048)
    blk = pl.BlockSpec((TOP_K, tc), lambda i, *_: (0, i))
    return pl.pallas_call(
        _slots_kernel,
        grid_spec=pltpu.PrefetchScalarGridSpec(
            num_scalar_prefetch=1, grid=(T // tc,), in_specs=[blk, blk], out_specs=blk),
        out_shape=jax.ShapeDtypeStruct((TOP_K, T), I32),
        compiler_params=pltpu.CompilerParams(dimension_semantics=("arbitrary",)),
        name="slots",
    )(pstart, topi, rnk)


def _dispatch_kernel(zrow_ref, nzero_ref, dest_ref, h1_ref, xs_hbm, zbuf, zsem, sem):
    i = pl.program_id(0)
    tm = dest_ref.shape[1]
    zrows = zbuf.shape[0]

    def zero_copy(z):
        start = pl.multiple_of(zrow_ref[z] * TOK_ROWS, zrows)
        return pltpu.make_async_copy(zbuf, xs_hbm.at[pl.ds(start, zrows)], zsem)

    @pl.when(i == 0)
    def _():
        zbuf[...] = jnp.zeros(zbuf.shape, zbuf.dtype)
        nz = nzero_ref[0]
        lax.fori_loop(0, nz, lambda z, c: (zero_copy(z).start(), c)[1], 0)
        lax.fori_loop(0, nz, lambda z, c: (zero_copy(z).wait(), c)[1], 0)

    def copy(k, j):
        return _tile_copy(h1_ref, xs_hbm, j, dest_ref[k, j], sem)

    def issue(j, c):
        for k in range(TOP_K):
            copy(k, j).start(priority=k % 2)
        return c

    def drain(j, c):
        for k in range(TOP_K):
            copy(k, j).wait()
        return c

    lax.fori_loop(0, tm, issue, 0)
    lax.fori_loop(0, tm, drain, 0)


def _dispatch(zrow, nzero, dest, h1, *, n_rows):
    T = h1.shape[0] // TOK_ROWS
    tm = TM_DISPATCH
    return pl.pallas_call(
        _dispatch_kernel,
        grid_spec=pltpu.PrefetchScalarGridSpec(
            num_scalar_prefetch=2,
            grid=(T // tm,),
            in_specs=[pl.BlockSpec((TOP_K, tm), lambda i, *_: (0, i), memory_space=pltpu.SMEM),
                      pl.BlockSpec((tm * TOK_ROWS, LANES), lambda i, *_: (i, 0))],
            out_specs=pl.BlockSpec(memory_space=pl.ANY),
            scratch_shapes=[pltpu.VMEM((TM_EXPERT * TOK_ROWS, LANES), h1.dtype),
                            pltpu.SemaphoreType.DMA, pltpu.SemaphoreType.DMA],
        ),
        out_shape=jax.ShapeDtypeStruct((n_rows * TOK_ROWS, LANES), h1.dtype),
        compiler_params=pltpu.CompilerParams(
            dimension_semantics=("arbitrary",), vmem_limit_bytes=VMEM_LIMIT_BYTES),
        name="dispatch",
    )(zrow, nzero, dest, h1)


def _experts_kernel(be_ref, nused_ref, bvalid_ref, first_ref, eslot_ref, nexte_ref,
                    xs_ref, wg_hbm, wu_hbm, wd_hbm, ys_ref, wg_f, wu_f, wd_f, wgu_b, wd_b, wsem):
    i = pl.program_id(0)
    ed = wg_hbm.shape[2]
    te = xs_ref.shape[0] // TOK_ROWS

    def weight_copies(e, slot):
        return [pltpu.make_async_copy(src.at[e], dst.at[slot], wsem.at[slot])
                for src, dst in ((wg_hbm, wg_f), (wu_hbm, wu_f), (wd_hbm, wd_f))]

    def swiglu_rows(n_tok):
        x_lo, x_hi = _unpack_pairs(_tt_load(xs_ref, n_tok))
        xb = jnp.concatenate([x_lo.astype(MXU_DTYPE), x_hi.astype(MXU_DTYPE)], axis=1)
        gu = _dot(xb, wgu_b[...])
        hb = _silu(gu[:, 0:ed]) * gu[:, ed:2 * ed]
        _tt_store(ys_ref, _pack_pairs(_dot(hb.astype(MXU_DTYPE), wd_b[...])))

    @pl.when(i < nused_ref[0])
    def _():
        @pl.when(first_ref[i] == 1)
        def _():
            slot = eslot_ref[i]

            @pl.when(i == 0)
            def _():
                for c in weight_copies(be_ref[i], slot):
                    c.start()

            for c in weight_copies(be_ref[i], slot):
                c.wait()
            wgu_b[:, 0:ed] = wg_f[slot].astype(MXU_DTYPE)
            wgu_b[:, ed:2 * ed] = wu_f[slot].astype(MXU_DTYPE)
            wd_b[...] = wd_f[slot].astype(MXU_DTYPE)

            @pl.when(nexte_ref[i] >= 0)
            def _():
                for c in weight_copies(nexte_ref[i], 1 - slot):
                    c.start()

        nvalid = bvalid_ref[i]

        @pl.when(nvalid > te // 2)
        def _():
            swiglu_rows(te)

        @pl.when(nvalid <= te // 2)
        def _():
            swiglu_rows(te // 2)
            rest = ys_ref.shape[0] // 2
            ys_ref[rest:, :] = jnp.zeros((rest, ys_ref.shape[1]), ys_ref.dtype)


def _experts(sched, xs, wg, wu, wd):
    te = TM_EXPERT
    D, ed = wg.shape[1], wg.shape[2]
    nblk = xs.shape[0] // (te * TOK_ROWS)
    blk_map = lambda i, be, nu, *_: (jnp.minimum(i, nu[0] - 1), 0)
    hbm = pl.BlockSpec(memory_space=pl.ANY)
    return pl.pallas_call(
        _experts_kernel,
        grid_spec=pltpu.PrefetchScalarGridSpec(
            num_scalar_prefetch=len(sched),
            grid=(nblk,),
            in_specs=[pl.BlockSpec((te * TOK_ROWS, LANES), blk_map), hbm, hbm, hbm],
            out_specs=pl.BlockSpec((te * TOK_ROWS, LANES), blk_map),
            scratch_shapes=[
                pltpu.VMEM((2, D, ed), wg.dtype), pltpu.VMEM((2, D, ed), wu.dtype),
                pltpu.VMEM((2, ed, D), wd.dtype),
                pltpu.VMEM((D, 2 * ed), MXU_DTYPE), pltpu.VMEM((ed, D), MXU_DTYPE),
                pltpu.SemaphoreType.DMA((2,)),
            ],
        ),
        out_shape=jax.ShapeDtypeStruct(xs.shape, xs.dtype),
        compiler_params=pltpu.CompilerParams(
            dimension_semantics=("arbitrary",), vmem_limit_bytes=VMEM_LIMIT_BYTES),
        name="experts",
    )(*sched, xs, wg, wu, wd)


def _final_kernel(dest_ref, dest_next_ref, tw_ref, h1_ref, p_ref, ys_hbm, wsgu_ref, wsd_ref, wpg_ref,
                  wpp_ref, g2_ref, b2_ref, g3_ref, b3_ref, o_ref, gbuf_a, gbuf_b, routed_ref, sems, *, alpha):
    i = pl.program_id(0)
    n = pl.num_programs(0)
    tm = o_ref.shape[0]
    sd = wsd_ref.shape[0]
    half = routed_ref.shape[1] // 2
    bufs = (gbuf_a, gbuf_b)

    def copy(idx_ref, slot, k, j):
        return _tile_copy(ys_hbm, bufs[slot].at[k], idx_ref[k, j], j, sems.at[slot])

    def gather_loop(idx_ref, slot, wait):
        def body(j, c):
            for k in range(TOP_K):
                if wait:
                    copy(idx_ref, slot, k, j).wait()
                else:
                    copy(idx_ref, slot, k, j).start(priority=k % 2)
            return c
        lax.fori_loop(0, tm, body, 0)

    @pl.when(i == 0)
    def _():
        gather_loop(dest_ref, 0, wait=False)

    def tile(cur):
        nxt = 1 - cur
        gbuf = bufs[cur]
        h1 = h1_ref[...]
        gu = _dot(h1.astype(MXU_DTYPE), wsgu_ref[...])
        shared = _dot((_silu(gu[:, 0:sd]) * gu[:, sd:2 * sd]).astype(MXU_DTYPE), wsd_ref[...])

        gather_loop(dest_ref, cur, wait=True)

        tw = tw_ref[...]
        wt = jnp.concatenate([tw, jnp.zeros((LANES - TOP_K, tm), F32)], axis=0).T
        for r in range(tm // SUBLANES):
            for j in range(r * SUBLANES, (r + 1) * SUBLANES):
                for k in range(TOP_K):
                    copy(dest_next_ref, nxt, k, j).start(priority=k % 2)
            wr = wt[r * SUBLANES:(r + 1) * SUBLANES, :]
            wb = [jnp.broadcast_to(wr[:, k:k + 1], (SUBLANES, LANES)) for k in range(TOP_K)]
            for s in range(TOK_ROWS):
                rows = pl.ds(r * SUBLANES * TOK_ROWS + s, SUBLANES, stride=TOK_ROWS)
                acc_lo = acc_hi = None
                for k in range(TOP_K):
                    y_lo, y_hi = _unpack_pairs(gbuf[k, rows, :])
                    acc_lo = wb[k] * y_lo if k == 0 else acc_lo + wb[k] * y_lo
                    acc_hi = wb[k] * y_hi if k == 0 else acc_hi + wb[k] * y_hi
                tok = slice(r * SUBLANES, (r + 1) * SUBLANES)
                routed_ref[tok, s * LANES:(s + 1) * LANES] = acc_lo
                routed_ref[tok, half + s * LANES:half + (s + 1) * LANES] = acc_hi
        routed = routed_ref[...]

        h2 = _layer_norm(alpha * h1 + (routed + shared), g2_ref[...], b2_ref[...])
        gate = jax.nn.sigmoid(_dot(h2.astype(MXU_DTYPE), wpg_ref[...]))
        ple = gate * _dot(p_ref[...].astype(MXU_DTYPE), wpp_ref[...])
        o_ref[...] = _layer_norm(alpha * h2 + ple, g3_ref[...], b3_ref[...])

        @pl.when(i == n - 1)
        def _():
            gather_loop(dest_next_ref, nxt, wait=True)

    for parity in range(2):
        pl.when(i % 2 == parity)(functools.partial(tile, parity))


def _final(dest, topw, h1, p2, ys, wsgu, wsd, wpg, wpp, g2, b2, g3, b3, *, alpha):
    T, D = p2.shape[0], wpg.shape[0]
    tm = TM_FINAL
    n_tiles = T // tm
    row = lambda i: (i, 0)
    full = lambda i: (0, 0)
    col = lambda i: (0, i)
    col_next = lambda i: (0, jnp.minimum(i + 1, n_tiles - 1))
    return pl.pallas_call(
        functools.partial(_final_kernel, alpha=alpha),
        grid=(n_tiles,),
        in_specs=[
            pl.BlockSpec((TOP_K, tm), col, memory_space=pltpu.SMEM),
            pl.BlockSpec((TOP_K, tm), col_next, memory_space=pltpu.SMEM),
            pl.BlockSpec((TOP_K, tm), col),
            pl.BlockSpec((tm, D), row),
            pl.BlockSpec((tm, p2.shape[1]), row),
            pl.BlockSpec(memory_space=pl.ANY),
            pl.BlockSpec(wsgu.shape, full), pl.BlockSpec(wsd.shape, full),
            pl.BlockSpec(wpg.shape, full), pl.BlockSpec(wpp.shape, full),
            pl.BlockSpec(g2.shape, full), pl.BlockSpec(b2.shape, full),
            pl.BlockSpec(g3.shape, full), pl.BlockSpec(b3.shape, full),
        ],
        out_specs=pl.BlockSpec((tm, D), row),
        out_shape=jax.ShapeDtypeStruct((T, D), F32),
        scratch_shapes=[pltpu.VMEM((TOP_K, tm * TOK_ROWS, LANES), ys.dtype),
                        pltpu.VMEM((TOP_K, tm * TOK_ROWS, LANES), ys.dtype),
                        pltpu.VMEM((tm, D), F32), pltpu.SemaphoreType.DMA((2,))],
        compiler_params=pltpu.CompilerParams(
            dimension_semantics=("arbitrary",), vmem_limit_bytes=VMEM_LIMIT_BYTES),
        name="final",
    )(dest, dest, topw, h1, p2, ys, wsgu, wsd, wpg, wpp, g2, b2, g3, b3)


def _rope_tables(seq):
    half = ROT_DIM // 2
    f32 = np.float32
    inv_freq = f32(ROPE_THETA) ** (-np.arange(0, ROT_DIM, 2, dtype=f32) / f32(ROT_DIM))
    ang = np.arange(seq, dtype=np.int32).astype(f32)[:, None] * inv_freq[None, :]
    cos, sin = np.cos(ang), np.sin(ang)
    ones = np.ones((seq, HEAD_DIM - ROT_DIM), f32)
    zeros = np.zeros((seq, HEAD_DIM - ROT_DIM), f32)
    zh = np.zeros((seq, half), f32)
    cos_h = np.concatenate([cos, cos, ones], axis=1)
    sa_h = np.concatenate([-sin, zh, zeros], axis=1)
    sb_h = np.concatenate([zh, sin, zeros], axis=1)
    rep = lambda t: jnp.asarray(np.concatenate([t] * HEADS_PER_LANE_GROUP, axis=1), F32)
    return rep(cos_h), rep(sa_h), rep(sb_h)


def _expert_layout(counts, n_blocks):
    te = TM_EXPERT
    nblk_e = (counts + te - 1) // te
    blk_end = jnp.cumsum(nblk_e)
    blk_start = blk_end - nblk_e
    nused = blk_end[-1]
    pstart = (blk_start * te).astype(I32)
    bid = jnp.arange(n_blocks, dtype=I32)
    block_e = jnp.sum(blk_end[None, :] <= jnp.minimum(bid, nused - 1)[:, None], axis=1).astype(I32)
    block_e = jnp.minimum(block_e, N_EXPERTS - 1)
    bvalid = jnp.clip(counts[block_e] - (bid - blk_start[block_e]) * te, 0, te).astype(I32)
    first = ((bid == blk_start[block_e]) & (bid < nused)).astype(I32)
    eslot = ((jnp.cumsum(first) - 1) & 1).astype(I32)
    after = blk_end[block_e]
    nexte = jnp.where(after < nused, block_e[jnp.minimum(after, n_blocks - 1)], -1).astype(I32)
    partial = (counts % te) != 0
    order = jnp.argsort(jnp.logical_not(partial), stable=True).astype(I32)
    n_partial = jnp.sum(partial).astype(I32)
    last_blk = (blk_end - 1).astype(I32)
    zi = jnp.arange(n_blocks + N_EXPERTS, dtype=I32)
    zblk = jnp.where(zi < n_partial, last_blk[order[jnp.minimum(zi, N_EXPERTS - 1)]],
                     nused + (zi - n_partial))
    nzero = n_partial + (n_blocks - nused)
    zrow = (jnp.clip(zblk, 0, n_blocks - 1) * te).astype(I32)
    sched = (block_e, nused.astype(I32).reshape(1), bvalid, first, eslot, nexte)
    return pstart, sched, zrow, nzero.astype(I32).reshape(1)


def _layer(h, p2, w_in, conv_w, w_attn_out, w_conv_out, w_out, ln1_g, ln1_b, w_router, router_bias,
           w_exp_gate, w_exp_up, w_exp_down, w_sh_gate, w_sh_up, w_sh_down, ln2_g, ln2_b,
           w_ple_gate, w_ple_proj, ln3_g, ln3_b, *, nbatch, seq, alpha):
    T, D = h.shape
    aw = w_attn_out.shape[0]
    cwid = w_conv_out.shape[0]
    bf = lambda w: w.astype(MXU_DTYPE)
    rowv = lambda g: g.reshape(1, -1)

    cos_t, sa_t, sb_t = _rope_tables(seq)
    q, k, v, km, yc, ga, gc = _inproj(h, bf(w_in), cos_t, sa_t, sb_t, conv_w, seq=seq, aw=aw, cwid=cwid)
    attn = _attention(q, k, v, km, nbatch=nbatch, seq=seq, aw=aw)

    wr_t = w_router.astype(F32).T
    wrh = wr_t.astype(MXU_DTYPE)
    wrl = (wr_t - wrh.astype(F32)).astype(MXU_DTYPE)
    h1, h1w, topi, topw, rnk, cnt = _post(
        attn, yc, ga, gc, h, bf(w_attn_out), bf(w_conv_out), bf(w_out), rowv(ln1_g), rowv(ln1_b),
        wrh, wrl, router_bias.astype(F32).reshape(-1, 1), alpha=alpha)

    n_blocks = (T * TOP_K) // TM_EXPERT + N_EXPERTS
    counts = cnt[:, 0].astype(I32)
    pstart, sched, zrow, nzero = _expert_layout(counts, n_blocks)
    dest = _slots(pstart, topi, rnk)
    xs = _dispatch(zrow, nzero, dest, h1w, n_rows=n_blocks * TM_EXPERT)
    ys = _experts(sched, xs, w_exp_gate, w_exp_up, w_exp_down)

    wsgu = jnp.concatenate([bf(w_sh_gate), bf(w_sh_up)], axis=1)
    return _final(dest, topw, h1, p2, ys, wsgu, bf(w_sh_down), bf(w_ple_gate), bf(w_ple_proj),
                  rowv(ln2_g), rowv(ln2_b), rowv(ln3_g), rowv(ln3_b), alpha=alpha)


def kernel(x, p, w_in, conv_w, w_attn_out, w_conv_out, w_out, ln1_g, ln1_b, w_router, router_bias,
           w_exp_gate, w_exp_up, w_exp_down, w_sh_gate, w_sh_up, w_sh_down, ln2_g, ln2_b,
           w_ple_gate, w_ple_proj, ln3_g, ln3_b):
    nbatch, seq, d_model = x.shape
    depth = w_in.shape[0]
    alpha = (2 * depth) ** 0.25
    assert seq % MOBA_BLOCK == 0
    h = x.reshape(nbatch * seq, d_model)
    for i in range(depth):
        h = _layer(h, p[i].reshape(nbatch * seq, -1), w_in[i], conv_w[i], w_attn_out[i], w_conv_out[i],
                   w_out[i], ln1_g[i], ln1_b[i], w_router[i], router_bias[i],
                   w_exp_gate[i], w_exp_up[i], w_exp_down[i], w_sh_gate[i], w_sh_up[i], w_sh_down[i],
                   ln2_g[i], ln2_b[i], w_ple_gate[i], w_ple_proj[i], ln3_g[i], ln3_b[i],
                   nbatch=nbatch, seq=seq, alpha=alpha)
    return h.reshape(nbatch, seq, d_model)
```

```python
import functools

import jax
import jax.numpy as jnp
import numpy as np
from jax import lax
from jax.experimental import pallas as pl
from jax.experimental.pallas import tpu as pltpu

N_HEADS = 8
HEAD_DIM = 64
ROT_DIM = 16
ROPE_THETA = 500000.0
MOBA_BLOCK = 256
MOBA_TOPK = 3
CONV_K = 3
N_EXPERTS = 256
TOP_K = 8
N_GROUPS = 8
TOPK_GROUPS = 4
ROUTED_SCALE = 2.5
LN_EPS = 1e-5
NEG = -1e30

LANES = 128
SUBLANES = 8
HEADS_PER_LANE_GROUP = LANES // HEAD_DIM
VMEM_LIMIT_BYTES = 56 * 1024 * 1024

MXU_DTYPE = jnp.bfloat16
F32 = jnp.float32
I32 = jnp.int32
U32 = jnp.uint32

TM_INPROJ = 512
TM_POST = 512
TM_DISPATCH = 512
TM_EXPERT = 512
TM_FINAL = 256


def _dot(a, b):
    return jnp.dot(a, b, preferred_element_type=F32)


def _dot_nt(a, b):
    return lax.dot_general(a, b, (((1,), (1,)), ((), ())), preferred_element_type=F32)


def _split_bf16(a):
    hi = a.astype(MXU_DTYPE)
    lo = (a - hi.astype(F32)).astype(MXU_DTYPE)
    return hi, lo


TOK_ROWS = 4
HI_MASK = 0xFFFF0000


def _pack_pairs(f):
    half = f.shape[1] // 2
    bits = lambda t: lax.bitcast_convert_type(t.astype(jnp.bfloat16).astype(F32), U32)
    return (bits(f[:, half:]) & jnp.uint32(HI_MASK)) | lax.shift_right_logical(bits(f[:, :half]), jnp.uint32(16))


def _unpack_pairs(w):
    lo = lax.bitcast_convert_type(lax.shift_left(w, jnp.uint32(16)), F32)
    hi = lax.bitcast_convert_type(w & jnp.uint32(HI_MASK), F32)
    return lo, hi


def _tt_load(ref, n_tok):
    chunks = [ref[pl.ds(s, n_tok, stride=TOK_ROWS), :] for s in range(TOK_ROWS)]
    return jnp.concatenate(chunks, axis=1)


def _tt_store(ref, words):
    n_tok = words.shape[0]
    for s in range(TOK_ROWS):
        ref[pl.ds(s, n_tok, stride=TOK_ROWS), :] = words[:, s * LANES:(s + 1) * LANES]


def _tile_copy(src, dst, src_tok, dst_tok, sem):
    rows = lambda t: pl.ds(pl.multiple_of(t * TOK_ROWS, TOK_ROWS), TOK_ROWS)
    return pltpu.make_async_copy(src.at[rows(src_tok)], dst.at[rows(dst_tok)], sem)


def _layer_norm(t, g, b):
    mu = jnp.mean(t, axis=-1, keepdims=True)
    d = t - mu
    var = jnp.mean(d * d, axis=-1, keepdims=True)
    return d * lax.rsqrt(var + LN_EPS) * g + b


def _silu(t):
    return t * jax.nn.sigmoid(t)


def _inproj_kernel(x_ref, w_ref, cos_ref, sa_ref, sb_ref, cw_ref,
                   q_ref, k_ref, v_ref, km_ref, yc_ref, ga_ref, gc_ref, ubuf,
                   *, tiles_per_seq, aw, cwid, d_model):
    tm = x_ref.shape[0]
    j = pl.program_id(0) % tiles_per_seq
    xb = x_ref[...].astype(MXU_DTYPE)
    cos, sa, sb = cos_ref[...], sa_ref[...], sb_ref[...]

    def proj(c0, n):
        return _dot(xb, w_ref[:, c0:c0 + n])

    def rope(z):
        parts = []
        for g in range(aw // LANES):
            zg = z[:, g * LANES:(g + 1) * LANES]
            parts.append(zg * cos
                         + pltpu.roll(zg, LANES - ROT_DIM // 2, 1) * sa
                         + pltpu.roll(zg, ROT_DIM // 2, 1) * sb)
        return jnp.concatenate(parts, axis=1)

    q_ref[...] = (rope(proj(0, aw)) * (HEAD_DIM ** -0.5)).astype(q_ref.dtype)

    k = rope(proj(aw, aw))
    k_ref[...] = k.astype(k_ref.dtype)
    nb_seq = km_ref.shape[1]
    blocks_per_tile = tm // MOBA_BLOCK

    @pl.when(j == 0)
    def _():
        km_ref[...] = jnp.zeros(km_ref.shape, km_ref.dtype)

    rows = lax.broadcasted_iota(I32, (nb_seq, aw), 0)
    km = km_ref[0]
    for bi in range(blocks_per_tile):
        mean = jnp.sum(k[bi * MOBA_BLOCK:(bi + 1) * MOBA_BLOCK], axis=0, keepdims=True) * (1.0 / MOBA_BLOCK)
        km = jnp.where(rows == j * blocks_per_tile + bi, mean, km)
    km_ref[0] = km

    v_ref[...] = proj(2 * aw, aw).astype(v_ref.dtype)

    c0 = 3 * aw
    cb = proj(c0, cwid)
    u = proj(c0 + cwid, cwid) * proj(c0 + 2 * cwid, cwid)

    @pl.when(j == 0)
    def _():
        ubuf[0:SUBLANES, :] = jnp.zeros((SUBLANES, cwid), F32)

    ubuf[SUBLANES:SUBLANES + tm, :] = u
    um1 = ubuf[SUBLANES - 1:SUBLANES - 1 + tm, :]
    um2 = ubuf[SUBLANES - 2:SUBLANES - 2 + tm, :]
    conv = cw_ref[0:1, :] * um2 + cw_ref[1:2, :] * um1 + cw_ref[2:3, :] * u
    yc_ref[...] = (cb * conv).astype(yc_ref.dtype)
    ubuf[0:SUBLANES, :] = u[tm - SUBLANES:tm, :]

    g0 = c0 + 3 * cwid
    ga_ref[...] = jax.nn.sigmoid(proj(g0, d_model))
    gc_ref[...] = jax.nn.sigmoid(proj(g0 + d_model, d_model))


def _inproj(x2, w_in_b, cos_t, sa_t, sb_t, conv_w, *, seq, aw, cwid):
    T, D = x2.shape
    tm = TM_INPROJ
    assert seq % tm == 0 and tm % MOBA_BLOCK == 0 and T % seq == 0
    tiles_per_seq = seq // tm
    nb_seq = seq // MOBA_BLOCK
    nbatch = T // seq
    row = lambda i: (i, 0)
    tab = lambda i: (i % tiles_per_seq, 0)
    kern = functools.partial(_inproj_kernel, tiles_per_seq=tiles_per_seq, aw=aw, cwid=cwid, d_model=D)
    return pl.pallas_call(
        kern,
        grid=(T // tm,),
        in_specs=[
            pl.BlockSpec((tm, D), row),
            pl.BlockSpec(w_in_b.shape, lambda i: (0, 0)),
            pl.BlockSpec((tm, LANES), tab),
            pl.BlockSpec((tm, LANES), tab),
            pl.BlockSpec((tm, LANES), tab),
            pl.BlockSpec(conv_w.shape, lambda i: (0, 0)),
        ],
        out_specs=[
            pl.BlockSpec((tm, aw), row),
            pl.BlockSpec((tm, aw), row),
            pl.BlockSpec((tm, aw), row),
            pl.BlockSpec((1, nb_seq, aw), lambda i: (i // tiles_per_seq, 0, 0)),
            pl.BlockSpec((tm, cwid), row),
            pl.BlockSpec((tm, D), row),
            pl.BlockSpec((tm, D), row),
        ],
        out_shape=[
            jax.ShapeDtypeStruct((T, aw), MXU_DTYPE),
            jax.ShapeDtypeStruct((T, aw), MXU_DTYPE),
            jax.ShapeDtypeStruct((T, aw), MXU_DTYPE),
            jax.ShapeDtypeStruct((nbatch, nb_seq, aw), F32),
            jax.ShapeDtypeStruct((T, cwid), MXU_DTYPE),
            jax.ShapeDtypeStruct((T, D), F32),
            jax.ShapeDtypeStruct((T, D), F32),
        ],
        scratch_shapes=[pltpu.VMEM((tm + SUBLANES, cwid), F32)],
        compiler_params=pltpu.CompilerParams(
            dimension_semantics=("arbitrary",), vmem_limit_bytes=VMEM_LIMIT_BYTES),
        name="inproj",
    )(x2, w_in_b, cos_t, sa_t, sb_t, conv_w)


def _attn_kernel(q_ref, k_ref, v_ref, km_ref, o_ref, vt_ref, sel_ref, s_ref, m_ref, l_ref, acc_ref):
    qb = pl.program_id(1)
    blk = MOBA_BLOCK
    nb = km_ref.shape[1]
    n_groups = q_ref.shape[2] // LANES
    hpg = HEADS_PER_LANE_GROUP
    lanes = lambda g: slice(g * LANES, (g + 1) * LANES)

    @pl.when(qb == 0)
    def _():
        for b in range(nb):
            for g in range(n_groups):
                vt_ref[b, lanes(g), :] = v_ref[0, b * blk:(b + 1) * blk, lanes(g)].astype(F32).T.astype(vt_ref.dtype)

    blk_id = lax.broadcasted_iota(I32, (nb, blk), 0).astype(F32)
    past = blk_id < qb.astype(F32)
    dim = lax.broadcasted_iota(I32, (LANES, blk), 0)

    qts = []
    for g in range(n_groups):
        q2t = q_ref[0, :, lanes(g)].astype(F32).T
        km_hi, km_lo = _split_bf16(km_ref[0, :, lanes(g)])
        for h in range(hpg):
            in_head = (dim >= h * HEAD_DIM) & (dim < (h + 1) * HEAD_DIM)
            qt = jnp.where(in_head, q2t, 0.0).astype(MXU_DTYPE)
            qts.append(qt)
            gate = jnp.where(past, _dot(km_hi, qt) + _dot(km_lo, qt), -jnp.inf)
            sel = jnp.zeros((nb, blk), F32)
            for _ in range(MOBA_TOPK):
                mx = jnp.max(gate, axis=0, keepdims=True)
                idx = jnp.min(jnp.where(gate == mx, blk_id, float(nb)), axis=0, keepdims=True)
                pick = blk_id == idx
                sel = jnp.where(pick, jnp.where(past, 1.0, sel), sel)
                gate = jnp.where(pick, -jnp.inf, gate)
            sel_ref[g * hpg + h] = sel

    def v_t(jb, hd):
        return vt_ref[jb, hd * HEAD_DIM:(hd + 1) * HEAD_DIM, :]

    def scores_to_scratch(g, kg, nk):
        for h in range(hpg):
            s_ref[g % 2, h, 0:nk * blk, :] = _dot(kg, qts[g * hpg + h])

    def run_groups(key_rows, nk, update):
        scores_to_scratch(0, k_ref[0, key_rows, lanes(0)], nk)
        for g in range(n_groups):
            if g + 1 < n_groups:
                scores_to_scratch(g + 1, k_ref[0, key_rows, lanes(g + 1)], nk)
            for h in range(hpg):
                update(g * hpg + h, s_ref[g % 2, h, 0:nk * blk, :])

    key_pos = lax.broadcasted_iota(I32, (blk, blk), 0)
    qry_pos = lax.broadcasted_iota(I32, (blk, blk), 1)

    def init(hd, s_raw):
        s = jnp.where(key_pos <= qry_pos, s_raw, NEG)
        m = jnp.max(s, axis=0, keepdims=True)
        p = jnp.exp(s - m)
        m_ref[hd] = m
        l_ref[hd] = jnp.sum(p, axis=0, keepdims=True)
        acc_ref[hd] = _dot(v_t(qb, hd), p.astype(MXU_DTYPE))

    run_groups(pl.ds(pl.multiple_of(qb * blk, blk), blk), 1, init)

    def make_step(nk):
        def step(b0):
            def update(hd, s_raw):
                parts = [jnp.where(sel_ref[hd, pl.ds(b0 + t, 1), :] > 0.5,
                                   s_raw[t * blk:(t + 1) * blk], NEG) for t in range(nk)]
                s = jnp.concatenate(parts, axis=0) if nk > 1 else parts[0]
                m = m_ref[hd]
                m_new = jnp.maximum(m, jnp.max(s, axis=0, keepdims=True))
                alpha = jnp.exp(m - m_new)
                p = jnp.exp(s - m_new)
                vt = [v_t(b0 + t, hd) for t in range(nk)]
                vt = jnp.concatenate(vt, axis=1) if nk > 1 else vt[0]
                m_ref[hd] = m_new
                l_ref[hd] = alpha * l_ref[hd] + jnp.sum(p, axis=0, keepdims=True)
                acc_ref[hd] = alpha * acc_ref[hd] + _dot(vt, p.astype(MXU_DTYPE))

            run_groups(pl.ds(pl.multiple_of(b0 * blk, blk), nk * blk), nk, update)
        return step

    quad, pair, single = make_step(4), make_step(2), make_step(1)
    n_quads = lax.shift_right_logical(qb, 2)
    lax.fori_loop(0, n_quads, lambda i, c: (quad(4 * i), c)[1], 0)
    lax.fori_loop(0, lax.shift_right_logical(qb, 1) & 1, lambda i, c: (pair(4 * n_quads), c)[1], 0)
    lax.fori_loop(0, qb & 1, lambda i, c: (single(qb - 1), c)[1], 0)

    for g in range(n_groups):
        outs = [acc_ref[g * hpg + h] / l_ref[g * hpg + h] for h in range(hpg)]
        o_ref[0, :, lanes(g)] = jnp.concatenate(outs, axis=0).T.astype(o_ref.dtype)


def _attention(q, k, v, km, *, nbatch, seq, aw):
    blk = MOBA_BLOCK
    nb = seq // blk
    n_heads = aw // HEAD_DIM
    max_nk = 4
    assert nb % SUBLANES == 0
    q3, k3, v3 = (t.reshape(nbatch, seq, aw) for t in (q, k, v))
    out = pl.pallas_call(
        _attn_kernel,
        grid=(nbatch, nb),
        in_specs=[
            pl.BlockSpec((1, blk, aw), lambda b, i: (b, i, 0)),
            pl.BlockSpec((1, seq, aw), lambda b, i: (b, 0, 0)),
            pl.BlockSpec((1, seq, aw), lambda b, i: (b, 0, 0)),
            pl.BlockSpec((1, nb, aw), lambda b, i: (b, 0, 0)),
        ],
        out_specs=pl.BlockSpec((1, blk, aw), lambda b, i: (b, i, 0)),
        out_shape=jax.ShapeDtypeStruct((nbatch, seq, aw), MXU_DTYPE),
        scratch_shapes=[
            pltpu.VMEM((nb, aw, blk), MXU_DTYPE),
            pltpu.VMEM((n_heads, nb, blk), F32),
            pltpu.VMEM((2, HEADS_PER_LANE_GROUP, max_nk * blk, blk), F32),
            pltpu.VMEM((n_heads, 1, blk), F32),
            pltpu.VMEM((n_heads, 1, blk), F32),
            pltpu.VMEM((n_heads, HEAD_DIM, blk), F32),
        ],
        compiler_params=pltpu.CompilerParams(
            dimension_semantics=("arbitrary", "arbitrary"), vmem_limit_bytes=VMEM_LIMIT_BYTES),
        name="moba_attn",
    )(q3, k3, v3, km)
    return out.reshape(nbatch * seq, aw)


def _post_kernel(attn_ref, yc_ref, ga_ref, gc_ref, x_ref, wao_ref, wco_ref, wo_ref,
                 g1_ref, b1_ref, wrh_ref, wrl_ref, rb_ref,
                 h1_ref, h1w_ref, ti_ref, tw_ref, rk_ref, cnt_ref, cnt_acc, *, alpha):
    i = pl.program_id(0)
    tm = x_ref.shape[0]
    ne = wrh_ref.shape[0]

    y_attn = _dot(attn_ref[...], wao_ref[...])
    y_conv = _dot(yc_ref[...], wco_ref[...])
    merged = ga_ref[...] * y_attn + gc_ref[...] * y_conv
    mix = _dot(merged.astype(MXU_DTYPE), wo_ref[...])
    h1 = _layer_norm(alpha * x_ref[...] + mix, g1_ref[...], b1_ref[...])
    h1_ref[...] = h1
    _tt_store(h1w_ref, _pack_pairs(h1))

    h_hi, h_lo = _split_bf16(h1)
    logits = _dot_nt(wrh_ref[...], h_hi) + _dot_nt(wrh_ref[...], h_lo) + _dot_nt(wrl_ref[...], h_hi)
    scores = jax.nn.sigmoid(logits)
    choice = scores + rb_ref[...]

    gsz = ne // N_GROUPS
    gshape = (N_GROUPS, gsz, tm)
    c3 = choice.reshape(gshape)
    in_grp = lax.broadcasted_iota(I32, gshape, 1).astype(F32)
    m1 = jnp.max(c3, axis=1, keepdims=True)
    i1 = jnp.min(jnp.where(c3 == m1, in_grp, float(gsz)), axis=1, keepdims=True)
    m2 = jnp.max(jnp.where(in_grp == i1, -jnp.inf, c3), axis=1, keepdims=True)
    gscore = jnp.broadcast_to(m1 + m2, gshape).reshape(ne, tm)

    eid = lax.broadcasted_iota(I32, (ne, tm), 0).astype(F32)
    gid = lax.broadcasted_iota(I32, gshape, 0).astype(F32).reshape(ne, tm)
    cand = jnp.full((ne, tm), -jnp.inf, F32)
    for _ in range(TOPK_GROUPS):
        mx = jnp.max(gscore, axis=0, keepdims=True)
        idx = jnp.min(jnp.where(gscore == mx, gid, float(N_GROUPS)), axis=0, keepdims=True)
        pick = gid == idx
        cand = jnp.where(pick, choice, cand)
        gscore = jnp.where(pick, -jnp.inf, gscore)

    selmat = jnp.zeros((ne, tm), F32)
    idxs, svals = [], []
    for _ in range(TOP_K):
        mx = jnp.max(cand, axis=0, keepdims=True)
        idx = jnp.min(jnp.where(cand == mx, eid, float(ne)), axis=0, keepdims=True)
        pick = eid == idx
        svals.append(jnp.sum(jnp.where(pick, scores, 0.0), axis=0, keepdims=True))
        idxs.append(idx)
        selmat = jnp.where(pick, 1.0, selmat)
        cand = jnp.where(pick, -jnp.inf, cand)
    ssum = svals[0]
    for r in range(1, TOP_K):
        ssum = ssum + svals[r]

    @pl.when(i == 0)
    def _():
        cnt_acc[...] = jnp.zeros(cnt_acc.shape, F32)

    tr = lax.broadcasted_iota(I32, (tm, tm), 0)
    tc = lax.broadcasted_iota(I32, (tm, tm), 1)
    upper = jnp.where(tr < tc, 1.0, 0.0).astype(MXU_DTYPE)
    selb = selmat.astype(MXU_DTYPE)
    base = cnt_acc[...]
    rank = _dot(selb, upper) + jnp.concatenate([base] * (tm // LANES), axis=1)
    new_cnt = base + _dot(selb, jnp.ones((tm, LANES), MXU_DTYPE))
    cnt_acc[...] = new_cnt
    cnt_ref[...] = new_cnt

    for r in range(TOP_K):
        pick = eid == idxs[r]
        ti_ref[r:r + 1, :] = idxs[r].astype(I32)
        tw_ref[r:r + 1, :] = svals[r] / ssum * ROUTED_SCALE
        rk_ref[r:r + 1, :] = jnp.sum(jnp.where(pick, rank, 0.0), axis=0, keepdims=True).astype(I32)


def _post(attn, yc, ga, gc, x2, wao, wco, wo, g1, b1, wrh, wrl, rb, *, alpha):
    T, D = x2.shape
    tm = TM_POST
    assert D == 2 * TOK_ROWS * LANES
    aw, cwid = attn.shape[1], yc.shape[1]
    ne = wrh.shape[0]
    row = lambda i: (i, 0)
    full = lambda i: (0, 0)
    col = lambda i: (0, i)
    return pl.pallas_call(
        functools.partial(_post_kernel, alpha=alpha),
        grid=(T // tm,),
        in_specs=[
            pl.BlockSpec((tm, aw), row), pl.BlockSpec((tm, cwid), row),
            pl.BlockSpec((tm, D), row), pl.BlockSpec((tm, D), row), pl.BlockSpec((tm, D), row),
            pl.BlockSpec(wao.shape, full), pl.BlockSpec(wco.shape, full), pl.BlockSpec(wo.shape, full),
            pl.BlockSpec(g1.shape, full), pl.BlockSpec(b1.shape, full),
            pl.BlockSpec(wrh.shape, full), pl.BlockSpec(wrl.shape, full), pl.BlockSpec(rb.shape, full),
        ],
        out_specs=[
            pl.BlockSpec((tm, D), row),
            pl.BlockSpec((tm * TOK_ROWS, LANES), row),
            pl.BlockSpec((TOP_K, tm), col), pl.BlockSpec((TOP_K, tm), col), pl.BlockSpec((TOP_K, tm), col),
            pl.BlockSpec((ne, LANES), full),
        ],
        out_shape=[
            jax.ShapeDtypeStruct((T, D), F32),
            jax.ShapeDtypeStruct((T * TOK_ROWS, LANES), U32),
            jax.ShapeDtypeStruct((TOP_K, T), I32),
            jax.ShapeDtypeStruct((TOP_K, T), F32),
            jax.ShapeDtypeStruct((TOP_K, T), I32),
            jax.ShapeDtypeStruct((ne, LANES), F32),
        ],
        scratch_shapes=[pltpu.VMEM((ne, LANES), F32)],
        compiler_params=pltpu.CompilerParams(
            dimension_semantics=("arbitrary",), vmem_limit_bytes=VMEM_LIMIT_BYTES),
        name="post_route",
    )(attn, yc, ga, gc, x2, wao, wco, wo, g1, b1, wrh, wrl, rb)


def _slots_kernel(pstart_ref, ti_ref, rk_ref, dest_ref):
    ti = ti_ref[...]
    start = lax.fori_loop(0, N_EXPERTS, lambda e, acc: jnp.where(ti == e, pstart_ref[e], acc),
                          jnp.zeros(ti.shape, I32))
    dest_ref[...] = start + rk_ref[...]


def _slots(pstart, topi, rnk):
    T = topi.shape[1]
    tc = min(T, 2048)
    blk = pl.BlockSpec((TOP_K, tc), lambda i, *_: (0, i))
    return pl.pallas_call(
        _slots_kernel,
        grid_spec=pltpu.PrefetchScalarGridSpec(
            num_scalar_prefetch=1, grid=(T // tc,), in_specs=[blk, blk], out_specs=blk),
        out_shape=jax.ShapeDtypeStruct((TOP_K, T), I32),
        compiler_params=pltpu.CompilerParams(dimension_semantics=("arbitrary",)),
        name="slots",
    )(pstart, topi, rnk)


def _dispatch_kernel(zrow_ref, nzero_ref, dest_ref, h1_ref, xs_hbm, zbuf, zsem, sem):
    i = pl.program_id(0)
    tm = dest_ref.shape[1]
    zrows = zbuf.shape[0]

    def zero_copy(z):
        start = pl.multiple_of(zrow_ref[z] * TOK_ROWS, zrows)
        return pltpu.make_async_copy(zbuf, xs_hbm.at[pl.ds(start, zrows)], zsem)

    @pl.when(i == 0)
    def _():
        zbuf[...] = jnp.zeros(zbuf.shape, zbuf.dtype)
        nz = nzero_ref[0]
        lax.fori_loop(0, nz, lambda z, c: (zero_copy(z).start(), c)[1], 0)
        lax.fori_loop(0, nz, lambda z, c: (zero_copy(z).wait(), c)[1], 0)

    def copy(k, j):
        return _tile_copy(h1_ref, xs_hbm, j, dest_ref[k, j], sem)

    def issue(j, c):
        for k in range(TOP_K):
            copy(k, j).start(priority=k % 2)
        return c

    def drain(j, c):
        for k in range(TOP_K):
            copy(k, j).wait()
        return c

    lax.fori_loop(0, tm, issue, 0)
    lax.fori_loop(0, tm, drain, 0)


def _dispatch(zrow, nzero, dest, h1, *, n_rows):
    T = h1.shape[0] // TOK_ROWS
    tm = TM_DISPATCH
    return pl.pallas_call(
        _dispatch_kernel,
        grid_spec=pltpu.PrefetchScalarGridSpec(
            num_scalar_prefetch=2,
            grid=(T // tm,),
            in_specs=[pl.BlockSpec((TOP_K, tm), lambda i, *_: (0, i), memory_space=pltpu.SMEM),
                      pl.BlockSpec((tm * TOK_ROWS, LANES), lambda i, *_: (i, 0))],
            out_specs=pl.BlockSpec(memory_space=pl.ANY),
            scratch_shapes=[pltpu.VMEM((TM_EXPERT // 2 * TOK_ROWS, LANES), h1.dtype),
                            pltpu.SemaphoreType.DMA, pltpu.SemaphoreType.DMA],
        ),
        out_shape=jax.ShapeDtypeStruct((n_rows * TOK_ROWS, LANES), h1.dtype),
        compiler_params=pltpu.CompilerParams(
            dimension_semantics=("arbitrary",), vmem_limit_bytes=VMEM_LIMIT_BYTES),
        name="dispatch",
    )(zrow, nzero, dest, h1)


def _experts_kernel(be_ref, nused_ref, bvalid_ref, first_ref, eslot_ref, nexte_ref,
                    xs_ref, wg_hbm, wu_hbm, wd_hbm, ys_ref, wg_f, wu_f, wd_f, wgu_b, wd_b, wsem):
    i = pl.program_id(0)
    ed = wg_hbm.shape[2]
    te = xs_ref.shape[0] // TOK_ROWS

    def weight_copies(e, slot):
        return [pltpu.make_async_copy(src.at[e], dst.at[slot], wsem.at[slot])
                for src, dst in ((wg_hbm, wg_f), (wu_hbm, wu_f), (wd_hbm, wd_f))]

    def swiglu_rows(n_tok):
        x_lo, x_hi = _unpack_pairs(_tt_load(xs_ref, n_tok))
        xb = jnp.concatenate([x_lo.astype(MXU_DTYPE), x_hi.astype(MXU_DTYPE)], axis=1)
        gu = _dot(xb, wgu_b[...])
        hb = _silu(gu[:, 0:ed]) * gu[:, ed:2 * ed]
        _tt_store(ys_ref, _pack_pairs(_dot(hb.astype(MXU_DTYPE), wd_b[...])))

    @pl.when(i < nused_ref[0])
    def _():
        @pl.when(first_ref[i] == 1)
        def _():
            slot = eslot_ref[i]

            @pl.when(i == 0)
            def _():
                for c in weight_copies(be_ref[i], slot):
                    c.start()

            for c in weight_copies(be_ref[i], slot):
                c.wait()
            wgu_b[:, 0:ed] = wg_f[slot].astype(MXU_DTYPE)
            wgu_b[:, ed:2 * ed] = wu_f[slot].astype(MXU_DTYPE)
            wd_b[...] = wd_f[slot].astype(MXU_DTYPE)

            @pl.when(nexte_ref[i] >= 0)
            def _():
                for c in weight_copies(nexte_ref[i], 1 - slot):
                    c.start()

        nvalid = bvalid_ref[i]

        @pl.when(nvalid > te // 2)
        def _():
            swiglu_rows(te)

        @pl.when(nvalid <= te // 2)
        def _():
            swiglu_rows(te // 2)
            rest = ys_ref.shape[0] // 2
            ys_ref[rest:, :] = jnp.zeros((rest, ys_ref.shape[1]), ys_ref.dtype)


def _experts(sched, xs, wg, wu, wd):
    te = TM_EXPERT
    D, ed = wg.shape[1], wg.shape[2]
    nblk = xs.shape[0] // (te * TOK_ROWS)
    blk_map = lambda i, be, nu, *_: (jnp.minimum(i, nu[0] - 1), 0)
    hbm = pl.BlockSpec(memory_space=pl.ANY)
    return pl.pallas_call(
        _experts_kernel,
        grid_spec=pltpu.PrefetchScalarGridSpec(
            num_scalar_prefetch=len(sched),
            grid=(nblk,),
            in_specs=[pl.BlockSpec((te * TOK_ROWS, LANES), blk_map), hbm, hbm, hbm],
            out_specs=pl.BlockSpec((te * TOK_ROWS, LANES), blk_map),
            scratch_shapes=[
                pltpu.VMEM((2, D, ed), wg.dtype), pltpu.VMEM((2, D, ed), wu.dtype),
                pltpu.VMEM((2, ed, D), wd.dtype),
                pltpu.VMEM((D, 2 * ed), MXU_DTYPE), pltpu.VMEM((ed, D), MXU_DTYPE),
                pltpu.SemaphoreType.DMA((2,)),
            ],
        ),
        out_shape=jax.ShapeDtypeStruct(xs.shape, xs.dtype),
        compiler_params=pltpu.CompilerParams(
            dimension_semantics=("arbitrary",), vmem_limit_bytes=VMEM_LIMIT_BYTES),
        name="experts",
    )(*sched, xs, wg, wu, wd)


def _final_kernel(dest_ref, dest_next_ref, tw_ref, h1_ref, p_ref, ys_hbm, wsgu_ref, wsd_ref, wpg_ref,
                  wpp_ref, g2_ref, b2_ref, g3_ref, b3_ref, o_ref, gbuf_a, gbuf_b, routed_ref, sems, *, alpha):
    i = pl.program_id(0)
    n = pl.num_programs(0)
    tm = o_ref.shape[0]
    sd = wsd_ref.shape[0]
    half = routed_ref.shape[1] // 2
    bufs = (gbuf_a, gbuf_b)

    def copy(idx_ref, slot, k, j):
        return _tile_copy(ys_hbm, bufs[slot].at[k], idx_ref[k, j], j, sems.at[slot])

    def gather_loop(idx_ref, slot, wait):
        def body(j, c):
            for k in range(TOP_K):
                if wait:
                    copy(idx_ref, slot, k, j).wait()
                else:
                    copy(idx_ref, slot, k, j).start(priority=k % 2)
            return c
        lax.fori_loop(0, tm, body, 0)

    @pl.when(i == 0)
    def _():
        gather_loop(dest_ref, 0, wait=False)

    def tile(cur):
        nxt = 1 - cur
        gbuf = bufs[cur]
        h1 = h1_ref[...]
        gu = _dot(h1.astype(MXU_DTYPE), wsgu_ref[...])
        shared = _dot((_silu(gu[:, 0:sd]) * gu[:, sd:2 * sd]).astype(MXU_DTYPE), wsd_ref[...])

        gather_loop(dest_ref, cur, wait=True)

        tw = tw_ref[...]
        wt = jnp.concatenate([tw, jnp.zeros((LANES - TOP_K, tm), F32)], axis=0).T
        for r in range(tm // SUBLANES):
            for j in range(r * SUBLANES, (r + 1) * SUBLANES):
                for k in range(TOP_K):
                    copy(dest_next_ref, nxt, k, j).start(priority=k % 2)
            wr = wt[r * SUBLANES:(r + 1) * SUBLANES, :]
            wb = [jnp.broadcast_to(wr[:, k:k + 1], (SUBLANES, LANES)) for k in range(TOP_K)]
            for s in range(TOK_ROWS):
                rows = pl.ds(r * SUBLANES * TOK_ROWS + s, SUBLANES, stride=TOK_ROWS)
                acc_lo = acc_hi = None
                for k in range(TOP_K):
                    y_lo, y_hi = _unpack_pairs(gbuf[k, rows, :])
                    acc_lo = wb[k] * y_lo if k == 0 else acc_lo + wb[k] * y_lo
                    acc_hi = wb[k] * y_hi if k == 0 else acc_hi + wb[k] * y_hi
                tok = slice(r * SUBLANES, (r + 1) * SUBLANES)
                routed_ref[tok, s * LANES:(s + 1) * LANES] = acc_lo
                routed_ref[tok, half + s * LANES:half + (s + 1) * LANES] = acc_hi
        routed = routed_ref[...]

        h2 = _layer_norm(alpha * h1 + (routed + shared), g2_ref[...], b2_ref[...])
        gate = jax.nn.sigmoid(_dot(h2.astype(MXU_DTYPE), wpg_ref[...]))
        ple = gate * _dot(p_ref[...].astype(MXU_DTYPE), wpp_ref[...])
        o_ref[...] = _layer_norm(alpha * h2 + ple, g3_ref[...], b3_ref[...])

        @pl.when(i == n - 1)
        def _():
            gather_loop(dest_next_ref, nxt, wait=True)

    for parity in range(2):
        pl.when(i % 2 == parity)(functools.partial(tile, parity))


def _final(dest, topw, h1, p2, ys, wsgu, wsd, wpg, wpp, g2, b2, g3, b3, *, alpha):
    T, D = p2.shape[0], wpg.shape[0]
    tm = TM_FINAL
    n_tiles = T // tm
    row = lambda i: (i, 0)
    full = lambda i: (0, 0)
    col = lambda i: (0, i)
    col_next = lambda i: (0, jnp.minimum(i + 1, n_tiles - 1))
    return pl.pallas_call(
        functools.partial(_final_kernel, alpha=alpha),
        grid=(n_tiles,),
        in_specs=[
            pl.BlockSpec((TOP_K, tm), col, memory_space=pltpu.SMEM),
            pl.BlockSpec((TOP_K, tm), col_next, memory_space=pltpu.SMEM),
            pl.BlockSpec((TOP_K, tm), col),
            pl.BlockSpec((tm, D), row),
            pl.BlockSpec((tm, p2.shape[1]), row),
            pl.BlockSpec(memory_space=pl.ANY),
            pl.BlockSpec(wsgu.shape, full), pl.BlockSpec(wsd.shape, full),
            pl.BlockSpec(wpg.shape, full), pl.BlockSpec(wpp.shape, full),
            pl.BlockSpec(g2.shape, full), pl.BlockSpec(b2.shape, full),
            pl.BlockSpec(g3.shape, full), pl.BlockSpec(b3.shape, full),
        ],
        out_specs=pl.BlockSpec((tm, D), row),
        out_shape=jax.ShapeDtypeStruct((T, D), F32),
        scratch_shapes=[pltpu.VMEM((TOP_K, tm * TOK_ROWS, LANES), ys.dtype),
                        pltpu.VMEM((TOP_K, tm * TOK_ROWS, LANES), ys.dtype),
                        pltpu.VMEM((tm, D), F32), pltpu.SemaphoreType.DMA((2,))],
        compiler_params=pltpu.CompilerParams(
            dimension_semantics=("arbitrary",), vmem_limit_bytes=VMEM_LIMIT_BYTES),
        name="final",
    )(dest, dest, topw, h1, p2, ys, wsgu, wsd, wpg, wpp, g2, b2, g3, b3)


def _rope_tables(seq):
    half = ROT_DIM // 2
    f32 = np.float32
    inv_freq = f32(ROPE_THETA) ** (-np.arange(0, ROT_DIM, 2, dtype=f32) / f32(ROT_DIM))
    ang = np.arange(seq, dtype=np.int32).astype(f32)[:, None] * inv_freq[None, :]
    cos, sin = np.cos(ang), np.sin(ang)
    ones = np.ones((seq, HEAD_DIM - ROT_DIM), f32)
    zeros = np.zeros((seq, HEAD_DIM - ROT_DIM), f32)
    zh = np.zeros((seq, half), f32)
    cos_h = np.concatenate([cos, cos, ones], axis=1)
    sa_h = np.concatenate([-sin, zh, zeros], axis=1)
    sb_h = np.concatenate([zh, sin, zeros], axis=1)
    rep = lambda t: jnp.asarray(np.concatenate([t] * HEADS_PER_LANE_GROUP, axis=1), F32)
    return rep(cos_h), rep(sa_h), rep(sb_h)


def _expert_layout(counts, n_blocks):
    te = TM_EXPERT
    nblk_e = (counts + te - 1) // te
    blk_end = jnp.cumsum(nblk_e)
    blk_start = blk_end - nblk_e
    nused = blk_end[-1]
    pstart = (blk_start * te).astype(I32)
    bid = jnp.arange(n_blocks, dtype=I32)
    block_e = jnp.sum(blk_end[None, :] <= jnp.minimum(bid, nused - 1)[:, None], axis=1).astype(I32)
    block_e = jnp.minimum(block_e, N_EXPERTS - 1)
    bvalid = jnp.clip(counts[block_e] - (bid - blk_start[block_e]) * te, 0, te).astype(I32)
    first = ((bid == blk_start[block_e]) & (bid < nused)).astype(I32)
    eslot = ((jnp.cumsum(first) - 1) & 1).astype(I32)
    after = blk_end[block_e]
    nexte = jnp.where(after < nused, block_e[jnp.minimum(after, n_blocks - 1)], -1).astype(I32)
    half = te // 2
    partial = (counts % half) != 0
    order = jnp.argsort(jnp.logical_not(partial), stable=True).astype(I32)
    nzero = jnp.sum(partial).astype(I32)
    zrow = (pstart + (counts // half) * half)[order].astype(I32)
    sched = (block_e, nused.astype(I32).reshape(1), bvalid, first, eslot, nexte)
    return pstart, sched, zrow, nzero.reshape(1)


def _layer(h, p2, w_in, conv_w, w_attn_out, w_conv_out, w_out, ln1_g, ln1_b, w_router, router_bias,
           w_exp_gate, w_exp_up, w_exp_down, w_sh_gate, w_sh_up, w_sh_down, ln2_g, ln2_b,
           w_ple_gate, w_ple_proj, ln3_g, ln3_b, *, nbatch, seq, alpha):
    T, D = h.shape
    aw = w_attn_out.shape[0]
    cwid = w_conv_out.shape[0]
    bf = lambda w: w.astype(MXU_DTYPE)
    rowv = lambda g: g.reshape(1, -1)

    cos_t, sa_t, sb_t = _rope_tables(seq)
    q, k, v, km, yc, ga, gc = _inproj(h, bf(w_in), cos_t, sa_t, sb_t, conv_w, seq=seq, aw=aw, cwid=cwid)
    attn = _attention(q, k, v, km, nbatch=nbatch, seq=seq, aw=aw)

    wr_t = w_router.astype(F32).T
    wrh = wr_t.astype(MXU_DTYPE)
    wrl = (wr_t - wrh.astype(F32)).astype(MXU_DTYPE)
    h1, h1w, topi, topw, rnk, cnt = _post(
        attn, yc, ga, gc, h, bf(w_attn_out), bf(w_conv_out), bf(w_out), rowv(ln1_g), rowv(ln1_b),
        wrh, wrl, router_bias.astype(F32).reshape(-1, 1), alpha=alpha)

    n_blocks = (T * TOP_K) // TM_EXPERT + N_EXPERTS
    counts = cnt[:, 0].astype(I32)
    pstart, sched, zrow, nzero = _expert_layout(counts, n_blocks)
    dest = _slots(pstart, topi, rnk)
    xs = _dispatch(zrow, nzero, dest, h1w, n_rows=n_blocks * TM_EXPERT)
    ys = _experts(sched, xs, w_exp_gate, w_exp_up, w_exp_down)

    wsgu = jnp.concatenate([bf(w_sh_gate), bf(w_sh_up)], axis=1)
    return _final(dest, topw, h1, p2, ys, wsgu, bf(w_sh_down), bf(w_ple_gate), bf(w_ple_proj),
                  rowv(ln2_g), rowv(ln2_b), rowv(ln3_g), rowv(ln3_b), alpha=alpha)


def kernel(x, p, w_in, conv_w, w_attn_out, w_conv_out, w_out, ln1_g, ln1_b, w_router, router_bias,
           w_exp_gate, w_exp_up, w_exp_down, w_sh_gate, w_sh_up, w_sh_down, ln2_g, ln2_b,
           w_ple_gate, w_ple_proj, ln3_g, ln3_b):
    nbatch, seq, d_model = x.shape
    depth = w_in.shape[0]
    alpha = (2 * depth) ** 0.25
    assert seq % MOBA_BLOCK == 0
    h = x.reshape(nbatch * seq, d_model)
    for i in range(depth):
        h = _layer(h, p[i].reshape(nbatch * seq, -1), w_in[i], conv_w[i], w_attn_out[i], w_conv_out[i],
                   w_out[i], ln1_g[i], ln1_b[i], w_router[i], router_bias[i],
                   w_exp_gate[i], w_exp_up[i], w_exp_down[i], w_sh_gate[i], w_sh_up[i], w_sh_down[i],
                   ln2_g[i], ln2_b[i], w_ple_gate[i], w_ple_proj[i], ln3_g[i], ln3_b[i],
                   nbatch=nbatch, seq=seq, alpha=alpha)
    return h.reshape(nbatch, seq, d_model)
```

```python
import functools

import jax
import jax.numpy as jnp
import numpy as np
from jax import lax
from jax.experimental import pallas as pl
from jax.experimental.pallas import tpu as pltpu

N_HEADS = 8
HEAD_DIM = 64
ROT_DIM = 16
ROPE_THETA = 500000.0
MOBA_BLOCK = 256
MOBA_TOPK = 3
CONV_K = 3
N_EXPERTS = 256
TOP_K = 8
N_GROUPS = 8
TOPK_GROUPS = 4
ROUTED_SCALE = 2.5
LN_EPS = 1e-5
NEG = -1e30
LOG2_E = 1.4426950408889634

LANES = 128
SUBLANES = 8
HEADS_PER_LANE_GROUP = LANES // HEAD_DIM
VMEM_LIMIT_BYTES = 56 * 1024 * 1024

MXU_DTYPE = jnp.bfloat16
F32 = jnp.float32
I32 = jnp.int32
U32 = jnp.uint32

TM_INPROJ = 512
TM_POST = 512
TM_DISPATCH = 512
TM_EXPERT = 512
TM_FINAL = 256
TC_SLOTS = 2048


def _dot(a, b):
    return jnp.dot(a, b, preferred_element_type=F32)


def _dot_nt(a, b):
    return lax.dot_general(a, b, (((1,), (1,)), ((), ())), preferred_element_type=F32)


def _split_bf16(a):
    hi = a.astype(MXU_DTYPE)
    lo = (a - hi.astype(F32)).astype(MXU_DTYPE)
    return hi, lo


TOK_ROWS = 4
HI_MASK = 0xFFFF0000


def _pack_pairs(f):
    half = f.shape[1] // 2
    bits = lambda t: lax.bitcast_convert_type(t.astype(jnp.bfloat16).astype(F32), U32)
    return (bits(f[:, half:]) & jnp.uint32(HI_MASK)) | lax.shift_right_logical(bits(f[:, :half]), jnp.uint32(16))


def _unpack_pairs(w):
    lo = lax.bitcast_convert_type(lax.shift_left(w, jnp.uint32(16)), F32)
    hi = lax.bitcast_convert_type(w & jnp.uint32(HI_MASK), F32)
    return lo, hi


def _tt_load(ref, n_tok):
    chunks = [ref[pl.ds(s, n_tok, stride=TOK_ROWS), :] for s in range(TOK_ROWS)]
    return jnp.concatenate(chunks, axis=1)


def _tt_store(ref, words):
    n_tok = words.shape[0]
    for s in range(TOK_ROWS):
        ref[pl.ds(s, n_tok, stride=TOK_ROWS), :] = words[:, s * LANES:(s + 1) * LANES]


def _tile_copy(src, dst, src_tok, dst_tok, sem):
    rows = lambda t: pl.ds(pl.multiple_of(t * TOK_ROWS, TOK_ROWS), TOK_ROWS)
    return pltpu.make_async_copy(src.at[rows(src_tok)], dst.at[rows(dst_tok)], sem)


def _layer_norm(t, g, b):
    mu = jnp.mean(t, axis=-1, keepdims=True)
    d = t - mu
    var = jnp.mean(d * d, axis=-1, keepdims=True)
    return d * lax.rsqrt(var + LN_EPS) * g + b


def _silu(t):
    return t * jax.nn.sigmoid(t)


def _inproj_kernel(x_ref, w_ref, cos_ref, sa_ref, sb_ref, cw_ref,
                   q_ref, k_ref, v_ref, km_ref, yc_ref, ga_ref, gc_ref, ubuf,
                   *, tiles_per_seq, aw, cwid, d_model):
    tm = x_ref.shape[0]
    j = pl.program_id(0) % tiles_per_seq
    xb = x_ref[...].astype(MXU_DTYPE)
    cos, sa, sb = cos_ref[...], sa_ref[...], sb_ref[...]

    def proj(c0, n):
        return _dot(xb, w_ref[:, c0:c0 + n])

    def rope(z):
        parts = []
        for g in range(aw // LANES):
            zg = z[:, g * LANES:(g + 1) * LANES]
            parts.append(zg * cos
                         + pltpu.roll(zg, LANES - ROT_DIM // 2, 1) * sa
                         + pltpu.roll(zg, ROT_DIM // 2, 1) * sb)
        return jnp.concatenate(parts, axis=1)

    q_ref[...] = (rope(proj(0, aw)) * (HEAD_DIM ** -0.5 * LOG2_E)).astype(q_ref.dtype)

    k = rope(proj(aw, aw))
    k_ref[...] = k.astype(k_ref.dtype)
    nb_seq = km_ref.shape[1]
    blocks_per_tile = tm // MOBA_BLOCK

    @pl.when(j == 0)
    def _():
        km_ref[...] = jnp.zeros(km_ref.shape, km_ref.dtype)

    rows = lax.broadcasted_iota(I32, (nb_seq, aw), 0)
    km = km_ref[0]
    for bi in range(blocks_per_tile):
        mean = jnp.sum(k[bi * MOBA_BLOCK:(bi + 1) * MOBA_BLOCK], axis=0, keepdims=True) * (1.0 / MOBA_BLOCK)
        km = jnp.where(rows == j * blocks_per_tile + bi, mean, km)
    km_ref[0] = km

    v_ref[...] = proj(2 * aw, aw).astype(v_ref.dtype)

    c0 = 3 * aw
    cb = proj(c0, cwid)
    u = proj(c0 + cwid, cwid) * proj(c0 + 2 * cwid, cwid)

    @pl.when(j == 0)
    def _():
        ubuf[0:SUBLANES, :] = jnp.zeros((SUBLANES, cwid), F32)

    ubuf[SUBLANES:SUBLANES + tm, :] = u
    um1 = ubuf[SUBLANES - 1:SUBLANES - 1 + tm, :]
    um2 = ubuf[SUBLANES - 2:SUBLANES - 2 + tm, :]
    conv = cw_ref[0:1, :] * um2 + cw_ref[1:2, :] * um1 + cw_ref[2:3, :] * u
    yc_ref[...] = (cb * conv).astype(yc_ref.dtype)
    ubuf[0:SUBLANES, :] = u[tm - SUBLANES:tm, :]

    g0 = c0 + 3 * cwid
    ga_ref[...] = jax.nn.sigmoid(proj(g0, d_model))
    gc_ref[...] = jax.nn.sigmoid(proj(g0 + d_model, d_model))


def _inproj(x2, w_in_b, cos_t, sa_t, sb_t, conv_w, *, seq, aw, cwid):
    T, D = x2.shape
    tm = TM_INPROJ
    assert seq % tm == 0 and tm % MOBA_BLOCK == 0 and T % seq == 0
    tiles_per_seq = seq // tm
    nb_seq = seq // MOBA_BLOCK
    nbatch = T // seq
    row = lambda i: (i, 0)
    tab = lambda i: (i % tiles_per_seq, 0)
    kern = functools.partial(_inproj_kernel, tiles_per_seq=tiles_per_seq, aw=aw, cwid=cwid, d_model=D)
    return pl.pallas_call(
        kern,
        grid=(T // tm,),
        in_specs=[
            pl.BlockSpec((tm, D), row),
            pl.BlockSpec(w_in_b.shape, lambda i: (0, 0)),
            pl.BlockSpec((tm, LANES), tab),
            pl.BlockSpec((tm, LANES), tab),
            pl.BlockSpec((tm, LANES), tab),
            pl.BlockSpec(conv_w.shape, lambda i: (0, 0)),
        ],
        out_specs=[
            pl.BlockSpec((tm, aw), row),
            pl.BlockSpec((tm, aw), row),
            pl.BlockSpec((tm, aw), row),
            pl.BlockSpec((1, nb_seq, aw), lambda i: (i // tiles_per_seq, 0, 0)),
            pl.BlockSpec((tm, cwid), row),
            pl.BlockSpec((tm, D), row),
            pl.BlockSpec((tm, D), row),
        ],
        out_shape=[
            jax.ShapeDtypeStruct((T, aw), MXU_DTYPE),
            jax.ShapeDtypeStruct((T, aw), MXU_DTYPE),
            jax.ShapeDtypeStruct((T, aw), MXU_DTYPE),
            jax.ShapeDtypeStruct((nbatch, nb_seq, aw), F32),
            jax.ShapeDtypeStruct((T, cwid), MXU_DTYPE),
            jax.ShapeDtypeStruct((T, D), F32),
            jax.ShapeDtypeStruct((T, D), F32),
        ],
        scratch_shapes=[pltpu.VMEM((tm + SUBLANES, cwid), F32)],
        compiler_params=pltpu.CompilerParams(
            dimension_semantics=("arbitrary",), vmem_limit_bytes=VMEM_LIMIT_BYTES),
        name="inproj",
    )(x2, w_in_b, cos_t, sa_t, sb_t, conv_w)


def _attn_kernel(q_ref, k_ref, v_ref, km_ref, o_ref, vt_ref, sel_ref, s_ref, m_ref, l_ref, acc_ref):
    qb = pl.program_id(1)
    blk = MOBA_BLOCK
    nb = km_ref.shape[1]
    n_groups = q_ref.shape[2] // LANES
    hpg = HEADS_PER_LANE_GROUP
    lanes = lambda g: slice(g * LANES, (g + 1) * LANES)

    @pl.when(qb == 0)
    def _():
        for b in range(nb):
            for g in range(n_groups):
                vt_ref[b, lanes(g), :] = v_ref[0, b * blk:(b + 1) * blk, lanes(g)].astype(F32).T.astype(vt_ref.dtype)

    blk_id = lax.broadcasted_iota(I32, (nb, blk), 0).astype(F32)
    past = blk_id < qb.astype(F32)
    dim = lax.broadcasted_iota(I32, (LANES, blk), 0)

    qts = []
    for g in range(n_groups):
        q2t = q_ref[0, :, lanes(g)].astype(F32).T
        km_hi, km_lo = _split_bf16(km_ref[0, :, lanes(g)])
        for h in range(hpg):
            in_head = (dim >= h * HEAD_DIM) & (dim < (h + 1) * HEAD_DIM)
            qt = jnp.where(in_head, q2t, 0.0).astype(MXU_DTYPE)
            qts.append(qt)
            gate = jnp.where(past, _dot(km_hi, qt) + _dot(km_lo, qt), -jnp.inf)
            sel = jnp.zeros((nb, blk), F32)
            for _ in range(MOBA_TOPK):
                mx = jnp.max(gate, axis=0, keepdims=True)
                idx = jnp.min(jnp.where(gate == mx, blk_id, float(nb)), axis=0, keepdims=True)
                pick = blk_id == idx
                sel = jnp.where(pick, jnp.where(past, 1.0, sel), sel)
                gate = jnp.where(pick, -jnp.inf, gate)
            sel_ref[g * hpg + h] = sel

    def v_t(jb, hd):
        return vt_ref[jb, hd * HEAD_DIM:(hd + 1) * HEAD_DIM, :]

    def scores_to_scratch(g, kg, nk):
        for h in range(hpg):
            s_ref[g % 2, h, 0:nk * blk, :] = _dot(kg, qts[g * hpg + h])

    def run_groups(key_rows, nk, update):
        scores_to_scratch(0, k_ref[0, key_rows, lanes(0)], nk)
        for g in range(n_groups):
            if g + 1 < n_groups:
                scores_to_scratch(g + 1, k_ref[0, key_rows, lanes(g + 1)], nk)
            for h in range(hpg):
                update(g * hpg + h, s_ref[g % 2, h, 0:nk * blk, :])

    key_pos = lax.broadcasted_iota(I32, (blk, blk), 0)
    qry_pos = lax.broadcasted_iota(I32, (blk, blk), 1)

    def init(hd, s_raw):
        s = jnp.where(key_pos <= qry_pos, s_raw, NEG)
        m = jnp.max(s, axis=0, keepdims=True)
        p = jnp.exp2(s - m)
        m_ref[hd] = m
        l_ref[hd] = jnp.sum(p, axis=0, keepdims=True)
        acc_ref[hd] = _dot(v_t(qb, hd), p.astype(MXU_DTYPE))

    run_groups(pl.ds(pl.multiple_of(qb * blk, blk), blk), 1, init)

    def make_step(nk):
        def step(b0):
            def update(hd, s_raw):
                parts = [jnp.where(sel_ref[hd, pl.ds(b0 + t, 1), :] > 0.5,
                                   s_raw[t * blk:(t + 1) * blk], NEG) for t in range(nk)]
                s = jnp.concatenate(parts, axis=0) if nk > 1 else parts[0]
                m = m_ref[hd]
                m_new = jnp.maximum(m, jnp.max(s, axis=0, keepdims=True))
                alpha = jnp.exp2(m - m_new)
                p = jnp.exp2(s - m_new)
                vt = [v_t(b0 + t, hd) for t in range(nk)]
                vt = jnp.concatenate(vt, axis=1) if nk > 1 else vt[0]
                m_ref[hd] = m_new
                l_ref[hd] = alpha * l_ref[hd] + jnp.sum(p, axis=0, keepdims=True)
                acc_ref[hd] = alpha * acc_ref[hd] + _dot(vt, p.astype(MXU_DTYPE))

            run_groups(pl.ds(pl.multiple_of(b0 * blk, blk), nk * blk), nk, update)
        return step

    quad, pair, single = make_step(4), make_step(2), make_step(1)
    n_quads = lax.shift_right_logical(qb, 2)
    lax.fori_loop(0, n_quads, lambda i, c: (quad(4 * i), c)[1], 0)
    lax.fori_loop(0, lax.shift_right_logical(qb, 1) & 1, lambda i, c: (pair(4 * n_quads), c)[1], 0)
    lax.fori_loop(0, qb & 1, lambda i, c: (single(qb - 1), c)[1], 0)

    for g in range(n_groups):
        outs = [acc_ref[g * hpg + h] / l_ref[g * hpg + h] for h in range(hpg)]
        o_ref[0, :, lanes(g)] = jnp.concatenate(outs, axis=0).T.astype(o_ref.dtype)


def _attention(q, k, v, km, *, nbatch, seq, aw):
    blk = MOBA_BLOCK
    nb = seq // blk
    n_heads = aw // HEAD_DIM
    max_nk = 4
    assert nb % SUBLANES == 0
    q3, k3, v3 = (t.reshape(nbatch, seq, aw) for t in (q, k, v))
    out = pl.pallas_call(
        _attn_kernel,
        grid=(nbatch, nb),
        in_specs=[
            pl.BlockSpec((1, blk, aw), lambda b, i: (b, i, 0)),
            pl.BlockSpec((1, seq, aw), lambda b, i: (b, 0, 0)),
            pl.BlockSpec((1, seq, aw), lambda b, i: (b, 0, 0)),
            pl.BlockSpec((1, nb, aw), lambda b, i: (b, 0, 0)),
        ],
        out_specs=pl.BlockSpec((1, blk, aw), lambda b, i: (b, i, 0)),
        out_shape=jax.ShapeDtypeStruct((nbatch, seq, aw), MXU_DTYPE),
        scratch_shapes=[
            pltpu.VMEM((nb, aw, blk), MXU_DTYPE),
            pltpu.VMEM((n_heads, nb, blk), F32),
            pltpu.VMEM((2, HEADS_PER_LANE_GROUP, max_nk * blk, blk), F32),
            pltpu.VMEM((n_heads, 1, blk), F32),
            pltpu.VMEM((n_heads, 1, blk), F32),
            pltpu.VMEM((n_heads, HEAD_DIM, blk), F32),
        ],
        compiler_params=pltpu.CompilerParams(
            dimension_semantics=("arbitrary", "arbitrary"), vmem_limit_bytes=VMEM_LIMIT_BYTES),
        name="moba_attn",
    )(q3, k3, v3, km)
    return out.reshape(nbatch * seq, aw)


def _post_kernel(attn_ref, yc_ref, ga_ref, gc_ref, x_ref, wao_ref, wco_ref, wo_ref,
                 g1_ref, b1_ref, wrh_ref, wrl_ref, rb_ref,
                 h1_ref, h1w_ref, ti_ref, tw_ref, rk_ref, cnt_ref, cnt_acc, *, alpha):
    i = pl.program_id(0)
    tm = x_ref.shape[0]
    ne = wrh_ref.shape[0]

    y_attn = _dot(attn_ref[...], wao_ref[...])
    y_conv = _dot(yc_ref[...], wco_ref[...])
    merged = ga_ref[...] * y_attn + gc_ref[...] * y_conv
    mix = _dot(merged.astype(MXU_DTYPE), wo_ref[...])
    h1 = _layer_norm(alpha * x_ref[...] + mix, g1_ref[...], b1_ref[...])
    h1_ref[...] = h1
    _tt_store(h1w_ref, _pack_pairs(h1))

    h_hi, h_lo = _split_bf16(h1)
    logits = _dot_nt(wrh_ref[...], h_hi) + _dot_nt(wrh_ref[...], h_lo) + _dot_nt(wrl_ref[...], h_hi)
    scores = jax.nn.sigmoid(logits)
    choice = scores + rb_ref[...]

    gsz = ne // N_GROUPS
    gshape = (N_GROUPS, gsz, tm)
    c3 = choice.reshape(gshape)
    in_grp = lax.broadcasted_iota(I32, gshape, 1).astype(F32)
    m1 = jnp.max(c3, axis=1, keepdims=True)
    i1 = jnp.min(jnp.where(c3 == m1, in_grp, float(gsz)), axis=1, keepdims=True)
    m2 = jnp.max(jnp.where(in_grp == i1, -jnp.inf, c3), axis=1, keepdims=True)
    gscore = jnp.broadcast_to(m1 + m2, gshape).reshape(ne, tm)

    eid = lax.broadcasted_iota(I32, (ne, tm), 0).astype(F32)
    gid = lax.broadcasted_iota(I32, gshape, 0).astype(F32).reshape(ne, tm)
    cand = jnp.full((ne, tm), -jnp.inf, F32)
    for _ in range(TOPK_GROUPS):
        mx = jnp.max(gscore, axis=0, keepdims=True)
        idx = jnp.min(jnp.where(gscore == mx, gid, float(N_GROUPS)), axis=0, keepdims=True)
        pick = gid == idx
        cand = jnp.where(pick, choice, cand)
        gscore = jnp.where(pick, -jnp.inf, gscore)

    selmat = jnp.zeros((ne, tm), F32)
    idxs, svals = [], []
    for _ in range(TOP_K):
        mx = jnp.max(cand, axis=0, keepdims=True)
        idx = jnp.min(jnp.where(cand == mx, eid, float(ne)), axis=0, keepdims=True)
        pick = eid == idx
        svals.append(jnp.sum(jnp.where(pick, scores, 0.0), axis=0, keepdims=True))
        idxs.append(idx)
        selmat = jnp.where(pick, 1.0, selmat)
        cand = jnp.where(pick, -jnp.inf, cand)
    ssum = svals[0]
    for r in range(1, TOP_K):
        ssum = ssum + svals[r]

    @pl.when(i == 0)
    def _():
        cnt_acc[...] = jnp.zeros(cnt_acc.shape, F32)

    tr = lax.broadcasted_iota(I32, (tm, tm), 0)
    tc = lax.broadcasted_iota(I32, (tm, tm), 1)
    upper = jnp.where(tr < tc, 1.0, 0.0).astype(MXU_DTYPE)
    selb = selmat.astype(MXU_DTYPE)
    base = cnt_acc[...]
    rank = _dot(selb, upper) + jnp.concatenate([base] * (tm // LANES), axis=1)
    new_cnt = base + _dot(selb, jnp.ones((tm, LANES), MXU_DTYPE))
    cnt_acc[...] = new_cnt
    cnt_ref[...] = new_cnt

    for r in range(TOP_K):
        pick = eid == idxs[r]
        ti_ref[r:r + 1, :] = idxs[r].astype(I32)
        tw_ref[r:r + 1, :] = svals[r] / ssum * ROUTED_SCALE
        rk_ref[r:r + 1, :] = jnp.sum(jnp.where(pick, rank, 0.0), axis=0, keepdims=True).astype(I32)


def _post(attn, yc, ga, gc, x2, wao, wco, wo, g1, b1, wrh, wrl, rb, *, alpha):
    T, D = x2.shape
    tm = TM_POST
    assert D == 2 * TOK_ROWS * LANES
    aw, cwid = attn.shape[1], yc.shape[1]
    ne = wrh.shape[0]
    row = lambda i: (i, 0)
    full = lambda i: (0, 0)
    col = lambda i: (0, i)
    return pl.pallas_call(
        functools.partial(_post_kernel, alpha=alpha),
        grid=(T // tm,),
        in_specs=[
            pl.BlockSpec((tm, aw), row), pl.BlockSpec((tm, cwid), row),
            pl.BlockSpec((tm, D), row), pl.BlockSpec((tm, D), row), pl.BlockSpec((tm, D), row),
            pl.BlockSpec(wao.shape, full), pl.BlockSpec(wco.shape, full), pl.BlockSpec(wo.shape, full),
            pl.BlockSpec(g1.shape, full), pl.BlockSpec(b1.shape, full),
            pl.BlockSpec(wrh.shape, full), pl.BlockSpec(wrl.shape, full), pl.BlockSpec(rb.shape, full),
        ],
        out_specs=[
            pl.BlockSpec((tm, D), row),
            pl.BlockSpec((tm * TOK_ROWS, LANES), row),
            pl.BlockSpec((TOP_K, tm), col), pl.BlockSpec((TOP_K, tm), col), pl.BlockSpec((TOP_K, tm), col),
            pl.BlockSpec((ne, LANES), full),
        ],
        out_shape=[
            jax.ShapeDtypeStruct((T, D), F32),
            jax.ShapeDtypeStruct((T * TOK_ROWS, LANES), U32),
            jax.ShapeDtypeStruct((TOP_K, T), I32),
            jax.ShapeDtypeStruct((TOP_K, T), F32),
            jax.ShapeDtypeStruct((TOP_K, T), I32),
            jax.ShapeDtypeStruct((ne, LANES), F32),
        ],
        scratch_shapes=[pltpu.VMEM((ne, LANES), F32)],
        compiler_params=pltpu.CompilerParams(
            dimension_semantics=("arbitrary",), vmem_limit_bytes=VMEM_LIMIT_BYTES),
        name="post_route",
    )(attn, yc, ga, gc, x2, wao, wco, wo, g1, b1, wrh, wrl, rb)


def _slots_kernel(pstart_ref, ti_ref, rk_ref, dest_ref):
    ti = ti_ref[...]
    start = lax.fori_loop(0, N_EXPERTS, lambda e, acc: jnp.where(ti == e, pstart_ref[e], acc),
                          jnp.zeros(ti.shape, I32), unroll=8)
    dest_ref[...] = start + rk_ref[...]


def _slots(pstart, topi, rnk):
    T = topi.shape[1]
    tc = min(T, TC_SLOTS)
    blk = pl.BlockSpec((TOP_K, tc), lambda i, *_: (0, i))
    return pl.pallas_call(
        _slots_kernel,
        grid_spec=pltpu.PrefetchScalarGridSpec(
            num_scalar_prefetch=1, grid=(T // tc,), in_specs=[blk, blk], out_specs=blk),
        out_shape=jax.ShapeDtypeStruct((TOP_K, T), I32),
        compiler_params=pltpu.CompilerParams(dimension_semantics=("arbitrary",)),
        name="slots",
    )(pstart, topi, rnk)


def _dispatch_kernel(zrow_ref, nzero_ref, dest_ref, h1_ref, xs_hbm, zbuf, zsem, sem):
    i = pl.program_id(0)
    tm = dest_ref.shape[1]
    zrows = zbuf.shape[0]

    def zero_copy(z):
        start = pl.multiple_of(zrow_ref[z] * TOK_ROWS, zrows)
        return pltpu.make_async_copy(zbuf, xs_hbm.at[pl.ds(start, zrows)], zsem)

    @pl.when(i == 0)
    def _():
        zbuf[...] = jnp.zeros(zbuf.shape, zbuf.dtype)
        nz = nzero_ref[0]
        lax.fori_loop(0, nz, lambda z, c: (zero_copy(z).start(), c)[1], 0)
        lax.fori_loop(0, nz, lambda z, c: (zero_copy(z).wait(), c)[1], 0)

    def copy(k, j):
        return _tile_copy(h1_ref, xs_hbm, j, dest_ref[k, j], sem)

    def issue(j, c):
        for k in range(TOP_K):
            copy(k, j).start(priority=k % 2)
        return c

    def drain(j, c):
        for k in range(TOP_K):
            copy(k, j).wait()
        return c

    lax.fori_loop(0, tm, issue, 0)
    lax.fori_loop(0, tm, drain, 0)


def _dispatch(zrow, nzero, dest, h1, *, n_rows):
    T = h1.shape[0] // TOK_ROWS
    tm = TM_DISPATCH
    return pl.pallas_call(
        _dispatch_kernel,
        grid_spec=pltpu.PrefetchScalarGridSpec(
            num_scalar_prefetch=2,
            grid=(T // tm,),
            in_specs=[pl.BlockSpec((TOP_K, tm), lambda i, *_: (0, i), memory_space=pltpu.SMEM),
                      pl.BlockSpec((tm * TOK_ROWS, LANES), lambda i, *_: (i, 0))],
            out_specs=pl.BlockSpec(memory_space=pl.ANY),
            scratch_shapes=[pltpu.VMEM((TM_EXPERT // 2 * TOK_ROWS, LANES), h1.dtype),
                            pltpu.SemaphoreType.DMA, pltpu.SemaphoreType.DMA],
        ),
        out_shape=jax.ShapeDtypeStruct((n_rows * TOK_ROWS, LANES), h1.dtype),
        compiler_params=pltpu.CompilerParams(
            dimension_semantics=("arbitrary",), vmem_limit_bytes=VMEM_LIMIT_BYTES),
        name="dispatch",
    )(zrow, nzero, dest, h1)


def _experts_kernel(be_ref, nused_ref, bvalid_ref, first_ref, eslot_ref, nexte_ref,
                    xs_ref, wg_hbm, wu_hbm, wd_hbm, ys_ref, wg_f, wu_f, wd_f, wgu_b, wd_b, wsem):
    i = pl.program_id(0)
    ed = wg_hbm.shape[2]
    te = xs_ref.shape[0] // TOK_ROWS

    def weight_copies(e, slot):
        return [pltpu.make_async_copy(src.at[e], dst.at[slot], wsem.at[slot])
                for src, dst in ((wg_hbm, wg_f), (wu_hbm, wu_f), (wd_hbm, wd_f))]

    def swiglu_rows(n_tok):
        x_lo, x_hi = _unpack_pairs(_tt_load(xs_ref, n_tok))
        xb = jnp.concatenate([x_lo.astype(MXU_DTYPE), x_hi.astype(MXU_DTYPE)], axis=1)
        gu = _dot(xb, wgu_b[...])
        hb = _silu(gu[:, 0:ed]) * gu[:, ed:2 * ed]
        _tt_store(ys_ref, _pack_pairs(_dot(hb.astype(MXU_DTYPE), wd_b[...])))

    @pl.when(i < nused_ref[0])
    def _():
        @pl.when(first_ref[i] == 1)
        def _():
            slot = eslot_ref[i]

            @pl.when(i == 0)
            def _():
                for c in weight_copies(be_ref[i], slot):
                    c.start()

            for c in weight_copies(be_ref[i], slot):
                c.wait()
            wgu_b[:, 0:ed] = wg_f[slot].astype(MXU_DTYPE)
            wgu_b[:, ed:2 * ed] = wu_f[slot].astype(MXU_DTYPE)
            wd_b[...] = wd_f[slot].astype(MXU_DTYPE)

            @pl.when(nexte_ref[i] >= 0)
            def _():
                for c in weight_copies(nexte_ref[i], 1 - slot):
                    c.start()

        nvalid = bvalid_ref[i]

        @pl.when(nvalid > te // 2)
        def _():
            swiglu_rows(te)

        @pl.when(nvalid <= te // 2)
        def _():
            swiglu_rows(te // 2)
            rest = ys_ref.shape[0] // 2
            ys_ref[rest:, :] = jnp.zeros((rest, ys_ref.shape[1]), ys_ref.dtype)


def _experts(sched, xs, wg, wu, wd):
    te = TM_EXPERT
    D, ed = wg.shape[1], wg.shape[2]
    nblk = xs.shape[0] // (te * TOK_ROWS)
    blk_map = lambda i, be, nu, *_: (jnp.minimum(i, nu[0] - 1), 0)
    hbm = pl.BlockSpec(memory_space=pl.ANY)
    return pl.pallas_call(
        _experts_kernel,
        grid_spec=pltpu.PrefetchScalarGridSpec(
            num_scalar_prefetch=len(sched),
            grid=(nblk,),
            in_specs=[pl.BlockSpec((te * TOK_ROWS, LANES), blk_map), hbm, hbm, hbm],
            out_specs=pl.BlockSpec((te * TOK_ROWS, LANES), blk_map),
            scratch_shapes=[
                pltpu.VMEM((2, D, ed), wg.dtype), pltpu.VMEM((2, D, ed), wu.dtype),
                pltpu.VMEM((2, ed, D), wd.dtype),
                pltpu.VMEM((D, 2 * ed), MXU_DTYPE), pltpu.VMEM((ed, D), MXU_DTYPE),
                pltpu.SemaphoreType.DMA((2,)),
            ],
        ),
        out_shape=jax.ShapeDtypeStruct(xs.shape, xs.dtype),
        compiler_params=pltpu.CompilerParams(
            dimension_semantics=("arbitrary",), vmem_limit_bytes=VMEM_LIMIT_BYTES),
        name="experts",
    )(*sched, xs, wg, wu, wd)


def _final_kernel(dest_ref, dest_next_ref, tw_ref, h1_ref, p_ref, ys_hbm, wsgu_ref, wsd_ref, wpg_ref,
                  wpp_ref, g2_ref, b2_ref, g3_ref, b3_ref, o_ref, gbuf_a, gbuf_b, routed_ref, sems, *, alpha):
    i = pl.program_id(0)
    n = pl.num_programs(0)
    tm = o_ref.shape[0]
    sd = wsd_ref.shape[0]
    half = routed_ref.shape[1] // 2
    bufs = (gbuf_a, gbuf_b)

    def copy(idx_ref, slot, k, j):
        return _tile_copy(ys_hbm, bufs[slot].at[k], idx_ref[k, j], j, sems.at[slot])

    def gather_loop(idx_ref, slot, wait):
        def body(j, c):
            for k in range(TOP_K):
                if wait:
                    copy(idx_ref, slot, k, j).wait()
                else:
                    copy(idx_ref, slot, k, j).start(priority=k % 2)
            return c
        lax.fori_loop(0, tm, body, 0)

    @pl.when(i == 0)
    def _():
        gather_loop(dest_ref, 0, wait=False)

    def tile(cur):
        nxt = 1 - cur
        gbuf = bufs[cur]
        n_grp = tm // SUBLANES
        early = n_grp // 4

        def start_group(g):
            for j in range(g * SUBLANES, (g + 1) * SUBLANES):
                for k in range(TOP_K):
                    copy(dest_next_ref, nxt, k, j).start(priority=k % 2)

        for g in range(early):
            start_group(g)
        h1 = h1_ref[...]
        gu = _dot(h1.astype(MXU_DTYPE), wsgu_ref[...])
        shared = _dot((_silu(gu[:, 0:sd]) * gu[:, sd:2 * sd]).astype(MXU_DTYPE), wsd_ref[...])

        gather_loop(dest_ref, cur, wait=True)

        tw = tw_ref[...]
        wt = jnp.concatenate([tw, jnp.zeros((LANES - TOP_K, tm), F32)], axis=0).T
        for r in range(n_grp):
            if r + early < n_grp:
                start_group(r + early)
            wr = wt[r * SUBLANES:(r + 1) * SUBLANES, :]
            wb = [jnp.broadcast_to(wr[:, k:k + 1], (SUBLANES, LANES)) for k in range(TOP_K)]
            for s in range(TOK_ROWS):
                rows = pl.ds(r * SUBLANES * TOK_ROWS + s, SUBLANES, stride=TOK_ROWS)
                acc_lo = acc_hi = None
                for k in range(TOP_K):
                    y_lo, y_hi = _unpack_pairs(gbuf[k, rows, :])
                    acc_lo = wb[k] * y_lo if k == 0 else acc_lo + wb[k] * y_lo
                    acc_hi = wb[k] * y_hi if k == 0 else acc_hi + wb[k] * y_hi
                tok = slice(r * SUBLANES, (r + 1) * SUBLANES)
                routed_ref[tok, s * LANES:(s + 1) * LANES] = acc_lo
                routed_ref[tok, half + s * LANES:half + (s + 1) * LANES] = acc_hi
        routed = routed_ref[...]

        h2 = _layer_norm(alpha * h1 + (routed + shared), g2_ref[...], b2_ref[...])
        gate = jax.nn.sigmoid(_dot(h2.astype(MXU_DTYPE), wpg_ref[...]))
        ple = gate * _dot(p_ref[...].astype(MXU_DTYPE), wpp_ref[...])
        o_ref[...] = _layer_norm(alpha * h2 + ple, g3_ref[...], b3_ref[...])

        @pl.when(i == n - 1)
        def _():
            gather_loop(dest_next_ref, nxt, wait=True)

    for parity in range(2):
        pl.when(i % 2 == parity)(functools.partial(tile, parity))


def _final(dest, topw, h1, p2, ys, wsgu, wsd, wpg, wpp, g2, b2, g3, b3, *, alpha):
    T, D = p2.shape[0], wpg.shape[0]
    tm = TM_FINAL
    n_tiles = T // tm
    row = lambda i: (i, 0)
    full = lambda i: (0, 0)
    col = lambda i: (0, i)
    col_next = lambda i: (0, jnp.minimum(i + 1, n_tiles - 1))
    return pl.pallas_call(
        functools.partial(_final_kernel, alpha=alpha),
        grid=(n_tiles,),
        in_specs=[
            pl.BlockSpec((TOP_K, tm), col, memory_space=pltpu.SMEM),
            pl.BlockSpec((TOP_K, tm), col_next, memory_space=pltpu.SMEM),
            pl.BlockSpec((TOP_K, tm), col),
            pl.BlockSpec((tm, D), row),
            pl.BlockSpec((tm, p2.shape[1]), row),
            pl.BlockSpec(memory_space=pl.ANY),
            pl.BlockSpec(wsgu.shape, full), pl.BlockSpec(wsd.shape, full),
            pl.BlockSpec(wpg.shape, full), pl.BlockSpec(wpp.shape, full),
            pl.BlockSpec(g2.shape, full), pl.BlockSpec(b2.shape, full),
            pl.BlockSpec(g3.shape, full), pl.BlockSpec(b3.shape, full),
        ],
        out_specs=pl.BlockSpec((tm, D), row),
        out_shape=jax.ShapeDtypeStruct((T, D), F32),
        scratch_shapes=[pltpu.VMEM((TOP_K, tm * TOK_ROWS, LANES), ys.dtype),
                        pltpu.VMEM((TOP_K, tm * TOK_ROWS, LANES), ys.dtype),
                        pltpu.VMEM((tm, D), F32), pltpu.SemaphoreType.DMA((2,))],
        compiler_params=pltpu.CompilerParams(
            dimension_semantics=("arbitrary",), vmem_limit_bytes=VMEM_LIMIT_BYTES),
        name="final",
    )(dest, dest, topw, h1, p2, ys, wsgu, wsd, wpg, wpp, g2, b2, g3, b3)


def _rope_tables(seq):
    half = ROT_DIM // 2
    f32 = np.float32
    inv_freq = f32(ROPE_THETA) ** (-np.arange(0, ROT_DIM, 2, dtype=f32) / f32(ROT_DIM))
    ang = np.arange(seq, dtype=np.int32).astype(f32)[:, None] * inv_freq[None, :]
    cos, sin = np.cos(ang), np.sin(ang)
    ones = np.ones((seq, HEAD_DIM - ROT_DIM), f32)
    zeros = np.zeros((seq, HEAD_DIM - ROT_DIM), f32)
    zh = np.zeros((seq, half), f32)
    cos_h = np.concatenate([cos, cos, ones], axis=1)
    sa_h = np.concatenate([-sin, zh, zeros], axis=1)
    sb_h = np.concatenate([zh, sin, zeros], axis=1)
    rep = lambda t: jnp.asarray(np.concatenate([t] * HEADS_PER_LANE_GROUP, axis=1), F32)
    return rep(cos_h), rep(sa_h), rep(sb_h)


def _expert_layout(counts, n_blocks):
    te = TM_EXPERT
    nblk_e = (counts + te - 1) // te
    blk_end = jnp.cumsum(nblk_e)
    blk_start = blk_end - nblk_e
    nused = blk_end[-1]
    pstart = (blk_start * te).astype(I32)
    bid = jnp.arange(n_blocks, dtype=I32)
    block_e = jnp.sum(blk_end[None, :] <= jnp.minimum(bid, nused - 1)[:, None], axis=1).astype(I32)
    block_e = jnp.minimum(block_e, N_EXPERTS - 1)
    bvalid = jnp.clip(counts[block_e] - (bid - blk_start[block_e]) * te, 0, te).astype(I32)
    first = ((bid == blk_start[block_e]) & (bid < nused)).astype(I32)
    eslot = ((jnp.cumsum(first) - 1) & 1).astype(I32)
    after = blk_end[block_e]
    nexte = jnp.where(after < nused, block_e[jnp.minimum(after, n_blocks - 1)], -1).astype(I32)
    half = te // 2
    partial = (counts % half) != 0
    order = jnp.argsort(jnp.logical_not(partial), stable=True).astype(I32)
    nzero = jnp.sum(partial).astype(I32)
    zrow = (pstart + (counts // half) * half)[order].astype(I32)
    sched = (block_e, nused.astype(I32).reshape(1), bvalid, first, eslot, nexte)
    return pstart, sched, zrow, nzero.reshape(1)


def _layer(h, p2, w_in, conv_w, w_attn_out, w_conv_out, w_out, ln1_g, ln1_b, w_router, router_bias,
           w_exp_gate, w_exp_up, w_exp_down, w_sh_gate, w_sh_up, w_sh_down, ln2_g, ln2_b,
           w_ple_gate, w_ple_proj, ln3_g, ln3_b, *, nbatch, seq, alpha):
    T, D = h.shape
    aw = w_attn_out.shape[0]
    cwid = w_conv_out.shape[0]
    bf = lambda w: w.astype(MXU_DTYPE)
    rowv = lambda g: g.reshape(1, -1)

    cos_t, sa_t, sb_t = _rope_tables(seq)
    q, k, v, km, yc, ga, gc = _inproj(h, bf(w_in), cos_t, sa_t, sb_t, conv_w, seq=seq, aw=aw, cwid=cwid)
    attn = _attention(q, k, v, km, nbatch=nbatch, seq=seq, aw=aw)

    wr_t = w_router.astype(F32).T
    wrh = wr_t.astype(MXU_DTYPE)
    wrl = (wr_t - wrh.astype(F32)).astype(MXU_DTYPE)
    h1, h1w, topi, topw, rnk, cnt = _post(
        attn, yc, ga, gc, h, bf(w_attn_out), bf(w_conv_out), bf(w_out), rowv(ln1_g), rowv(ln1_b),
        wrh, wrl, router_bias.astype(F32).reshape(-1, 1), alpha=alpha)

    n_blocks = (T * TOP_K) // TM_EXPERT + N_EXPERTS
    counts = cnt[:, 0].astype(I32)
    pstart, sched, zrow, nzero = _expert_layout(counts, n_blocks)
    dest = _slots(pstart, topi, rnk)
    xs = _dispatch(zrow, nzero, dest, h1w, n_rows=n_blocks * TM_EXPERT)
    ys = _experts(sched, xs, w_exp_gate, w_exp_up, w_exp_down)

    wsgu = jnp.concatenate([bf(w_sh_gate), bf(w_sh_up)], axis=1)
    return _final(dest, topw, h1, p2, ys, wsgu, bf(w_sh_down), bf(w_ple_gate), bf(w_ple_proj),
                  rowv(ln2_g), rowv(ln2_b), rowv(ln3_g), rowv(ln3_b), alpha=alpha)


def kernel(x, p, w_in, conv_w, w_attn_out, w_conv_out, w_out, ln1_g, ln1_b, w_router, router_bias,
           w_exp_gate, w_exp_up, w_exp_down, w_sh_gate, w_sh_up, w_sh_down, ln2_g, ln2_b,
           w_ple_gate, w_ple_proj, ln3_g, ln3_b):
    nbatch, seq, d_model = x.shape
    depth = w_in.shape[0]
    alpha = (2 * depth) ** 0.25
    assert seq % MOBA_BLOCK == 0
    h = x.reshape(nbatch * seq, d_model)
    for i in range(depth):
        h = _layer(h, p[i].reshape(nbatch * seq, -1), w_in[i], conv_w[i], w_attn_out[i], w_conv_out[i],
                   w_out[i], ln1_g[i], ln1_b[i], w_router[i], router_bias[i],
                   w_exp_gate[i], w_exp_up[i], w_exp_down[i], w_sh_gate[i], w_sh_up[i], w_sh_down[i],
                   ln2_g[i], ln2_b[i], w_ple_gate[i], w_ple_proj[i], ln3_g[i], ln3_b[i],
                   nbatch=nbatch, seq=seq, alpha=alpha)
    return h.reshape(nbatch, seq, d_model)
```
